```python
import math
import jax, jax.numpy as jnp
from jax import lax
import numpy as np

D_MODEL = 2048
BATCH = 8
SEQ = 2048
DEPTH = 2
DEC_BATCH = 32
DEC_SEQ = 8
PAST_LEN = 8192
PAGE_SIZE = 128

MIX = D_MODEL
W_GROUP = MIX // 4
H_A = 4
DV_A = W_GROUP // H_A
DK_A = DV_A // 2
ROPE_DIM = DK_A // 4
ROPE_THETA = 500000.0
Q_BLOCK = 128
W_B = W_GROUP
P_B = 64
H_B = W_B // P_B
N_B = 128
G_B = 2
CONV_W = 4
SSD_CHUNK = 64
CONV_B_CH = W_B + 2 * G_B * N_B
H_C = 4
DV_C = W_GROUP // H_C
DK_C = DV_C // 2
GK_RANK = 16
GK_NORM = 16.0
GLA_CHUNK = 16
W_D = W_GROUP
NB_D = 8
BW_D = W_D // NB_D
LRU_C = 8.0
EPS = 1e-6

SPLITS = (
    2 * H_A * DK_A, 2 * H_A * DK_A, H_A * DV_A, W_GROUP,
    CONV_B_CH, H_B, W_B,
    H_C * DK_C, H_C * DK_C, H_C * DV_C, GK_RANK, W_GROUP,
    W_D, W_D,
)
IN_COLS = sum(SPLITS)

kernel_name = "hybrid_parallel_heads_decode_step"


def _split_points():
    pts, acc = [], 0
    for s in SPLITS[:-1]:
        acc += s
        pts.append(acc)
    return pts


def rmsnorm(x, w):
    xf = x.astype(jnp.float32)
    y = xf * lax.rsqrt(jnp.mean(xf * xf, axis=-1, keepdims=True) + EPS)
    return (y * w.astype(jnp.float32)).astype(x.dtype)


def rope_partial(x, pos):
    half = ROPE_DIM // 2
    inv = ROPE_THETA ** (-jnp.arange(half, dtype=jnp.float32) / half)
    ang = pos.astype(jnp.float32)[:, None] * inv[None, :]
    cos = jnp.cos(ang)[:, None, :]
    sin = jnp.sin(ang)[:, None, :]
    xr = x[..., :ROPE_DIM].astype(jnp.float32)
    x1, x2 = xr[..., :half], xr[..., half:]
    rot = jnp.concatenate([x1 * cos - x2 * sin, x2 * cos + x1 * sin], axis=-1)
    return jnp.concatenate([rot.astype(x.dtype), x[..., ROPE_DIM:]], axis=-1)


def causal_conv(x, buf, w, b):
    L = x.shape[1]
    xp = jnp.concatenate([buf.astype(x.dtype), x], axis=1)
    y = xp[:, 0:L] * w[0]
    for t in range(1, CONV_W):
        y = y + xp[:, t:t + L] * w[t]
    return y + b, xp[:, -(CONV_W - 1):]


def diff_attend(q, k, v, q_pos, k_pos, lam):
    s = jnp.einsum('bqhmd,bkhmd->bhmqk', q, k, preferred_element_type=jnp.float32) * (DK_A ** -0.5)
    mask = k_pos[None, :] <= q_pos[:, None]
    p = jax.nn.softmax(jnp.where(mask, s, -jnp.inf), axis=-1)
    attn = p[:, :, 0] - lam * p[:, :, 1]
    return jnp.einsum('bhqk,bkhv->bqhv', attn.astype(v.dtype), v)


def diff_attention(q, k, v, q_pos, k_pos, lam):
    Bsz, L = q.shape[:2]
    blk = math.gcd(L, Q_BLOCK)
    nb = L // blk
    qb = jnp.moveaxis(q.reshape((Bsz, nb, blk) + q.shape[2:]), 1, 0)
    pb = q_pos.reshape(nb, blk)
    o = lax.map(lambda qp: diff_attend(qp[0], k, v, qp[1], k_pos, lam), (qb, pb))
    return jnp.moveaxis(o, 0, 1).reshape(Bsz, L, H_A, DV_A)


def ssd_scan(x, dt, a, bm, cm, s0, chunk):
    Bsz, L = x.shape[:2]
    nc = L // chunk
    rep = H_B // G_B
    bm = jnp.repeat(bm, rep, axis=2)
    cm = jnp.repeat(cm, rep, axis=2)
    tri = jnp.tril(jnp.ones((chunk, chunk), dtype=bool))[None, :, :, None]

    def to_chunks(t):
        return jnp.moveaxis(t.reshape((Bsz, nc, chunk) + t.shape[2:]), 1, 0)

    def step(s, inp):
        xc, dtc, bc, cc = inp
        cum = jnp.cumsum(dtc * a, axis=1)
        seg = cum[:, :, None, :] - cum[:, None, :, :]
        decay = jnp.exp(jnp.where(tri, seg, -jnp.inf))
        cb = jnp.einsum('bihn,bjhn->bijh', cc, bc)
        y = jnp.einsum('bijh,bjh,bjhp->bihp', cb * decay, dtc, xc)
        y = y + jnp.einsum('bihn,bhpn->bihp', cc * jnp.exp(cum)[..., None], s)
        last = cum[:, -1]
        wj = jnp.exp(last[:, None, :] - cum) * dtc
        s = jnp.exp(last)[:, :, None, None] * s + jnp.einsum('bjh,bjhn,bjhp->bhpn', wj, bc, xc)
        return s, y

    s, ys = lax.scan(step, s0, (to_chunks(x), to_chunks(dt), to_chunks(bm), to_chunks(cm)))
    return jnp.moveaxis(ys, 0, 1).reshape(Bsz, L, H_B, P_B), s


def gla_scan(q, k, v, g, s0, chunk):
    Bsz, L = q.shape[:2]
    nc = L // chunk
    tri = jnp.tril(jnp.ones((chunk, chunk), dtype=bool))[None, :, :, None, None]

    def to_chunks(t):
        return jnp.moveaxis(t.reshape((Bsz, nc, chunk) + t.shape[2:]), 1, 0)

    def step(s, inp):
        qc, kc, vc, gc = inp
        b = jnp.cumsum(gc, axis=1)
        seg = b[:, :, None] - b[:, None, :]
        decay = jnp.exp(jnp.where(tri, seg, -jnp.inf))
        att = jnp.einsum('bihd,bjhd,bijhd->bijh', qc, kc, decay)
        o = jnp.einsum('bijh,bjhv->bihv', att, vc)
        o = o + jnp.einsum('bihd,bhdv->bihv', qc * jnp.exp(b), s)
        last = b[:, -1]
        s = jnp.exp(last)[..., None] * s + jnp.einsum('bjhd,bjhv->bhdv', kc * jnp.exp(last[:, None] - b), vc)
        return s, o

    s, os_ = lax.scan(step, s0, (to_chunks(q), to_chunks(k), to_chunks(v), to_chunks(g)))
    return jnp.moveaxis(os_, 0, 1).reshape(Bsz, L, H_C, DV_C), s


def rglru(xr, h0, wa, ba, wx, bx, lam):
    Bsz, L, _ = xr.shape
    xb = xr.reshape(Bsz, L, NB_D, BW_D)
    r = jax.nn.sigmoid(jnp.einsum('blnc,ncd->blnd', xb, wa).reshape(Bsz, L, W_D) + ba)
    i = jax.nn.sigmoid(jnp.einsum('blnc,ncd->blnd', xb, wx).reshape(Bsz, L, W_D) + bx)
    log_a = -LRU_C * r.astype(jnp.float32) * jax.nn.softplus(-lam.astype(jnp.float32))
    a = jnp.exp(log_a)
    u = jnp.sqrt(-jnp.expm1(2.0 * log_a)) * (i * xr).astype(jnp.float32)
    u = u.at[:, 0].add(a[:, 0] * h0.astype(jnp.float32))

    def comb(e1, e2):
        a1, b1 = e1
        a2, b2 = e2
        return a1 * a2, a2 * b1 + b2

    _, h = lax.associative_scan(comb, (a, u), axis=1)
    return h, h[:, -1]


def hybrid_layer(x, pos, layer_idx, p, st, past_k, past_v):
    (norm_w, w_in, w_out, lam_q1, lam_k1, lam_q2, lam_k2, subln_w,
     ssd_conv_w, ssd_conv_b, ssd_dt_bias, ssd_a_log, ssd_d, ssd_norm_w,
     gla_gk_w2, gla_gk_b, gla_norm_w,
     lru_conv_w, lru_conv_b, lru_wa, lru_ba, lru_wx, lru_bx, lru_lambda) = p
    ssd_s0, ssd_buf, gla_s0, lru_h0, lru_buf = st
    f32 = jnp.float32
    Bsz, L, _ = x.shape
    u = rmsnorm(x, norm_w)
    proj = u @ w_in
    (qa, ka, va, za, xbc, dt_raw, zb, qc, kc, vc, gk_lr, gc, xd, zd) = jnp.split(proj, _split_points(), axis=-1)

    lam_init = 0.8 - 0.6 * math.exp(-0.3 * layer_idx)
    lam = (jnp.exp(jnp.sum(lam_q1.astype(f32) * lam_k1.astype(f32)))
           - jnp.exp(jnp.sum(lam_q2.astype(f32) * lam_k2.astype(f32))) + lam_init)
    q = rope_partial(qa.reshape(Bsz, L, 2 * H_A, DK_A), pos).reshape(Bsz, L, H_A, 2, DK_A)
    k = rope_partial(ka.reshape(Bsz, L, 2 * H_A, DK_A), pos).reshape(Bsz, L, H_A, 2, DK_A)
    v = va.reshape(Bsz, L, H_A, DV_A)
    if past_k is None:
        kk, vv, k_pos = k, v, pos
    else:
        past_len = past_k.shape[1]
        kk = jnp.concatenate([past_k.reshape(Bsz, past_len, H_A, 2, DK_A).astype(k.dtype), k], axis=1)
        vv = jnp.concatenate([past_v.astype(v.dtype), v], axis=1)
        k_pos = jnp.concatenate([jnp.arange(past_len, dtype=pos.dtype), pos])
    o_a = diff_attention(q, kk, vv, pos, k_pos, lam)
    o_a = rmsnorm(o_a, subln_w) * (1.0 - lam_init)
    y_a = o_a.reshape(Bsz, L, W_GROUP) * jax.nn.silu(za)

    xbc, ssd_buf_new = causal_conv(xbc, ssd_buf, ssd_conv_w, ssd_conv_b)
    xbc = jax.nn.silu(xbc)
    xs, bm, cm = jnp.split(xbc, [W_B, W_B + G_B * N_B], axis=-1)
    dt = jax.nn.softplus(dt_raw.astype(f32) + ssd_dt_bias.astype(f32))
    a = -jnp.exp(ssd_a_log.astype(f32))
    xh = xs.reshape(Bsz, L, H_B, P_B)
    y_ssd, ssd_s = ssd_scan(xh.astype(f32), dt, a,
                            bm.reshape(Bsz, L, G_B, N_B).astype(f32),
                            cm.reshape(Bsz, L, G_B, N_B).astype(f32),
                            ssd_s0.astype(f32), math.gcd(L, SSD_CHUNK))
    y_ssd = y_ssd + ssd_d.astype(f32)[:, None] * xh.astype(f32)
    y_b = rmsnorm(y_ssd.reshape(Bsz, L, W_B).astype(x.dtype) * jax.nn.silu(zb), ssd_norm_w)

    gk = jax.nn.log_sigmoid((gk_lr @ gla_gk_w2 + gla_gk_b).astype(f32)) / GK_NORM
    o_c, gla_s = gla_scan(qc.reshape(Bsz, L, H_C, DK_C).astype(f32) * (DK_C ** -0.5),
                          kc.reshape(Bsz, L, H_C, DK_C).astype(f32),
                          vc.reshape(Bsz, L, H_C, DV_C).astype(f32),
                          gk.reshape(Bsz, L, H_C, DK_C), gla_s0.astype(f32),
                          math.gcd(L, GLA_CHUNK))
    y_c = rmsnorm(o_c.astype(x.dtype), gla_norm_w).reshape(Bsz, L, W_GROUP) * jax.nn.silu(gc)

    xr, lru_buf_new = causal_conv(xd, lru_buf, lru_conv_w, lru_conv_b)
    h, lru_h = rglru(xr, lru_h0, lru_wa, lru_ba, lru_wx, lru_bx, lru_lambda)
    y_d = h.astype(x.dtype) * jax.nn.silu(zd)

    out = jnp.concatenate([y_a, y_b, y_c, y_d], axis=-1) @ w_out
    new_k = k.reshape(Bsz, L, H_A, 2 * DK_A)
    new_state = (new_k, v, ssd_s.astype(x.dtype), ssd_buf_new, gla_s.astype(x.dtype),
                 lru_h.astype(x.dtype), lru_buf_new)
    return x + out, new_state


def setup_inputs(seed: int = 0) -> dict:
    key = jax.random.key(seed)
    ks = iter(jax.random.split(key, 48))
    f32 = jnp.float32

    def nrm(shape, scale):
        return jax.random.normal(next(ks), shape, f32) * scale

    def gain(shape):
        return 1.0 + nrm(shape, 0.02)

    n_pages = PAST_LEN // PAGE_SIZE
    n_used = DEC_BATCH * n_pages
    n_pool = n_used + n_used // 4 + 1
    page_table = jax.random.permutation(next(ks), n_pool)[:n_used].reshape(DEC_BATCH, n_pages).astype(jnp.int32)

    dt0 = jnp.exp(jax.random.uniform(next(ks), (DEPTH, H_B), f32, math.log(1e-3), math.log(1e-1)))
    ssd_dt_bias = dt0 + jnp.log(-jnp.expm1(-dt0))
    ssd_a_log = jnp.log(jax.random.uniform(next(ks), (DEPTH, H_B), f32, 1.0, 16.0))
    a_target = jax.random.uniform(next(ks), (DEPTH, W_D), f32, 0.9, 0.999)
    a_base = a_target ** (1.0 / LRU_C)
    lru_lambda = jnp.log(a_base) - jnp.log1p(-a_base)

    return {
        'x_prompt': nrm((BATCH, SEQ, D_MODEL), 1.0),
        'x_sample': nrm((DEC_BATCH, DEC_SEQ, D_MODEL), 1.0),
        'cache_k': nrm((DEPTH, n_pool, PAGE_SIZE, H_A, 2 * DK_A), 1.0),
        'cache_v': nrm((DEPTH, n_pool, PAGE_SIZE, H_A, DV_A), 1.0),
        'page_table': page_table,
        'state_ssd': nrm((DEPTH, DEC_BATCH, H_B, P_B, N_B), 0.1),
        'state_ssd_conv': nrm((DEPTH, DEC_BATCH, CONV_W - 1, CONV_B_CH), 1.0),
        'state_gla': nrm((DEPTH, DEC_BATCH, H_C, DK_C, DV_C), 0.1),
        'state_lru': nrm((DEPTH, DEC_BATCH, W_D), 0.5),
        'state_lru_conv': nrm((DEPTH, DEC_BATCH, CONV_W - 1, W_D), 1.0),
        'norm_w': gain((DEPTH, D_MODEL)),
        'w_in': nrm((DEPTH, D_MODEL, IN_COLS), D_MODEL ** -0.5),
        'w_out': nrm((DEPTH, MIX, D_MODEL), MIX ** -0.5),
        'lam_q1': nrm((DEPTH, DK_A), 0.1),
        'lam_k1': nrm((DEPTH, DK_A), 0.1),
        'lam_q2': nrm((DEPTH, DK_A), 0.1),
        'lam_k2': nrm((DEPTH, DK_A), 0.1),
        'subln_w': gain((DEPTH, DV_A)),
        'ssd_conv_w': nrm((DEPTH, CONV_W, CONV_B_CH), CONV_W ** -0.5),
        'ssd_conv_b': nrm((DEPTH, CONV_B_CH), 0.02),
        'ssd_dt_bias': ssd_dt_bias,
        'ssd_a_log': ssd_a_log,
        'ssd_d': gain((DEPTH, H_B)),
        'ssd_norm_w': gain((DEPTH, W_B)),
        'gla_gk_w2': nrm((DEPTH, GK_RANK, H_C * DK_C), GK_RANK ** -0.5),
        'gla_gk_b': nrm((DEPTH, H_C * DK_C), 0.02),
        'gla_norm_w': gain((DEPTH, DV_C)),
        'lru_conv_w': nrm((DEPTH, CONV_W, W_D), CONV_W ** -0.5),
        'lru_conv_b': nrm((DEPTH, W_D), 0.02),
        'lru_wa': nrm((DEPTH, NB_D, BW_D, BW_D), BW_D ** -0.5),
        'lru_ba': nrm((DEPTH, W_D), 0.02),
        'lru_wx': nrm((DEPTH, NB_D, BW_D, BW_D), BW_D ** -0.5),
        'lru_bx': nrm((DEPTH, W_D), 0.02),
        'lru_lambda': lru_lambda,
        'final_norm_w': gain((D_MODEL,)),
    }


def reference(x_prompt, x_sample, cache_k, cache_v, page_table, state_ssd, state_ssd_conv,
              state_gla, state_lru, state_lru_conv, norm_w, w_in, w_out, lam_q1, lam_k1,
              lam_q2, lam_k2, subln_w, ssd_conv_w, ssd_conv_b, ssd_dt_bias, ssd_a_log, ssd_d,
              ssd_norm_w, gla_gk_w2, gla_gk_b, gla_norm_w, lru_conv_w, lru_conv_b, lru_wa,
              lru_ba, lru_wx, lru_bx, lru_lambda, final_norm_w):
    Bp, Lp, _ = x_prompt.shape
    Bs, Ls, _ = x_sample.shape
    past_len = page_table.shape[1] * cache_k.shape[2]
    pos_p = jnp.arange(Lp, dtype=jnp.int32)
    pos_s = past_len + jnp.arange(Ls, dtype=jnp.int32)
    dtp = x_prompt.dtype
    hp, hs = x_prompt, x_sample
    sp, ss = [], []
    for l in range(DEPTH):
        p = (norm_w[l], w_in[l], w_out[l], lam_q1[l], lam_k1[l], lam_q2[l], lam_k2[l], subln_w[l],
             ssd_conv_w[l], ssd_conv_b[l], ssd_dt_bias[l], ssd_a_log[l], ssd_d[l], ssd_norm_w[l],
             gla_gk_w2[l], gla_gk_b[l], gla_norm_w[l],
             lru_conv_w[l], lru_conv_b[l], lru_wa[l], lru_ba[l], lru_wx[l], lru_bx[l], lru_lambda[l])
        st_p = (jnp.zeros((Bp, H_B, P_B, N_B), jnp.float32),
                jnp.zeros((Bp, CONV_W - 1, CONV_B_CH), dtp),
                jnp.zeros((Bp, H_C, DK_C, DV_C), jnp.float32),
                jnp.zeros((Bp, W_D), jnp.float32),
                jnp.zeros((Bp, CONV_W - 1, W_D), dtp))
        hp, new_p = hybrid_layer(hp, pos_p, l, p, st_p, None, None)
        sp.append(new_p)
        past_k = cache_k[l, page_table].reshape(Bs, past_len, H_A, 2 * DK_A)
        past_v = cache_v[l, page_table].reshape(Bs, past_len, H_A, DV_A)
        st_s = (state_ssd[l], state_ssd_conv[l], state_gla[l], state_lru[l], state_lru_conv[l])
        hs, new_s = hybrid_layer(hs, pos_s, l, p, st_s, past_k, past_v)
        ss.append(new_s)
    y_prompt = rmsnorm(hp, final_norm_w)
    y_sample = rmsnorm(hs, final_norm_w)
    k_prompt = jnp.stack([s[0] for s in sp])
    v_prompt = jnp.stack([s[1] for s in sp])
    ssd_prompt = jnp.stack([s[2] for s in sp])
    ssd_conv_prompt = jnp.stack([s[3] for s in sp])
    gla_prompt = jnp.stack([s[4] for s in sp])
    lru_prompt = jnp.stack([s[5] for s in sp])
    lru_conv_prompt = jnp.stack([s[6] for s in sp])
    k_sample = jnp.stack([s[0] for s in ss])
    v_sample = jnp.stack([s[1] for s in ss])
    ssd_sample = jnp.stack([s[2] for s in ss])
    ssd_conv_sample = jnp.stack([s[3] for s in ss])
    gla_sample = jnp.stack([s[4] for s in ss])
    lru_sample = jnp.stack([s[5] for s in ss])
    lru_conv_sample = jnp.stack([s[6] for s in ss])
    return (y_prompt, y_sample, k_prompt, v_prompt, ssd_prompt, ssd_conv_prompt, gla_prompt,
            lru_prompt, lru_conv_prompt, k_sample, v_sample, ssd_sample, ssd_conv_sample,
            gla_sample, lru_sample, lru_conv_sample)
```

```python
import functools
import math

import jax
import jax.numpy as jnp
from jax import lax
from jax.experimental import pallas as pl
from jax.experimental.pallas import tpu as pltpu

F32 = jnp.float32
BF16 = jnp.bfloat16
EPS = 1e-6
NEG = -1e30

D_MODEL = 2048
W_GROUP = 512
H_A, DV_A, DK_A = 4, 128, 64
ROPE_DIM, ROPE_THETA = 16, 500000.0
H_B, P_B, N_B, G_B = 8, 64, 128, 2
CONV_W = 4
CONV_B_CH = W_GROUP + 2 * G_B * N_B
H_C, DK_C, DV_C = 4, 64, 128
GK_RANK, GK_NORM = 16, 16.0
W_D, NB_D, BW_D = 512, 8, 64
LRU_C = 8.0
DEPTH = 2

LANES = 128
SUBLANES = 8
COL_BLK = 512
VMEM_LIMIT = 56 * 1024 * 1024

_SPLITS = (512, 512, 512, 512, CONV_B_CH, H_B, W_GROUP, 256, 256, 512, GK_RANK, W_GROUP, W_D, W_D)
_OFF = [0]
for _s in _SPLITS:
    _OFF.append(_OFF[-1] + _s)
(_QA, _KA, _VA, _ZA, _XBC, _DT, _ZB, _QC, _KC, _VC, _GKLR, _GC, _XD, _ZD) = _OFF[:-1]

REST_COLS = 4608


def _cparams(sem):
    return pltpu.CompilerParams(dimension_semantics=sem, vmem_limit_bytes=VMEM_LIMIT)


def _sigmoid(x):
    return jax.nn.sigmoid(x)


def _silu(x):
    return x * _sigmoid(x)


def _softplus(x):
    return jnp.maximum(x, 0.0) + jnp.log1p(jnp.exp(-jnp.abs(x)))


def _rmsnorm_rows(x, w):
    return x * lax.rsqrt(jnp.mean(x * x, axis=-1, keepdims=True) + EPS) * w


def _dot(a, b):
    return jnp.dot(a, b, preferred_element_type=F32)


def _dot_nt(a, b):
    return lax.dot_general(a, b, (((1,), (1,)), ((), ())), preferred_element_type=F32)


def _dot_tn(a, b):
    return lax.dot_general(a, b, (((0,), (0,)), ((), ())), preferred_element_type=F32)


def _dot_exact(a, b):
    return jnp.dot(a, b, preferred_element_type=F32, precision=lax.Precision.HIGHEST)


def _lower_tri(n):
    r = lax.broadcasted_iota(jnp.int32, (n, n), 0)
    c = lax.broadcasted_iota(jnp.int32, (n, n), 1)
    return r >= c


def _inproj_kernel(x_ref, nw_ref, w_ref, ws_ref, rope_ref,
                   q_ref, k_ref, v_ref, rest_ref, small_ref, u_scr):
    j = pl.program_id(1)

    @pl.when(j == 0)
    def _():
        u = _rmsnorm_rows(x_ref[...], nw_ref[...]).astype(BF16)
        u_scr[...] = u
        small_ref[...] = _dot(u, ws_ref[...])

    acc = _dot(u_scr[...], w_ref[...])

    def rope_store(dst, scale):
        c, s1, s2 = rope_ref[0], rope_ref[1], rope_ref[2]
        for hh in range(COL_BLK // LANES):
            sl = slice(hh * LANES, (hh + 1) * LANES)
            a = acc[:, sl]
            r = (a * c + pltpu.roll(a, LANES - ROPE_DIM // 2, 1) * s1
                 + pltpu.roll(a, ROPE_DIM // 2, 1) * s2)
            dst[:, sl] = r * scale

    @pl.when(j == 0)
    def _():
        rope_store(q_ref, DK_A ** -0.5)

    @pl.when(j == 1)
    def _():
        rope_store(k_ref, 1.0)

    @pl.when(j == 2)
    def _():
        v_ref[...] = acc

    @pl.when(j >= 3)
    def _():
        rest_ref[...] = acc


def _inproj(x2d, norm_w, w_main, w_small, rope_tab, seq_len):
    T = x2d.shape[0]
    tm = min(512, T)
    assert T % tm == 0 and (seq_len % tm == 0 or tm % seq_len == 0)
    n_pos_blk = max(seq_len // tm, 1)
    nj = w_main.shape[1] // COL_BLK
    out_shape = (
        jax.ShapeDtypeStruct((T, COL_BLK), F32),
        jax.ShapeDtypeStruct((T, COL_BLK), F32),
        jax.ShapeDtypeStruct((T, COL_BLK), F32),
        jax.ShapeDtypeStruct((T, REST_COLS), F32),
        jax.ShapeDtypeStruct((T, LANES), F32),
    )
    row_blk = pl.BlockSpec((tm, COL_BLK), lambda i, j: (i, 0))
    return pl.pallas_call(
        _inproj_kernel,
        grid=(T // tm, nj),
        in_specs=[
            pl.BlockSpec((tm, D_MODEL), lambda i, j: (i, 0)),
            pl.BlockSpec((1, D_MODEL), lambda i, j: (0, 0)),
            pl.BlockSpec((D_MODEL, COL_BLK), lambda i, j: (0, j)),
            pl.BlockSpec((D_MODEL, LANES), lambda i, j: (0, 0)),
            pl.BlockSpec((3, tm, LANES), lambda i, j: (0, i % n_pos_blk, 0)),
        ],
        out_specs=(
            row_blk, row_blk, row_blk,
            pl.BlockSpec((tm, COL_BLK), lambda i, j: (i, jnp.maximum(j - 3, 0))),
            pl.BlockSpec((tm, LANES), lambda i, j: (i, 0)),
        ),
        out_shape=out_shape,
        scratch_shapes=[pltpu.VMEM((tm, D_MODEL), BF16)],
        compiler_params=_cparams(("arbitrary", "arbitrary")),
        name="inproj",
    )(x2d, norm_w, w_main, w_small, rope_tab)


def _attn_finish(o, lam_init, w, z):
    o = _rmsnorm_rows(o, w) * (1.0 - lam_init)
    return o * _silu(z)


def _attn_kernel(lam_ref, q_ref, k_ref, v_ref, z_ref, w_ref, o_ref, *, tq, lam_init):
    qi = pl.program_id(2)
    lam = lam_ref[0, 0]
    q = q_ref[...]
    lane = lax.broadcasted_iota(jnp.int32, q.shape, 1)
    qs = (jnp.where(lane < DK_A, q, 0.0).astype(BF16),
          jnp.where(lane >= DK_A, q, 0.0).astype(BF16))
    causal = _lower_tri(tq)

    def step(j, carry, masked):
        start = pl.multiple_of(j * tq, tq)
        kb = k_ref[pl.ds(start, tq), :].astype(BF16)
        vb = v_ref[pl.ds(start, tq), :].astype(BF16)
        new = []
        for m in range(2):
            mx, l, acc = carry[m]
            s = _dot_nt(qs[m], kb)
            if masked:
                s = jnp.where(causal, s, NEG)
            mn = jnp.maximum(mx, jnp.max(s, axis=-1, keepdims=True))
            alpha = jnp.exp(mx - mn)
            p = jnp.exp(s - mn)
            l = alpha * l + jnp.sum(p, axis=-1, keepdims=True)
            acc = alpha * acc + _dot(p.astype(BF16), vb)
            new.append((mn, l, acc))
        return tuple(new)

    init = tuple((jnp.full((tq, 1), NEG, F32), jnp.zeros((tq, 1), F32), jnp.zeros((tq, DV_A), F32))
                 for _ in range(2))
    carry = lax.fori_loop(0, qi, lambda j, c: step(j, c, False), init)
    (_, l1, a1), (_, l2, a2) = step(qi, carry, True)
    o = a1 / l1 - lam * (a2 / l2)
    o_ref[...] = _attn_finish(o, lam_init, w_ref[...], z_ref[...]).astype(o_ref.dtype)


def _attn_prompt(lam, q, k, v, rest, subln_w, B, L, lam_init):
    tq = min(256, L)
    nq = L // tq
    kern = functools.partial(_attn_kernel, tq=tq, lam_init=lam_init)
    return pl.pallas_call(
        kern,
        grid=(B, H_A, nq),
        in_specs=[
            pl.BlockSpec(memory_space=pltpu.SMEM),
            pl.BlockSpec((tq, LANES), lambda b, h, i: (b * nq + i, h)),
            pl.BlockSpec((L, LANES), lambda b, h, i: (b, h)),
            pl.BlockSpec((L, LANES), lambda b, h, i: (b, h)),
            pl.BlockSpec((tq, LANES), lambda b, h, i: (b * nq + i, 8 + h)),
            pl.BlockSpec((1, LANES), lambda b, h, i: (0, 0)),
        ],
        out_specs=pl.BlockSpec((tq, LANES), lambda b, h, i: (b * nq + i, h)),
        out_shape=jax.ShapeDtypeStruct((B * L, W_GROUP), BF16),
        compiler_params=_cparams(("arbitrary", "arbitrary", "arbitrary")),
        name="attn_prompt",
    )(lam, q, k, v, rest, subln_w)


def _decode_kernel(pt_ref, lam_ref, q_ref, kn_ref, vn_ref, z_ref, w_ref, *refs, n_pp, ls, lam_init):
    kp = refs[:n_pp]
    vp = refs[n_pp:2 * n_pp]
    o_ref = refs[2 * n_pp]
    qrow_scr, m_scr, l_scr, acc_scr = refs[2 * n_pp + 1:]
    p = pl.program_id(1)
    rows = 2 * H_A * ls

    @pl.when(p == 0)
    def _():
        q = q_ref[...]
        qt = jnp.concatenate([q] * (2 * H_A), axis=0)
        r = lax.broadcasted_iota(jnp.int32, qt.shape, 0)
        c = lax.broadcasted_iota(jnp.int32, qt.shape, 1)
        qrow_scr[...] = jnp.where((r // ls) == (c // DK_A), qt, 0.0).astype(BF16)
        m_scr[...] = jnp.full(m_scr.shape, NEG, F32)
        l_scr[...] = jnp.zeros(l_scr.shape, F32)
        acc_scr[...] = jnp.zeros(acc_scr.shape, F32)

    qrow = qrow_scr[...]

    def accumulate(scores, values):
        mx = m_scr[...]
        mloc = jnp.max(scores[0], axis=-1, keepdims=True)
        for s in scores[1:]:
            mloc = jnp.maximum(mloc, jnp.max(s, axis=-1, keepdims=True))
        mn = jnp.maximum(mx, mloc)
        alpha = jnp.exp(mx - mn)
        l = alpha * l_scr[...]
        acc = alpha * acc_scr[...]
        for s, vv in zip(scores, values):
            pr = jnp.exp(s - mn)
            l = l + jnp.sum(pr, axis=-1, keepdims=True)
            acc = acc + _dot(pr.astype(BF16), vv)
        m_scr[...] = mn
        l_scr[...] = l
        acc_scr[...] = acc

    accumulate([_dot_nt(qrow, kp[i][...].astype(BF16)) for i in range(n_pp)],
               [vp[i][...].astype(BF16) for i in range(n_pp)])

    @pl.when(p == pl.num_programs(1) - 1)
    def _():
        s = _dot_nt(qrow, kn_ref[...].astype(BF16))
        jj = lax.broadcasted_iota(jnp.int32, s.shape, 1)
        qq = lax.broadcasted_iota(jnp.int32, s.shape, 0) % ls
        accumulate([jnp.where(jj <= qq, s, NEG)], [vn_ref[...].astype(BF16)])
        lam = lam_ref[0, 0]
        l = l_scr[...]
        acc = acc_scr[...]
        w = w_ref[...]
        for h in range(H_A):
            r0 = h * 2 * ls
            cs = slice(h * DV_A, (h + 1) * DV_A)
            o1 = acc[r0:r0 + ls, cs] / l[r0:r0 + ls]
            o2 = acc[r0 + ls:r0 + 2 * ls, cs] / l[r0 + ls:r0 + 2 * ls]
            o = o1 - lam * o2
            o_ref[:, cs] = _attn_finish(o, lam_init, w, z_ref[:, cs]).astype(o_ref.dtype)


def _attn_decode(lam, q, k, v, rest, subln_w, cache_k, cache_v, page_table, layer, Bs, Ls, lam_init):
    n_pages = page_table.shape[1]
    page = cache_k.shape[2]
    n_pp = math.gcd(n_pages, 4)
    rows = 2 * H_A * Ls
    ck = cache_k.reshape(cache_k.shape[0], cache_k.shape[1], page, H_A * 2 * DK_A)
    cv = cache_v.reshape(cache_v.shape[0], cache_v.shape[1], page, H_A * DV_A)

    def page_spec(i):
        return pl.BlockSpec((None, None, page, W_GROUP),
                            lambda b, p, pt: (layer, pt[b, p * n_pp + i], 0, 0))

    tok = pl.BlockSpec((Ls, W_GROUP), lambda b, p, pt: (b, 0))
    kern = functools.partial(_decode_kernel, n_pp=n_pp, ls=Ls, lam_init=lam_init)
    grid_spec = pltpu.PrefetchScalarGridSpec(
        num_scalar_prefetch=1,
        grid=(Bs, n_pages // n_pp),
        in_specs=[
            pl.BlockSpec(memory_space=pltpu.SMEM),
            tok, tok, tok,
            pl.BlockSpec((Ls, W_GROUP), lambda b, p, pt: (b, 2)),
            pl.BlockSpec((1, LANES), lambda b, p, pt: (0, 0)),
        ] + [page_spec(i) for i in range(n_pp)] * 2,
        out_specs=tok,
        scratch_shapes=[
            pltpu.VMEM((rows, W_GROUP), BF16),
            pltpu.VMEM((rows, 1), F32),
            pltpu.VMEM((rows, 1), F32),
            pltpu.VMEM((rows, W_GROUP), F32),
        ],
    )
    return pl.pallas_call(
        kern,
        grid_spec=grid_spec,
        out_shape=jax.ShapeDtypeStruct((Bs * Ls, W_GROUP), F32),
        compiler_params=_cparams(("arbitrary", "arbitrary")),
        name="attn_decode",
    )(page_table, lam, q, k, v, rest, subln_w, *([ck] * n_pp), *([cv] * n_pp))


def _conv_step(x_ref, buf_ref, cw_ref, cb_ref, cout_ref, xp_scr, q):
    c = pl.program_id(1)
    pad = SUBLANES
    tail0 = pad - (CONV_W - 1)

    @pl.when(c == 0)
    def _():
        xp_scr[tail0:pad, :] = buf_ref[...]

    xp_scr[pad:pad + q, :] = x_ref[...]
    y = xp_scr[tail0:tail0 + q, :] * cw_ref[0:1, :]
    for t in range(1, CONV_W):
        y = y + xp_scr[tail0 + t:tail0 + t + q, :] * cw_ref[t:t + 1, :]
    y = y + cb_ref[...]
    tail = xp_scr[q + tail0:q + pad, :]
    xp_scr[tail0:pad, :] = tail

    @pl.when(c == pl.num_programs(1) - 1)
    def _():
        cout_ref[...] = tail

    return y


def _ssd_kernel(xbc_ref, dt_ref, zb_ref, buf_ref, s0_ref, cw_ref, cb_ref, dtb_ref, alog_ref,
                dvec_ref, nw_ref, y_ref, sout_ref, cout_ref, xp_scr, s_scr, y_scr, *, q):
    c = pl.program_id(1)

    @pl.when(c == 0)
    def _():
        s_scr[...] = s0_ref[...]

    xbc = _silu(_conv_step(xbc_ref, buf_ref, cw_ref, cb_ref, cout_ref, xp_scr, q))
    xs = xbc[:, :W_GROUP]
    dt = _softplus(dt_ref[...] + dtb_ref[...])
    a = -jnp.exp(alog_ref[...])
    tri = _lower_tri(q)
    cum = _dot_exact(tri.astype(F32), dt * a)
    cum_t = cum.T
    dt_t = dt.T
    last = cum[q - 1:q, :]
    ecum = jnp.exp(cum)
    wj = jnp.exp(last - cum) * dt
    elast = jnp.exp(last)
    heads_per_group = H_B // G_B
    for g in range(G_B):
        bg = xbc[:, W_GROUP + g * N_B:W_GROUP + (g + 1) * N_B]
        cg = xbc[:, W_GROUP + (G_B + g) * N_B:W_GROUP + (G_B + g + 1) * N_B]
        bgb = bg.astype(BF16)
        cb_g = _dot_nt(cg.astype(BF16), bgb)
        for hh in range(heads_per_group):
            h = g * heads_per_group + hh
            hs = slice(h * P_B, (h + 1) * P_B)
            seg = cum[:, h:h + 1] - cum_t[h:h + 1, :]
            dec = jnp.exp(jnp.where(tri, seg, NEG))
            mh = (cb_g * dec * dt_t[h:h + 1, :]).astype(BF16)
            xh = xs[:, hs]
            s_old = s_scr[h]
            yh = _dot(mh, xh.astype(BF16)) + _dot_nt((cg * ecum[:, h:h + 1]).astype(BF16),
                                                     s_old.astype(BF16))
            y_scr[:, hs] = yh
            xw = (xh * wj[:, h:h + 1]).astype(BF16)
            s_scr[h] = elast[:, h:h + 1] * s_old + _dot_tn(xw, bgb)
    y = y_scr[...] + dvec_ref[...] * xs
    y_ref[...] = _rmsnorm_rows(y * _silu(zb_ref[...]), nw_ref[...]).astype(y_ref.dtype)

    @pl.when(c == pl.num_programs(1) - 1)
    def _():
        sout_ref[...] = s_scr[...]


def _ssd(rest, small, buf, s0, cw, cb, dtb, alog, dvec, nw, B, L, out_dtype):
    q = math.gcd(L, 64)
    nc = L // q
    kern = functools.partial(_ssd_kernel, q=q)
    full = lambda shape: pl.BlockSpec(shape, lambda b, c: (0,) * len(shape))
    return pl.pallas_call(
        kern,
        grid=(B, nc),
        in_specs=[
            pl.BlockSpec((q, CONV_B_CH), lambda b, c: (b * nc + c, 0)),
            pl.BlockSpec((q, LANES), lambda b, c: (b * nc + c, 0)),
            pl.BlockSpec((q, W_GROUP), lambda b, c: (b * nc + c, 3)),
            pl.BlockSpec((None, CONV_W - 1, CONV_B_CH), lambda b, c: (b, 0, 0)),
            pl.BlockSpec((None, H_B, P_B, N_B), lambda b, c: (b, 0, 0, 0)),
            full((CONV_W, CONV_B_CH)), full((1, CONV_B_CH)), full((1, LANES)), full((1, LANES)),
            full((1, W_GROUP)), full((1, W_GROUP)),
        ],
        out_specs=(
            pl.BlockSpec((q, W_GROUP), lambda b, c: (b * nc + c, 0)),
            pl.BlockSpec((None, H_B, P_B, N_B), lambda b, c: (b, 0, 0, 0)),
            pl.BlockSpec((None, CONV_W - 1, CONV_B_CH), lambda b, c: (b, 0, 0)),
        ),
        out_shape=(
            jax.ShapeDtypeStruct((B * L, W_GROUP), out_dtype),
            jax.ShapeDtypeStruct((B, H_B, P_B, N_B), F32),
            jax.ShapeDtypeStruct((B, CONV_W - 1, CONV_B_CH), F32),
        ),
        scratch_shapes=[
            pltpu.VMEM((q + SUBLANES, CONV_B_CH), F32),
            pltpu.VMEM((H_B, P_B, N_B), F32),
            pltpu.VMEM((q, W_GROUP), F32),
        ],
        compiler_params=_cparams(("arbitrary", "arbitrary")),
        name="ssd",
    )(rest, small, rest, buf, s0, cw, cb, dtb, alog, dvec, nw)


def _gla_kernel(qk_ref, v_ref, gc_ref, sm_ref, s0_ref, w2_ref, gb_ref, nw_ref,
                y_ref, sout_ref, st_scr, *, q):
    c = pl.program_id(1)

    @pl.when(c == 0)
    def _():
        st_scr[...] = s0_ref[...]

    gkl = _dot(sm_ref[...].astype(BF16), w2_ref[...]) + gb_ref[...]
    g = (jnp.minimum(gkl, 0.0) - jnp.log1p(jnp.exp(-jnp.abs(gkl)))) * (1.0 / GK_NORM)
    tri = _lower_tri(q)
    bc = _dot_exact(tri.astype(F32), g)
    last = bc[q - 1:q, :]
    width = H_C * DK_C
    qc = qk_ref[:, :width] * (DK_C ** -0.5)
    kc = qk_ref[:, width:]
    qe = (qc * jnp.exp(bc)).astype(BF16)
    ke = (kc * jnp.exp(-bc)).astype(BF16)
    kl = (kc * jnp.exp(last - bc)).astype(BF16)
    el = jnp.exp(last)
    nw = nw_ref[...]
    for h in range(H_C):
        ks = slice(h * DK_C, (h + 1) * DK_C)
        vs = slice(h * DV_C, (h + 1) * DV_C)
        att = jnp.where(tri, _dot_nt(qe[:, ks], ke[:, ks]), 0.0)
        vh = v_ref[:, vs].astype(BF16)
        st_old = st_scr[h]
        o = _dot(att.astype(BF16), vh) + _dot_nt(qe[:, ks], st_old.astype(BF16))
        st_scr[h] = el[:, ks] * st_old + _dot_tn(vh, kl[:, ks])
        y_ref[:, vs] = (_rmsnorm_rows(o, nw) * _silu(gc_ref[:, vs])).astype(y_ref.dtype)

    @pl.when(c == pl.num_programs(1) - 1)
    def _():
        sout_ref[...] = st_scr[...]


def _gla(rest, small, s0_t, w2p, gb, nw, B, L, out_dtype):
    q = min(L, 64)
    nc = L // q
    kern = functools.partial(_gla_kernel, q=q)
    full = lambda shape: pl.BlockSpec(shape, lambda b, c: (0,) * len(shape))
    state = pl.BlockSpec((None, H_C, DV_C, DK_C), lambda b, c: (b, 0, 0, 0))
    return pl.pallas_call(
        kern,
        grid=(B, nc),
        in_specs=[
            pl.BlockSpec((q, W_GROUP), lambda b, c: (b * nc + c, 4)),
            pl.BlockSpec((q, W_GROUP), lambda b, c: (b * nc + c, 5)),
            pl.BlockSpec((q, W_GROUP), lambda b, c: (b * nc + c, 6)),
            pl.BlockSpec((q, LANES), lambda b, c: (b * nc + c, 0)),
            state,
            full((LANES, H_C * DK_C)), full((1, H_C * DK_C)), full((1, DV_C)),
        ],
        out_specs=(pl.BlockSpec((q, W_GROUP), lambda b, c: (b * nc + c, 0)), state),
        out_shape=(
            jax.ShapeDtypeStruct((B * L, W_GROUP), out_dtype),
            jax.ShapeDtypeStruct((B, H_C, DV_C, DK_C), F32),
        ),
        scratch_shapes=[pltpu.VMEM((H_C, DV_C, DK_C), F32)],
        compiler_params=_cparams(("arbitrary", "arbitrary")),
        name="gla",
    )(rest, rest, rest, small, s0_t, w2p, gb, nw)


def _lru_kernel(xd_ref, zd_ref, buf_ref, h0_ref, cw_ref, cb_ref, wa_ref, ba_ref, wx_ref, bx_ref,
                lam_ref, y_ref, hout_ref, cout_ref, xp_scr, h_scr, a_scr, u_scr, *, q):
    c = pl.program_id(1)

    @pl.when(c == 0)
    def _():
        h_scr[...] = h0_ref[...]

    xr = _conv_step(xd_ref, buf_ref, cw_ref, cb_ref, cout_ref, xp_scr, q)
    xb = xr.astype(BF16)
    r = _sigmoid(_dot(xb, wa_ref[...]) + ba_ref[...])
    i = _sigmoid(_dot(xb, wx_ref[...]) + bx_ref[...])
    log_a = -LRU_C * r * _softplus(-lam_ref[...])
    a = jnp.exp(log_a)
    x2 = 2.0 * log_a
    e2 = a * a
    near0 = jnp.where(e2 == 1.0, x2, (e2 - 1.0) * x2 / jnp.log(e2))
    em1 = jnp.where(x2 < -1.0, e2 - 1.0, near0)
    u = jnp.sqrt(-em1) * (i * xr)
    r8 = lax.broadcasted_iota(jnp.int32, a.shape, 0) % SUBLANES
    for s in (1, 2, 4):
        keep = r8 >= s
        a_sh = pltpu.roll(a, s, 0)
        u_sh = pltpu.roll(u, s, 0)
        u = jnp.where(keep, a * u_sh + u, u)
        a = jnp.where(keep, a * a_sh, a)
    a_scr[...] = a
    u_scr[...] = u
    carry = h_scr[...]
    for gi in range(q // SUBLANES):
        rows = slice(gi * SUBLANES, (gi + 1) * SUBLANES)
        hg = a_scr[rows, :] * carry + u_scr[rows, :]
        y_ref[rows, :] = (hg * _silu(zd_ref[rows, :])).astype(y_ref.dtype)
        carry = hg[SUBLANES - 1:SUBLANES, :]
    h_scr[...] = carry

    @pl.when(c == pl.num_programs(1) - 1)
    def _():
        hout_ref[...] = carry


def _lru(rest, buf, h0, cw, cb, wa, ba, wx, bx, lam, B, L, out_dtype):
    q = min(L, 256)
    nc = L // q
    kern = functools.partial(_lru_kernel, q=q)
    full = lambda shape: pl.BlockSpec(shape, lambda b, c: (0,) * len(shape))
    return pl.pallas_call(
        kern,
        grid=(B, nc),
        in_specs=[
            pl.BlockSpec((q, W_D), lambda b, c: (b * nc + c, 7)),
            pl.BlockSpec((q, W_D), lambda b, c: (b * nc + c, 8)),
            pl.BlockSpec((None, CONV_W - 1, W_D), lambda b, c: (b, 0, 0)),
            pl.BlockSpec((None, 1, W_D), lambda b, c: (b, 0, 0)),
            full((CONV_W, W_D)), full((1, W_D)),
            full((W_D, W_D)), full((1, W_D)), full((W_D, W_D)), full((1, W_D)), full((1, W_D)),
        ],
        out_specs=(
            pl.BlockSpec((q, W_D), lambda b, c: (b * nc + c, 0)),
            pl.BlockSpec((None, 1, W_D), lambda b, c: (b, 0, 0)),
            pl.BlockSpec((None, CONV_W - 1, W_D), lambda b, c: (b, 0, 0)),
        ),
        out_shape=(
            jax.ShapeDtypeStruct((B * L, W_D), out_dtype),
            jax.ShapeDtypeStruct((B, 1, W_D), F32),
            jax.ShapeDtypeStruct((B, CONV_W - 1, W_D), F32),
        ),
        scratch_shapes=[
            pltpu.VMEM((q + SUBLANES, W_D), F32),
            pltpu.VMEM((1, W_D), F32),
            pltpu.VMEM((q, W_D), F32),
            pltpu.VMEM((q, W_D), F32),
        ],
        compiler_params=_cparams(("arbitrary", "arbitrary")),
        name="lru",
    )(rest, rest, buf, h0, cw, cb, wa, ba, wx, bx, lam)


def _outproj_kernel(ya_ref, yb_ref, yc_ref, yd_ref, w_ref, x_ref, fw_ref, o_ref, *, final):
    out = _dot(ya_ref[...].astype(BF16), w_ref[0])
    for g, y_ref in enumerate((yb_ref, yc_ref, yd_ref), start=1):
        out = out + _dot(y_ref[...].astype(BF16), w_ref[g])
    res = x_ref[...] + out
    if final:
        res = _rmsnorm_rows(res, fw_ref[...])
    o_ref[...] = res


def _outproj(ys, w_out4, x2d, final_w, final):
    T = x2d.shape[0]
    tm = min(256, T)
    kern = functools.partial(_outproj_kernel, final=final)
    yspec = pl.BlockSpec((tm, W_GROUP), lambda i: (i, 0))
    return pl.pallas_call(
        kern,
        grid=(T // tm,),
        in_specs=[yspec] * 4 + [
            pl.BlockSpec((4, W_GROUP, D_MODEL), lambda i: (0, 0, 0)),
            pl.BlockSpec((tm, D_MODEL), lambda i: (i, 0)),
            pl.BlockSpec((1, D_MODEL), lambda i: (0, 0)),
        ],
        out_specs=pl.BlockSpec((tm, D_MODEL), lambda i: (i, 0)),
        out_shape=jax.ShapeDtypeStruct((T, D_MODEL), F32),
        compiler_params=_cparams(("arbitrary",)),
        name="outproj",
    )(*ys, w_out4, x2d, final_w)


def _rope_table(pos, rows):
    half = ROPE_DIM // 2
    inv = ROPE_THETA ** (-jnp.arange(half, dtype=F32) / half)
    ang = pos.astype(F32)[:, None] * inv[None, :]
    cos, sin = jnp.cos(ang), jnp.sin(ang)
    n = pos.shape[0]
    pad = jnp.zeros((n, DK_A - ROPE_DIM), F32)
    c64 = jnp.concatenate([cos, cos, pad + 1.0], axis=1)
    s1 = jnp.concatenate([-sin, jnp.zeros_like(sin), pad], axis=1)
    s2 = jnp.concatenate([jnp.zeros_like(sin), sin, pad], axis=1)
    tab = jnp.stack([jnp.tile(t, (1, LANES // DK_A)) for t in (c64, s1, s2)])
    return jnp.tile(tab, (1, rows // n, 1))


def _block_diag(w):
    nb, bw, _ = w.shape
    eye = jnp.eye(nb, dtype=w.dtype)
    return (eye[:, None, :, None] * w[:, :, None, :]).reshape(nb * bw, nb * bw)


def _prep_layer(l, norm_w, w_in, w_out, lam_q1, lam_k1, lam_q2, lam_k2, subln_w, ssd_conv_w,
                ssd_conv_b, ssd_dt_bias, ssd_a_log, ssd_d, ssd_norm_w, gla_gk_w2, gla_gk_b,
                gla_norm_w, lru_conv_w, lru_conv_b, lru_wa, lru_ba, lru_wx, lru_bx, lru_lambda):
    w = w_in[l]
    cols = lambda a, n: w[:, a:a + n]
    w_main = jnp.concatenate([
        cols(_QA, 512), cols(_KA, 512), cols(_VA, 512), cols(_XBC, CONV_B_CH), cols(_ZA, 512),
        cols(_ZB, 512), cols(_QC, 256), cols(_KC, 256), cols(_VC, 512), cols(_GC, 512),
        cols(_XD, 512), cols(_ZD, 512)], axis=1).astype(BF16)
    w_small = jnp.concatenate([cols(_DT, H_B), cols(_GKLR, GK_RANK),
                               jnp.zeros((D_MODEL, LANES - H_B - GK_RANK), F32)], axis=1).astype(BF16)
    lam_init = 0.8 - 0.6 * math.exp(-0.3 * l)
    lam = (jnp.exp(jnp.sum(lam_q1[l] * lam_k1[l])) - jnp.exp(jnp.sum(lam_q2[l] * lam_k2[l]))
           + lam_init).reshape(1, 1).astype(F32)
    pad_lanes = lambda v: jnp.concatenate([v, jnp.zeros((LANES - v.shape[0],), F32)]).reshape(1, LANES)
    w2p = jnp.zeros((LANES, H_C * DK_C), F32).at[H_B:H_B + GK_RANK].set(gla_gk_w2[l]).astype(BF16)
    return dict(
        norm_w=norm_w[l].reshape(1, D_MODEL), w_main=w_main, w_small=w_small,
        w_out4=w_out[l].reshape(4, W_GROUP, D_MODEL).astype(BF16),
        lam=lam, lam_init=lam_init, subln_w=subln_w[l].reshape(1, DV_A),
        ssd_cw=ssd_conv_w[l], ssd_cb=ssd_conv_b[l].reshape(1, CONV_B_CH),
        ssd_dtb=pad_lanes(ssd_dt_bias[l]), ssd_alog=pad_lanes(ssd_a_log[l]),
        ssd_dvec=jnp.repeat(ssd_d[l], P_B).reshape(1, W_GROUP), ssd_nw=ssd_norm_w[l].reshape(1, W_GROUP),
        gla_w2p=w2p, gla_gb=gla_gk_b[l].reshape(1, H_C * DK_C), gla_nw=gla_norm_w[l].reshape(1, DV_C),
        lru_cw=lru_conv_w[l], lru_cb=lru_conv_b[l].reshape(1, W_D),
        lru_wa=_block_diag(lru_wa[l]).astype(BF16), lru_ba=lru_ba[l].reshape(1, W_D),
        lru_wx=_block_diag(lru_wx[l]).astype(BF16), lru_bx=lru_bx[l].reshape(1, W_D),
        lru_lam=lru_lambda[l].reshape(1, W_D),
    )


def _layer(x2d, B, L, rope_tab, p, states, past, layer, final_w, final):
    ssd_s0, ssd_buf, gla_s0, lru_h0, lru_buf = states
    ydt = BF16 if L % 16 == 0 else F32
    q, k, v, rest, small = _inproj(x2d, p["norm_w"], p["w_main"], p["w_small"], rope_tab, L)
    if past is None:
        ya = _attn_prompt(p["lam"], q, k, v, rest, p["subln_w"], B, L, p["lam_init"])
    else:
        cache_k, cache_v, page_table = past
        ya = _attn_decode(p["lam"], q, k, v, rest, p["subln_w"], cache_k, cache_v, page_table,
                          layer, B, L, p["lam_init"])
    yb, ssd_s, ssd_c = _ssd(rest, small, ssd_buf, ssd_s0, p["ssd_cw"], p["ssd_cb"], p["ssd_dtb"],
                            p["ssd_alog"], p["ssd_dvec"], p["ssd_nw"], B, L, ydt)
    yc, gla_st = _gla(rest, small, jnp.swapaxes(gla_s0, -1, -2), p["gla_w2p"], p["gla_gb"],
                      p["gla_nw"], B, L, ydt)
    yd, lru_h, lru_c = _lru(rest, lru_buf, lru_h0.reshape(B, 1, W_D), p["lru_cw"], p["lru_cb"],
                            p["lru_wa"], p["lru_ba"], p["lru_wx"], p["lru_bx"], p["lru_lam"], B, L, ydt)
    x_new = _outproj((ya, yb, yc, yd), p["w_out4"], x2d, final_w, final)
    new_state = (k.reshape(B, L, H_A, 2 * DK_A), v.reshape(B, L, H_A, DV_A), ssd_s, ssd_c,
                 jnp.swapaxes(gla_st, -1, -2), lru_h.reshape(B, W_D), lru_c)
    return x_new, new_state


def kernel(x_prompt, x_sample, cache_k, cache_v, page_table, state_ssd, state_ssd_conv, state_gla, state_lru, state_lru_conv, norm_w, w_in, w_out, lam_q1, lam_k1, lam_q2, lam_k2, subln_w, ssd_conv_w, ssd_conv_b, ssd_dt_bias, ssd_a_log, ssd_d, ssd_norm_w, gla_gk_w2, gla_gk_b, gla_norm_w, lru_conv_w, lru_conv_b, lru_wa, lru_ba, lru_wx, lru_bx, lru_lambda, final_norm_w):
    Bp, Lp, _ = x_prompt.shape
    Bs, Ls, _ = x_sample.shape
    depth = w_in.shape[0]
    past_len = page_table.shape[1] * cache_k.shape[2]
    tab_p = _rope_table(jnp.arange(Lp, dtype=jnp.int32), max(Lp, min(512, Bp * Lp)))
    tab_s = _rope_table(past_len + jnp.arange(Ls, dtype=jnp.int32), max(Ls, min(512, Bs * Ls)))
    final_w = final_norm_w.reshape(1, D_MODEL)
    hp = x_prompt.reshape(Bp * Lp, D_MODEL)
    hs = x_sample.reshape(Bs * Ls, D_MODEL)
    sp, ss = [], []
    for l in range(depth):
        p = _prep_layer(l, norm_w, w_in, w_out, lam_q1, lam_k1, lam_q2, lam_k2, subln_w, ssd_conv_w,
                        ssd_conv_b, ssd_dt_bias, ssd_a_log, ssd_d, ssd_norm_w, gla_gk_w2, gla_gk_b,
                        gla_norm_w, lru_conv_w, lru_conv_b, lru_wa, lru_ba, lru_wx, lru_bx, lru_lambda)
        final = l == depth - 1
        st_p = (jnp.zeros((Bp, H_B, P_B, N_B), F32), jnp.zeros((Bp, CONV_W - 1, CONV_B_CH), F32),
                jnp.zeros((Bp, H_C, DK_C, DV_C), F32), jnp.zeros((Bp, W_D), F32),
                jnp.zeros((Bp, CONV_W - 1, W_D), F32))
        hp, new_p = _layer(hp, Bp, Lp, tab_p, p, st_p, None, l, final_w, final)
        sp.append(new_p)
        st_s = (state_ssd[l], state_ssd_conv[l], state_gla[l], state_lru[l], state_lru_conv[l])
        hs, new_s = _layer(hs, Bs, Ls, tab_s, p, st_s, (cache_k, cache_v, page_table), l, final_w, final)
        ss.append(new_s)
    outs_p = [jnp.stack([s[i] for s in sp]) for i in range(7)]
    outs_s = [jnp.stack([s[i] for s in ss]) for i in range(7)]
    return (hp.reshape(Bp, Lp, D_MODEL), hs.reshape(Bs, Ls, D_MODEL), *outs_p, *outs_s)
```

```python
import functools
import math

import jax
import jax.numpy as jnp
from jax import lax
from jax.experimental import pallas as pl
from jax.experimental.pallas import tpu as pltpu

F32 = jnp.float32
BF16 = jnp.bfloat16
EPS = 1e-6
NEG = -1e30

D_MODEL = 2048
W_GROUP = 512
H_A, DV_A, DK_A = 4, 128, 64
ROPE_DIM, ROPE_THETA = 16, 500000.0
H_B, P_B, N_B, G_B = 8, 64, 128, 2
CONV_W = 4
CONV_B_CH = W_GROUP + 2 * G_B * N_B
H_C, DK_C, DV_C = 4, 64, 128
GK_RANK, GK_NORM = 16, 16.0
W_D, NB_D, BW_D = 512, 8, 64
LRU_C = 8.0

LANES = 128
SUBLANES = 8
COL_BLK = 512
INPROJ_ROWS = 1024
ATTN_BQ = 512
ATTN_BK = 512
DECODE_PAGES = 8
VMEM_LIMIT = 56 * 1024 * 1024

_SPLITS = (512, 512, 512, 512, CONV_B_CH, H_B, W_GROUP, 256, 256, 512, GK_RANK, W_GROUP, W_D, W_D)
_OFF = [0]
for _s in _SPLITS:
    _OFF.append(_OFF[-1] + _s)
(_QA, _KA, _VA, _ZA, _XBC, _DT, _ZB, _QC, _KC, _VC, _GKLR, _GC, _XD, _ZD) = _OFF[:-1]

REST_COLS = 4608


def _cparams(sem):
    return pltpu.CompilerParams(dimension_semantics=sem, vmem_limit_bytes=VMEM_LIMIT)


def _sigmoid(x):
    return jax.nn.sigmoid(x)


def _silu(x):
    return x * _sigmoid(x)


def _softplus(x):
    return jnp.maximum(x, 0.0) + jnp.log1p(jnp.exp(-jnp.abs(x)))


def _rmsnorm_rows(x, w):
    return x * lax.rsqrt(jnp.mean(x * x, axis=-1, keepdims=True) + EPS) * w


def _dot(a, b):
    return jnp.dot(a, b, preferred_element_type=F32)


def _dot_nt(a, b):
    return lax.dot_general(a, b, (((1,), (1,)), ((), ())), preferred_element_type=F32)


def _dot_tn(a, b):
    return lax.dot_general(a, b, (((0,), (0,)), ((), ())), preferred_element_type=F32)


def _dot_exact(a, b):
    return jnp.dot(a, b, preferred_element_type=F32, precision=lax.Precision.HIGHEST)


def _lower_tri(n):
    r = lax.broadcasted_iota(jnp.int32, (n, n), 0)
    c = lax.broadcasted_iota(jnp.int32, (n, n), 1)
    return r >= c


def _batch_group(b, want):
    return math.gcd(b, want)


def _inproj_kernel(x_ref, nw_ref, w_ref, ws_ref, rope_ref,
                   q_ref, k_ref, v_ref, rest_ref, small_ref, u_scr):
    j = pl.program_id(1)

    @pl.when(j == 0)
    def _():
        u = _rmsnorm_rows(x_ref[...], nw_ref[...]).astype(BF16)
        u_scr[...] = u
        small_ref[...] = _dot(u, ws_ref[...])

    acc = _dot(u_scr[...], w_ref[...])

    def rope_store(dst, scale):
        c, s1, s2 = rope_ref[0], rope_ref[1], rope_ref[2]
        for hh in range(COL_BLK // LANES):
            sl = slice(hh * LANES, (hh + 1) * LANES)
            a = acc[:, sl]
            r = (a * c + pltpu.roll(a, LANES - ROPE_DIM // 2, 1) * s1
                 + pltpu.roll(a, ROPE_DIM // 2, 1) * s2)
            dst[:, sl] = r * scale

    @pl.when(j == 0)
    def _():
        rope_store(q_ref, DK_A ** -0.5)

    @pl.when(j == 1)
    def _():
        rope_store(k_ref, 1.0)

    @pl.when(j == 2)
    def _():
        v_ref[...] = acc

    @pl.when(j >= 3)
    def _():
        rest_ref[...] = acc


def _inproj(x2d, norm_w, w_main, w_small, rope_tab, seq_len):
    T = x2d.shape[0]
    tm = min(INPROJ_ROWS, T)
    assert T % tm == 0 and (seq_len % tm == 0 or tm % seq_len == 0)
    n_pos_blk = max(seq_len // tm, 1)
    nj = w_main.shape[0]
    out_shape = (
        jax.ShapeDtypeStruct((T, COL_BLK), F32),
        jax.ShapeDtypeStruct((T, COL_BLK), F32),
        jax.ShapeDtypeStruct((T, COL_BLK), F32),
        jax.ShapeDtypeStruct((T, REST_COLS), F32),
        jax.ShapeDtypeStruct((T, LANES), F32),
    )
    row_blk = pl.BlockSpec((tm, COL_BLK), lambda i, j: (i, 0))
    return pl.pallas_call(
        _inproj_kernel,
        grid=(T // tm, nj),
        in_specs=[
            pl.BlockSpec((tm, D_MODEL), lambda i, j: (i, 0)),
            pl.BlockSpec((1, D_MODEL), lambda i, j: (0, 0)),
            pl.BlockSpec((None, D_MODEL, COL_BLK), lambda i, j: (j, 0, 0)),
            pl.BlockSpec((D_MODEL, LANES), lambda i, j: (0, 0)),
            pl.BlockSpec((3, tm, LANES), lambda i, j: (0, i % n_pos_blk, 0)),
        ],
        out_specs=(
            row_blk, row_blk, row_blk,
            pl.BlockSpec((tm, COL_BLK), lambda i, j: (i, jnp.maximum(j - 3, 0))),
            pl.BlockSpec((tm, LANES), lambda i, j: (i, 0)),
        ),
        out_shape=out_shape,
        scratch_shapes=[pltpu.VMEM((tm, D_MODEL), BF16)],
        compiler_params=_cparams(("arbitrary", "arbitrary")),
        name="inproj",
    )(x2d, norm_w, w_main, w_small, rope_tab)


def _attn_finish(o, lam_init, w, z):
    o = _rmsnorm_rows(o, w) * (1.0 - lam_init)
    return o * _silu(z)


def _attn_kernel(lam_ref, q_ref, k_ref, v_ref, z_ref, w_ref, o_ref, kb_scr, vt_scr,
                 m_scr, l_scr, acc_scr, *, bq, bk, lam_init):
    qi = pl.program_id(2)
    n_kb = kb_scr.shape[0]
    diag_blocks = bq // bk

    @pl.when(qi == 0)
    def _():
        for c in range(n_kb):
            rows = slice(c * bk, (c + 1) * bk)
            kb_scr[c] = k_ref[rows, :].astype(BF16)
            vt_scr[c] = v_ref[rows, :].T.astype(BF16)

    qt = q_ref[...].T
    feat = lax.broadcasted_iota(jnp.int32, qt.shape, 0)
    qts = (jnp.where(feat < DK_A, qt, 0.0).astype(BF16),
           jnp.where(feat >= DK_A, qt, 0.0).astype(BF16))
    m_scr[...] = jnp.full(m_scr.shape, NEG, F32)
    l_scr[...] = jnp.zeros(l_scr.shape, F32)
    acc_scr[...] = jnp.zeros(acc_scr.shape, F32)

    def step(j, mask):
        kb = kb_scr[j]
        vtb = vt_scr[j]
        for m in range(2):
            s = _dot(kb, qts[m])
            if mask is not None:
                s = jnp.where(mask, s, NEG)
            mx = m_scr[m]
            mn = jnp.maximum(mx, jnp.max(s, axis=0, keepdims=True))
            alpha = jnp.exp(mx - mn)
            p = jnp.exp(s - mn)
            l_scr[m] = alpha * l_scr[m] + jnp.sum(p, axis=0, keepdims=True)
            acc_scr[m] = alpha * acc_scr[m] + _dot(vtb, p.astype(BF16))
            m_scr[m] = mn

    def unmasked(j, carry):
        step(j, None)
        return carry

    lax.fori_loop(0, qi * diag_blocks, unmasked, 0)
    key_i = lax.broadcasted_iota(jnp.int32, (bk, bq), 0)
    qry_i = lax.broadcasted_iota(jnp.int32, (bk, bq), 1)
    for d in range(diag_blocks):
        step(qi * diag_blocks + d, key_i + d * bk <= qry_i)
    lam = lam_ref[0, 0]
    o = (acc_scr[0] / l_scr[0] - lam * (acc_scr[1] / l_scr[1])).T
    o_ref[...] = _attn_finish(o, lam_init, w_ref[...], z_ref[...]).astype(o_ref.dtype)


def _attn_prompt(lam, q, k, v, rest, subln_w, B, L, lam_init):
    bq = min(ATTN_BQ, L)
    bk = min(ATTN_BK, bq)
    nq = L // bq
    kern = functools.partial(_attn_kernel, bq=bq, bk=bk, lam_init=lam_init)
    return pl.pallas_call(
        kern,
        grid=(B, H_A, nq),
        in_specs=[
            pl.BlockSpec(memory_space=pltpu.SMEM),
            pl.BlockSpec((bq, LANES), lambda b, h, i: (b * nq + i, h)),
            pl.BlockSpec((L, LANES), lambda b, h, i: (b, h)),
            pl.BlockSpec((L, LANES), lambda b, h, i: (b, h)),
            pl.BlockSpec((bq, LANES), lambda b, h, i: (b * nq + i, 8 + h)),
            pl.BlockSpec((1, LANES), lambda b, h, i: (0, 0)),
        ],
        out_specs=pl.BlockSpec((bq, LANES), lambda b, h, i: (b * nq + i, h)),
        out_shape=jax.ShapeDtypeStruct((B * L, W_GROUP), BF16),
        scratch_shapes=[
            pltpu.VMEM((L // bk, bk, LANES), BF16),
            pltpu.VMEM((L // bk, DV_A, bk), BF16),
            pltpu.VMEM((2, 1, bq), F32),
            pltpu.VMEM((2, 1, bq), F32),
            pltpu.VMEM((2, DV_A, bq), F32),
        ],
        compiler_params=_cparams(("arbitrary", "arbitrary", "arbitrary")),
        name="attn_prompt",
    )(lam, q, k, v, rest, subln_w)


def _decode_kernel(pt_ref, lam_ref, q_ref, kn_ref, vn_ref, z_ref, w_ref, *refs,
                   n_pp, ls, page, lam_init):
    kp = refs[:n_pp]
    vp = refs[n_pp:2 * n_pp]
    o_ref = refs[2 * n_pp]
    qrow_scr, m_scr, l_scr, acc_scr = refs[2 * n_pp + 1:]
    p = pl.program_id(1)
    rows = 2 * ls

    @pl.when(p == 0)
    def _():
        r = lax.broadcasted_iota(jnp.int32, (rows, LANES), 0)
        c = lax.broadcasted_iota(jnp.int32, (rows, LANES), 1)
        own_map = (r // ls) == (c // DK_A)
        for h in range(H_A):
            qh = q_ref[:, h * LANES:(h + 1) * LANES]
            qrow_scr[h] = jnp.where(own_map, jnp.concatenate([qh, qh], axis=0), 0.0).astype(BF16)
        m_scr[...] = jnp.full(m_scr.shape, NEG, F32)
        l_scr[...] = jnp.zeros(l_scr.shape, F32)
        acc_scr[...] = jnp.zeros(acc_scr.shape, F32)

    def accumulate(h, s, vv):
        mx = m_scr[h]
        mn = jnp.maximum(mx, jnp.max(s, axis=-1, keepdims=True))
        alpha = jnp.exp(mx - mn)
        pr = jnp.exp(s - mn)
        l_scr[h] = alpha * l_scr[h] + jnp.sum(pr, axis=-1, keepdims=True)
        acc_scr[h] = alpha * acc_scr[h] + _dot(pr.astype(BF16), vv)
        m_scr[h] = mn

    for h in range(H_A):
        head_rows = pl.ds(h, page, stride=H_A)
        kh = jnp.concatenate([kp[i][head_rows, :] for i in range(n_pp)], axis=0).astype(BF16)
        vh = jnp.concatenate([vp[i][head_rows, :] for i in range(n_pp)], axis=0).astype(BF16)
        accumulate(h, _dot_nt(qrow_scr[h], kh), vh)

    @pl.when(p == pl.num_programs(1) - 1)
    def _():
        lam = lam_ref[0, 0]
        w = w_ref[...]
        jj = lax.broadcasted_iota(jnp.int32, (rows, ls), 1)
        qq = lax.broadcasted_iota(jnp.int32, (rows, ls), 0) % ls
        for h in range(H_A):
            cs = slice(h * LANES, (h + 1) * LANES)
            s = _dot_nt(qrow_scr[h], kn_ref[:, cs].astype(BF16))
            accumulate(h, jnp.where(jj <= qq, s, NEG), vn_ref[:, cs].astype(BF16))
            o = acc_scr[h] / l_scr[h]
            o = o[:ls] - lam * o[ls:]
            o_ref[:, cs] = _attn_finish(o, lam_init, w, z_ref[:, cs]).astype(o_ref.dtype)


def _attn_decode(lam, q, k, v, rest, subln_w, cache_k, cache_v, page_table, layer, Bs, Ls, lam_init):
    n_pages = page_table.shape[1]
    page = cache_k.shape[2]
    n_pp = math.gcd(n_pages, DECODE_PAGES)
    rows = 2 * Ls
    ck = cache_k.reshape(cache_k.shape[0], cache_k.shape[1], page * H_A, 2 * DK_A)
    cv = cache_v.reshape(cache_v.shape[0], cache_v.shape[1], page * H_A, DV_A)

    def page_spec(i):
        return pl.BlockSpec((None, None, page * H_A, LANES),
                            lambda b, p, pt: (layer, pt[b, p * n_pp + i], 0, 0))

    tok = pl.BlockSpec((Ls, W_GROUP), lambda b, p, pt: (b, 0))
    kern = functools.partial(_decode_kernel, n_pp=n_pp, ls=Ls, page=page, lam_init=lam_init)
    grid_spec = pltpu.PrefetchScalarGridSpec(
        num_scalar_prefetch=1,
        grid=(Bs, n_pages // n_pp),
        in_specs=[
            pl.BlockSpec(memory_space=pltpu.SMEM),
            tok, tok, tok,
            pl.BlockSpec((Ls, W_GROUP), lambda b, p, pt: (b, 2)),
            pl.BlockSpec((1, LANES), lambda b, p, pt: (0, 0)),
        ] + [page_spec(i) for i in range(n_pp)] * 2,
        out_specs=tok,
        scratch_shapes=[
            pltpu.VMEM((H_A, rows, LANES), BF16),
            pltpu.VMEM((H_A, rows, 1), F32),
            pltpu.VMEM((H_A, rows, 1), F32),
            pltpu.VMEM((H_A, rows, DV_A), F32),
        ],
    )
    return pl.pallas_call(
        kern,
        grid_spec=grid_spec,
        out_shape=jax.ShapeDtypeStruct((Bs * Ls, W_GROUP), F32),
        compiler_params=_cparams(("arbitrary", "arbitrary")),
        name="attn_decode",
    )(page_table, lam, q, k, v, rest, subln_w, *([ck] * n_pp), *([cv] * n_pp))


CONV_PAD = SUBLANES
CONV_TAIL = CONV_PAD - (CONV_W - 1)


def _conv_step(x_ref, cw_ref, cb_ref, xp_scr, q):
    xp_scr[CONV_PAD:CONV_PAD + q, :] = x_ref[...]
    y = xp_scr[CONV_TAIL:CONV_TAIL + q, :] * cw_ref[0:1, :]
    for t in range(1, CONV_W):
        y = y + xp_scr[CONV_TAIL + t:CONV_TAIL + t + q, :] * cw_ref[t:t + 1, :]
    y = y + cb_ref[...]
    xp_scr[CONV_TAIL:CONV_PAD, :] = xp_scr[q + CONV_TAIL:q + CONV_PAD, :]
    return y


def _ssd_kernel(xbc_ref, dt_ref, zb_ref, buf_ref, s0_ref, cw_ref, cb_ref, dtb_ref, alog_ref,
                dvec_ref, nw_ref, y_ref, sout_ref, cout_ref, xp_scr, s_scr, y_scr, *, q, nb):
    c = pl.program_id(1)

    @pl.when(c == 0)
    def _():
        s_scr[...] = s0_ref[...]
        xp_scr[:, CONV_TAIL:CONV_PAD, :] = buf_ref[...]

    tri = _lower_tri(q)
    tri_f = tri.astype(F32)
    a = -jnp.exp(alog_ref[...])
    heads_per_group = H_B // G_B
    for b in range(nb):
        xbc = _silu(_conv_step(xbc_ref.at[b], cw_ref, cb_ref, xp_scr.at[b], q))
        xs = xbc[:, :W_GROUP]
        dt = _softplus(dt_ref[b] + dtb_ref[...])
        cum = _dot_exact(tri_f, dt * a)
        cum_t = cum.T
        dt_t = dt.T
        last = cum[q - 1:q, :]
        ecum = jnp.exp(cum)
        wj = jnp.exp(last - cum) * dt
        elast = jnp.exp(last)
        for g in range(G_B):
            bg = xbc[:, W_GROUP + g * N_B:W_GROUP + (g + 1) * N_B]
            cg = xbc[:, W_GROUP + (G_B + g) * N_B:W_GROUP + (G_B + g + 1) * N_B]
            bgb = bg.astype(BF16)
            cb_g = _dot_nt(cg.astype(BF16), bgb)
            for hh in range(heads_per_group):
                h = g * heads_per_group + hh
                hs = slice(h * P_B, (h + 1) * P_B)
                seg = cum[:, h:h + 1] - cum_t[h:h + 1, :]
                dec = jnp.exp(jnp.where(tri, seg, NEG))
                mh = (cb_g * dec * dt_t[h:h + 1, :]).astype(BF16)
                xh = xs[:, hs]
                s_old = s_scr[b, h]
                yh = _dot(mh, xh.astype(BF16)) + _dot_nt((cg * ecum[:, h:h + 1]).astype(BF16),
                                                         s_old.astype(BF16))
                y_scr[b, :, hs] = yh
                xw = (xh * wj[:, h:h + 1]).astype(BF16)
                s_scr[b, h] = elast[:, h:h + 1] * s_old + _dot_tn(xw, bgb)
        y = y_scr[b] + dvec_ref[...] * xs
        y_ref[b] = _rmsnorm_rows(y * _silu(zb_ref[b]), nw_ref[...]).astype(y_ref.dtype)

    @pl.when(c == pl.num_programs(1) - 1)
    def _():
        sout_ref[...] = s_scr[...]
        cout_ref[...] = xp_scr[:, CONV_TAIL:CONV_PAD, :]


def _ssd(rest, small, buf, s0, cw, cb, dtb, alog, dvec, nw, B, L, out_dtype):
    q = math.gcd(L, 64)
    nc = L // q
    nb = _batch_group(B, 4)
    kern = functools.partial(_ssd_kernel, q=q, nb=nb)
    full = lambda shape: pl.BlockSpec(shape, lambda b, c: (0,) * len(shape))
    rest3 = rest.reshape(B, L, REST_COLS)
    return pl.pallas_call(
        kern,
        grid=(B // nb, nc),
        in_specs=[
            pl.BlockSpec((nb, q, CONV_B_CH), lambda b, c: (b, c, 0)),
            pl.BlockSpec((nb, q, LANES), lambda b, c: (b, c, 0)),
            pl.BlockSpec((nb, q, W_GROUP), lambda b, c: (b, c, 3)),
            pl.BlockSpec((nb, CONV_W - 1, CONV_B_CH), lambda b, c: (b, 0, 0)),
            pl.BlockSpec((nb, H_B, P_B, N_B), lambda b, c: (b, 0, 0, 0)),
            full((CONV_W, CONV_B_CH)), full((1, CONV_B_CH)), full((1, LANES)), full((1, LANES)),
            full((1, W_GROUP)), full((1, W_GROUP)),
        ],
        out_specs=(
            pl.BlockSpec((nb, q, W_GROUP), lambda b, c: (b, c, 0)),
            pl.BlockSpec((nb, H_B, P_B, N_B), lambda b, c: (b, 0, 0, 0)),
            pl.BlockSpec((nb, CONV_W - 1, CONV_B_CH), lambda b, c: (b, 0, 0)),
        ),
        out_shape=(
            jax.ShapeDtypeStruct((B, L, W_GROUP), out_dtype),
            jax.ShapeDtypeStruct((B, H_B, P_B, N_B), F32),
            jax.ShapeDtypeStruct((B, CONV_W - 1, CONV_B_CH), F32),
        ),
        scratch_shapes=[
            pltpu.VMEM((nb, q + SUBLANES, CONV_B_CH), F32),
            pltpu.VMEM((nb, H_B, P_B, N_B), F32),
            pltpu.VMEM((nb, q, W_GROUP), F32),
        ],
        compiler_params=_cparams(("arbitrary", "arbitrary")),
        name="ssd",
    )(rest3, small.reshape(B, L, LANES), rest3, buf, s0, cw, cb, dtb, alog, dvec, nw)


def _gla_kernel(qk_ref, v_ref, gc_ref, sm_ref, s0_ref, w2_ref, gb_ref, nw_ref,
                y_ref, sout_ref, st_scr, *, q, nb):
    c = pl.program_id(1)

    @pl.when(c == 0)
    def _():
        st_scr[...] = s0_ref[...]

    tri = _lower_tri(q)
    tri_f = tri.astype(F32)
    width = H_C * DK_C
    nw = nw_ref[...]
    for b in range(nb):
        gkl = _dot(sm_ref[b].astype(BF16), w2_ref[...]) + gb_ref[...]
        g = (jnp.minimum(gkl, 0.0) - jnp.log1p(jnp.exp(-jnp.abs(gkl)))) * (1.0 / GK_NORM)
        bc = _dot_exact(tri_f, g)
        last = bc[q - 1:q, :]
        qc = qk_ref[b, :, :width] * (DK_C ** -0.5)
        kc = qk_ref[b, :, width:]
        qe = (qc * jnp.exp(bc)).astype(BF16)
        ke = (kc * jnp.exp(-bc)).astype(BF16)
        kl = (kc * jnp.exp(last - bc)).astype(BF16)
        el = jnp.exp(last)
        for h in range(H_C):
            ks = slice(h * DK_C, (h + 1) * DK_C)
            vs = slice(h * DV_C, (h + 1) * DV_C)
            att = jnp.where(tri, _dot_nt(qe[:, ks], ke[:, ks]), 0.0)
            vh = v_ref[b, :, vs].astype(BF16)
            st_old = st_scr[b, h]
            o = _dot(att.astype(BF16), vh) + _dot_nt(qe[:, ks], st_old.astype(BF16))
            st_scr[b, h] = el[:, ks] * st_old + _dot_tn(vh, kl[:, ks])
            y_ref[b, :, vs] = (_rmsnorm_rows(o, nw) * _silu(gc_ref[b, :, vs])).astype(y_ref.dtype)

    @pl.when(c == pl.num_programs(1) - 1)
    def _():
        sout_ref[...] = st_scr[...]


def _gla(rest, small, s0_t, w2p, gb, nw, B, L, out_dtype):
    q = min(L, 64)
    nc = L // q
    nb = _batch_group(B, 4)
    kern = functools.partial(_gla_kernel, q=q, nb=nb)
    full = lambda shape: pl.BlockSpec(shape, lambda b, c: (0,) * len(shape))
    state = pl.BlockSpec((nb, H_C, DV_C, DK_C), lambda b, c: (b, 0, 0, 0))
    rest3 = rest.reshape(B, L, REST_COLS)
    return pl.pallas_call(
        kern,
        grid=(B // nb, nc),
        in_specs=[
            pl.BlockSpec((nb, q, W_GROUP), lambda b, c: (b, c, 4)),
            pl.BlockSpec((nb, q, W_GROUP), lambda b, c: (b, c, 5)),
            pl.BlockSpec((nb, q, W_GROUP), lambda b, c: (b, c, 6)),
            pl.BlockSpec((nb, q, LANES), lambda b, c: (b, c, 0)),
            state,
            full((LANES, H_C * DK_C)), full((1, H_C * DK_C)), full((1, DV_C)),
        ],
        out_specs=(pl.BlockSpec((nb, q, W_GROUP), lambda b, c: (b, c, 0)), state),
        out_shape=(
            jax.ShapeDtypeStruct((B, L, W_GROUP), out_dtype),
            jax.ShapeDtypeStruct((B, H_C, DV_C, DK_C), F32),
        ),
        scratch_shapes=[pltpu.VMEM((nb, H_C, DV_C, DK_C), F32)],
        compiler_params=_cparams(("arbitrary", "arbitrary")),
        name="gla",
    )(rest3, rest3, rest3, small.reshape(B, L, LANES), s0_t, w2p, gb, nw)


def _lru_kernel(xd_ref, zd_ref, buf_ref, h0_ref, cw_ref, cb_ref, wa_ref, ba_ref, wx_ref, bx_ref,
                lam_ref, y_ref, hout_ref, cout_ref, xp_scr, h_scr, a_scr, u_scr, *, q, nb):
    c = pl.program_id(1)

    @pl.when(c == 0)
    def _():
        h_scr[...] = h0_ref[...]
        xp_scr[:, CONV_TAIL:CONV_PAD, :] = buf_ref[...]

    sp_lam = _softplus(-lam_ref[...])
    for b in range(nb):
        xr = _conv_step(xd_ref.at[b], cw_ref, cb_ref, xp_scr.at[b], q)
        xb = xr.astype(BF16)
        r = _sigmoid(_dot(xb, wa_ref[...]) + ba_ref[...])
        i = _sigmoid(_dot(xb, wx_ref[...]) + bx_ref[...])
        log_a = -LRU_C * r * sp_lam
        a = jnp.exp(log_a)
        th = jnp.tanh(log_a)
        u = jnp.sqrt(-2.0 * th / (1.0 - th)) * (i * xr)
        r8 = lax.broadcasted_iota(jnp.int32, a.shape, 0) % SUBLANES
        for s in (1, 2, 4):
            keep = r8 >= s
            a_sh = pltpu.roll(a, s, 0)
            u_sh = pltpu.roll(u, s, 0)
            u = jnp.where(keep, a * u_sh + u, u)
            a = jnp.where(keep, a * a_sh, a)
        a_scr[b] = a
        u_scr[b] = u
    carry = [h_scr[b] for b in range(nb)]
    for gi in range(q // SUBLANES):
        rows = slice(gi * SUBLANES, (gi + 1) * SUBLANES)
        for b in range(nb):
            hg = a_scr[b, rows, :] * carry[b] + u_scr[b, rows, :]
            y_ref[b, rows, :] = (hg * _silu(zd_ref[b, rows, :])).astype(y_ref.dtype)
            carry[b] = hg[SUBLANES - 1:SUBLANES, :]
    for b in range(nb):
        h_scr[b] = carry[b]

    @pl.when(c == pl.num_programs(1) - 1)
    def _():
        hout_ref[...] = h_scr[...]
        cout_ref[...] = xp_scr[:, CONV_TAIL:CONV_PAD, :]


def _lru(rest, buf, h0, cw, cb, wa, ba, wx, bx, lam, B, L, out_dtype):
    q = min(L, 256)
    nc = L // q
    nb = _batch_group(B, 4)
    kern = functools.partial(_lru_kernel, q=q, nb=nb)
    full = lambda shape: pl.BlockSpec(shape, lambda b, c: (0,) * len(shape))
    rest3 = rest.reshape(B, L, REST_COLS)
    return pl.pallas_call(
        kern,
        grid=(B // nb, nc),
        in_specs=[
            pl.BlockSpec((nb, q, W_D), lambda b, c: (b, c, 7)),
            pl.BlockSpec((nb, q, W_D), lambda b, c: (b, c, 8)),
            pl.BlockSpec((nb, CONV_W - 1, W_D), lambda b, c: (b, 0, 0)),
            pl.BlockSpec((nb, 1, W_D), lambda b, c: (b, 0, 0)),
            full((CONV_W, W_D)), full((1, W_D)),
            full((W_D, W_D)), full((1, W_D)), full((W_D, W_D)), full((1, W_D)), full((1, W_D)),
        ],
        out_specs=(
            pl.BlockSpec((nb, q, W_D), lambda b, c: (b, c, 0)),
            pl.BlockSpec((nb, 1, W_D), lambda b, c: (b, 0, 0)),
            pl.BlockSpec((nb, CONV_W - 1, W_D), lambda b, c: (b, 0, 0)),
        ),
        out_shape=(
            jax.ShapeDtypeStruct((B, L, W_D), out_dtype),
            jax.ShapeDtypeStruct((B, 1, W_D), F32),
            jax.ShapeDtypeStruct((B, CONV_W - 1, W_D), F32),
        ),
        scratch_shapes=[
            pltpu.VMEM((nb, q + SUBLANES, W_D), F32),
            pltpu.VMEM((nb, 1, W_D), F32),
            pltpu.VMEM((nb, q, W_D), F32),
            pltpu.VMEM((nb, q, W_D), F32),
        ],
        compiler_params=_cparams(("arbitrary", "arbitrary")),
        name="lru",
    )(rest3, rest3, buf, h0, cw, cb, wa, ba, wx, bx, lam)


def _outproj_kernel(ya_ref, yb_ref, yc_ref, yd_ref, w_ref, x_ref, fw_ref, o_ref, *, final):
    out = _dot(ya_ref[...].astype(BF16), w_ref[0])
    for g, y_ref in enumerate((yb_ref, yc_ref, yd_ref), start=1):
        out = out + _dot(y_ref[...].astype(BF16), w_ref[g])
    res = x_ref[...] + out
    if final:
        res = _rmsnorm_rows(res, fw_ref[...])
    o_ref[...] = res


def _outproj(ys, w_out4, x2d, final_w, final):
    T = x2d.shape[0]
    tm = min(512, T)
    kern = functools.partial(_outproj_kernel, final=final)
    yspec = pl.BlockSpec((tm, W_GROUP), lambda i: (i, 0))
    return pl.pallas_call(
        kern,
        grid=(T // tm,),
        in_specs=[yspec] * 4 + [
            pl.BlockSpec((4, W_GROUP, D_MODEL), lambda i: (0, 0, 0)),
            pl.BlockSpec((tm, D_MODEL), lambda i: (i, 0)),
            pl.BlockSpec((1, D_MODEL), lambda i: (0, 0)),
        ],
        out_specs=pl.BlockSpec((tm, D_MODEL), lambda i: (i, 0)),
        out_shape=jax.ShapeDtypeStruct((T, D_MODEL), F32),
        compiler_params=_cparams(("arbitrary",)),
        name="outproj",
    )(*ys, w_out4, x2d, final_w)


def _rope_table(pos, rows):
    half = ROPE_DIM // 2
    inv = ROPE_THETA ** (-jnp.arange(half, dtype=F32) / half)
    ang = pos.astype(F32)[:, None] * inv[None, :]
    cos, sin = jnp.cos(ang), jnp.sin(ang)
    n = pos.shape[0]
    pad = jnp.zeros((n, DK_A - ROPE_DIM), F32)
    c64 = jnp.concatenate([cos, cos, pad + 1.0], axis=1)
    s1 = jnp.concatenate([-sin, jnp.zeros_like(sin), pad], axis=1)
    s2 = jnp.concatenate([jnp.zeros_like(sin), sin, pad], axis=1)
    tab = jnp.stack([jnp.tile(t, (1, LANES // DK_A)) for t in (c64, s1, s2)])
    return jnp.tile(tab, (1, rows // n, 1))


def _block_diag(w):
    nb, bw, _ = w.shape
    eye = jnp.eye(nb, dtype=w.dtype)
    return (eye[:, None, :, None] * w[:, :, None, :]).reshape(nb * bw, nb * bw)


def _prep_layer(l, norm_w, w_in, w_out, lam_q1, lam_k1, lam_q2, lam_k2, subln_w, ssd_conv_w,
                ssd_conv_b, ssd_dt_bias, ssd_a_log, ssd_d, ssd_norm_w, gla_gk_w2, gla_gk_b,
                gla_norm_w, lru_conv_w, lru_conv_b, lru_wa, lru_ba, lru_wx, lru_bx, lru_lambda):
    w = w_in[l]
    cols = lambda a, n: w[:, a:a + n]
    w_main = jnp.stack([
        cols(_QA, 512), cols(_KA, 512), cols(_VA, 512), cols(_XBC, 512), cols(_XBC + 512, 512),
        cols(_ZA, 512), cols(_ZB, 512), jnp.concatenate([cols(_QC, 256), cols(_KC, 256)], axis=1),
        cols(_VC, 512), cols(_GC, 512), cols(_XD, 512), cols(_ZD, 512)]).astype(BF16)
    w_small = jnp.concatenate([cols(_DT, H_B), cols(_GKLR, GK_RANK),
                               jnp.zeros((D_MODEL, LANES - H_B - GK_RANK), F32)], axis=1).astype(BF16)
    lam_init = 0.8 - 0.6 * math.exp(-0.3 * l)
    lam = (jnp.exp(jnp.sum(lam_q1[l] * lam_k1[l])) - jnp.exp(jnp.sum(lam_q2[l] * lam_k2[l]))
           + lam_init).reshape(1, 1).astype(F32)
    pad_lanes = lambda v: jnp.concatenate([v, jnp.zeros((LANES - v.shape[0],), F32)]).reshape(1, LANES)
    w2p = jnp.zeros((LANES, H_C * DK_C), F32).at[H_B:H_B + GK_RANK].set(gla_gk_w2[l]).astype(BF16)
    return dict(
        norm_w=norm_w[l].reshape(1, D_MODEL), w_main=w_main, w_small=w_small,
        w_out4=w_out[l].reshape(4, W_GROUP, D_MODEL).astype(BF16),
        lam=lam, lam_init=lam_init, subln_w=subln_w[l].reshape(1, DV_A),
        ssd_cw=ssd_conv_w[l], ssd_cb=ssd_conv_b[l].reshape(1, CONV_B_CH),
        ssd_dtb=pad_lanes(ssd_dt_bias[l]), ssd_alog=pad_lanes(ssd_a_log[l]),
        ssd_dvec=jnp.repeat(ssd_d[l], P_B).reshape(1, W_GROUP), ssd_nw=ssd_norm_w[l].reshape(1, W_GROUP),
        gla_w2p=w2p, gla_gb=gla_gk_b[l].reshape(1, H_C * DK_C), gla_nw=gla_norm_w[l].reshape(1, DV_C),
        lru_cw=lru_conv_w[l], lru_cb=lru_conv_b[l].reshape(1, W_D),
        lru_wa=_block_diag(lru_wa[l]).astype(BF16), lru_ba=lru_ba[l].reshape(1, W_D),
        lru_wx=_block_diag(lru_wx[l]).astype(BF16), lru_bx=lru_bx[l].reshape(1, W_D),
        lru_lam=lru_lambda[l].reshape(1, W_D),
    )


def _layer(x2d, B, L, rope_tab, p, states, past, layer, final_w, final):
    ssd_s0, ssd_buf, gla_s0, lru_h0, lru_buf = states
    ydt = BF16 if L % 16 == 0 else F32
    T = B * L
    q, k, v, rest, small = _inproj(x2d, p["norm_w"], p["w_main"], p["w_small"], rope_tab, L)
    if past is None:
        ya = _attn_prompt(p["lam"], q, k, v, rest, p["subln_w"], B, L, p["lam_init"])
    else:
        cache_k, cache_v, page_table = past
        ya = _attn_decode(p["lam"], q, k, v, rest, p["subln_w"], cache_k, cache_v, page_table,
                          layer, B, L, p["lam_init"])
    yb, ssd_s, ssd_c = _ssd(rest, small, ssd_buf, ssd_s0, p["ssd_cw"], p["ssd_cb"], p["ssd_dtb"],
                            p["ssd_alog"], p["ssd_dvec"], p["ssd_nw"], B, L, ydt)
    yc, gla_st = _gla(rest, small, jnp.swapaxes(gla_s0, -1, -2), p["gla_w2p"], p["gla_gb"],
                      p["gla_nw"], B, L, ydt)
    yd, lru_h, lru_c = _lru(rest, lru_buf, lru_h0.reshape(B, 1, W_D), p["lru_cw"], p["lru_cb"],
                            p["lru_wa"], p["lru_ba"], p["lru_wx"], p["lru_bx"], p["lru_lam"], B, L, ydt)
    ys = (ya, yb.reshape(T, W_GROUP), yc.reshape(T, W_GROUP), yd.reshape(T, W_D))
    x_new = _outproj(ys, p["w_out4"], x2d, final_w, final)
    new_state = (k.reshape(B, L, H_A, 2 * DK_A), v.reshape(B, L, H_A, DV_A), ssd_s, ssd_c,
                 jnp.swapaxes(gla_st, -1, -2), lru_h.reshape(B, W_D), lru_c)
    return x_new, new_state


def kernel(x_prompt, x_sample, cache_k, cache_v, page_table, state_ssd, state_ssd_conv, state_gla, state_lru, state_lru_conv, norm_w, w_in, w_out, lam_q1, lam_k1, lam_q2, lam_k2, subln_w, ssd_conv_w, ssd_conv_b, ssd_dt_bias, ssd_a_log, ssd_d, ssd_norm_w, gla_gk_w2, gla_gk_b, gla_norm_w, lru_conv_w, lru_conv_b, lru_wa, lru_ba, lru_wx, lru_bx, lru_lambda, final_norm_w):
    Bp, Lp, _ = x_prompt.shape
    Bs, Ls, _ = x_sample.shape
    depth = w_in.shape[0]
    past_len = page_table.shape[1] * cache_k.shape[2]
    tab_p = _rope_table(jnp.arange(Lp, dtype=jnp.int32), max(Lp, min(INPROJ_ROWS, Bp * Lp)))
    tab_s = _rope_table(past_len + jnp.arange(Ls, dtype=jnp.int32), max(Ls, min(INPROJ_ROWS, Bs * Ls)))
    final_w = final_norm_w.reshape(1, D_MODEL)
    hp = x_prompt.reshape(Bp * Lp, D_MODEL)
    hs = x_sample.reshape(Bs * Ls, D_MODEL)
    sp, ss = [], []
    for l in range(depth):
        p = _prep_layer(l, norm_w, w_in, w_out, lam_q1, lam_k1, lam_q2, lam_k2, subln_w, ssd_conv_w,
                        ssd_conv_b, ssd_dt_bias, ssd_a_log, ssd_d, ssd_norm_w, gla_gk_w2, gla_gk_b,
                        gla_norm_w, lru_conv_w, lru_conv_b, lru_wa, lru_ba, lru_wx, lru_bx, lru_lambda)
        final = l == depth - 1
        st_p = (jnp.zeros((Bp, H_B, P_B, N_B), F32), jnp.zeros((Bp, CONV_W - 1, CONV_B_CH), F32),
                jnp.zeros((Bp, H_C, DK_C, DV_C), F32), jnp.zeros((Bp, W_D), F32),
                jnp.zeros((Bp, CONV_W - 1, W_D), F32))
        hp, new_p = _layer(hp, Bp, Lp, tab_p, p, st_p, None, l, final_w, final)
        sp.append(new_p)
        st_s = (state_ssd[l], state_ssd_conv[l], state_gla[l], state_lru[l], state_lru_conv[l])
        hs, new_s = _layer(hs, Bs, Ls, tab_s, p, st_s, (cache_k, cache_v, page_table), l, final_w, final)
        ss.append(new_s)
    outs_p = [jnp.stack([s[i] for s in sp]) for i in range(7)]
    outs_s = [jnp.stack([s[i] for s in ss]) for i in range(7)]
    return (hp.reshape(Bp, Lp, D_MODEL), hs.reshape(Bs, Ls, D_MODEL), *outs_p, *outs_s)
```

```python
import functools
import math

import jax
import jax.numpy as jnp
from jax import lax
from jax.experimental import pallas as pl
from jax.experimental.pallas import tpu as pltpu

F32 = jnp.float32
BF16 = jnp.bfloat16
EPS = 1e-6
NEG = -1e30
LOG2E = math.log2(math.e)

D_MODEL = 2048
W_GROUP = 512
H_A, DV_A, DK_A = 4, 128, 64
ROPE_DIM, ROPE_THETA = 16, 500000.0
H_B, P_B, N_B, G_B = 8, 64, 128, 2
CONV_W = 4
CONV_B_CH = W_GROUP + 2 * G_B * N_B
H_C, DK_C, DV_C = 4, 64, 128
GK_RANK, GK_NORM = 16, 16.0
W_D, NB_D, BW_D = 512, 8, 64
LRU_C = 8.0

LANES = 128
SUBLANES = 8
COL_BLK = 512
INPROJ_ROWS = 1024
ATTN_BQ = 512
ATTN_BK = 512
DECODE_PAGES = 8
MIXER_SEQS = 4
VMEM_LIMIT = 56 * 1024 * 1024

_SPLITS = (512, 512, 512, 512, CONV_B_CH, H_B, W_GROUP, 256, 256, 512, GK_RANK, W_GROUP, W_D, W_D)
_OFF = [0]
for _s in _SPLITS:
    _OFF.append(_OFF[-1] + _s)
(_QA, _KA, _VA, _ZA, _XBC, _DT, _ZB, _QC, _KC, _VC, _GKLR, _GC, _XD, _ZD) = _OFF[:-1]

REST_COLS = 4608


def _cparams(sem):
    return pltpu.CompilerParams(dimension_semantics=sem, vmem_limit_bytes=VMEM_LIMIT)


def _sigmoid(x):
    return jax.nn.sigmoid(x)


def _silu(x):
    return x * _sigmoid(x)


def _softplus(x):
    return jnp.maximum(x, 0.0) + jnp.log1p(jnp.exp(-jnp.abs(x)))


def _rmsnorm_rows(x, w):
    return x * lax.rsqrt(jnp.mean(x * x, axis=-1, keepdims=True) + EPS) * w


def _dot(a, b):
    return jnp.dot(a, b, preferred_element_type=F32)


def _dot_nt(a, b):
    return lax.dot_general(a, b, (((1,), (1,)), ((), ())), preferred_element_type=F32)


def _dot_tn(a, b):
    return lax.dot_general(a, b, (((0,), (0,)), ((), ())), preferred_element_type=F32)


def _dot_exact(a, b):
    return jnp.dot(a, b, preferred_element_type=F32, precision=lax.Precision.HIGHEST)


def _lower_tri(n):
    r = lax.broadcasted_iota(jnp.int32, (n, n), 0)
    c = lax.broadcasted_iota(jnp.int32, (n, n), 1)
    return r >= c


def _batch_group(b, want):
    return math.gcd(b, want)


def _inproj_kernel(x_ref, nw_ref, w_ref, ws_ref, rope_ref,
                   q_ref, k_ref, v_ref, rest_ref, small_ref, kf_ref, vf_ref, u_scr):
    j = pl.program_id(1)
    tm = x_ref.shape[0]

    @pl.when(j == 0)
    def _():
        u = _rmsnorm_rows(x_ref[...], nw_ref[...]).astype(BF16)
        u_scr[...] = u
        small_ref[...] = _dot(u, ws_ref[...])

    acc = _dot(u_scr[...], w_ref[...])

    def rope_store(dst, scale, dst_heads=None):
        c, s1, s2 = rope_ref[0], rope_ref[1], rope_ref[2]
        for hh in range(COL_BLK // LANES):
            sl = slice(hh * LANES, (hh + 1) * LANES)
            a = acc[:, sl]
            r = (a * c + pltpu.roll(a, LANES - ROPE_DIM // 2, 1) * s1
                 + pltpu.roll(a, ROPE_DIM // 2, 1) * s2)
            dst[:, sl] = r * scale
            if dst_heads is not None:
                dst_heads[pl.ds(hh, tm, stride=H_A), :] = r

    @pl.when(j == 0)
    def _():
        rope_store(q_ref, DK_A ** -0.5 * LOG2E)

    @pl.when(j == 1)
    def _():
        rope_store(k_ref, 1.0, kf_ref)

    @pl.when(j == 2)
    def _():
        v_ref[...] = acc
        for hh in range(H_A):
            vf_ref[pl.ds(hh, tm, stride=H_A), :] = acc[:, hh * LANES:(hh + 1) * LANES]

    @pl.when(j >= 3)
    def _():
        rest_ref[...] = acc


def _inproj_kernel_aliased(x_ref, nw_ref, w_ref, ws_ref, rope_ref, kprev_ref, vprev_ref, *refs):
    del kprev_ref, vprev_ref
    _inproj_kernel(x_ref, nw_ref, w_ref, ws_ref, rope_ref, *refs)


def _inproj(x2d, norm_w, w_main, w_small, rope_tab, seq_len, layer, depth, kv_prev):
    T = x2d.shape[0]
    tm = min(INPROJ_ROWS, T)
    assert T % tm == 0 and (seq_len % tm == 0 or tm % seq_len == 0)
    n_pos_blk = max(seq_len // tm, 1)
    nj = w_main.shape[0]
    kv_shape = jax.ShapeDtypeStruct((depth, T * H_A, LANES), F32)
    out_shape = (
        jax.ShapeDtypeStruct((T, COL_BLK), F32),
        jax.ShapeDtypeStruct((T, COL_BLK), F32),
        jax.ShapeDtypeStruct((T, COL_BLK), F32),
        jax.ShapeDtypeStruct((T, REST_COLS), F32),
        jax.ShapeDtypeStruct((T, LANES), F32),
        kv_shape, kv_shape,
    )
    row_blk = pl.BlockSpec((tm, COL_BLK), lambda i, j: (i, 0))
    kv_blk = pl.BlockSpec((None, tm * H_A, LANES), lambda i, j: (layer, i, 0))
    in_specs = [
        pl.BlockSpec((tm, D_MODEL), lambda i, j: (i, 0)),
        pl.BlockSpec((1, D_MODEL), lambda i, j: (0, 0)),
        pl.BlockSpec((None, D_MODEL, COL_BLK), lambda i, j: (j, 0, 0)),
        pl.BlockSpec((D_MODEL, LANES), lambda i, j: (0, 0)),
        pl.BlockSpec((3, tm, LANES), lambda i, j: (0, i % n_pos_blk, 0)),
    ]
    args = (x2d, norm_w, w_main, w_small, rope_tab)
    aliases = {}
    kern = _inproj_kernel
    if kv_prev is not None:
        in_specs += [pl.BlockSpec(memory_space=pl.ANY)] * 2
        aliases = {len(args): 5, len(args) + 1: 6}
        args += tuple(kv_prev)
        kern = _inproj_kernel_aliased
    return pl.pallas_call(
        kern,
        grid=(T // tm, nj),
        in_specs=in_specs,
        out_specs=(
            row_blk, row_blk, row_blk,
            pl.BlockSpec((tm, COL_BLK), lambda i, j: (i, jnp.maximum(j - 3, 0))),
            pl.BlockSpec((tm, LANES), lambda i, j: (i, 0)),
            kv_blk, kv_blk,
        ),
        out_shape=out_shape,
        input_output_aliases=aliases,
        scratch_shapes=[pltpu.VMEM((tm, D_MODEL), BF16)],
        compiler_params=_cparams(("arbitrary", "arbitrary")),
        name="inproj",
    )(*args)


def _attn_finish(o, lam_init, w, z):
    o = _rmsnorm_rows(o, w) * (1.0 - lam_init)
    return o * _silu(z)


def _attn_kernel(lam_ref, q_ref, k_ref, v_ref, z_ref, w_ref, o_ref, kb_scr, vt_scr,
                 m_scr, l_scr, acc_scr, *, bq, bk, lam_init):
    qi = pl.program_id(2)
    n_kb = kb_scr.shape[0]
    diag_blocks = bq // bk

    @pl.when(qi == 0)
    def _():
        for c in range(n_kb):
            rows = slice(c * bk, (c + 1) * bk)
            kb_scr[c] = k_ref[rows, :].astype(BF16)
            vt_scr[c] = v_ref[rows, :].T.astype(BF16)

    qt = q_ref[...].T
    feat = lax.broadcasted_iota(jnp.int32, qt.shape, 0)
    qts = (jnp.where(feat < DK_A, qt, 0.0).astype(BF16),
           jnp.where(feat >= DK_A, qt, 0.0).astype(BF16))
    m_scr[...] = jnp.full(m_scr.shape, NEG, F32)
    l_scr[...] = jnp.zeros(l_scr.shape, F32)
    acc_scr[...] = jnp.zeros(acc_scr.shape, F32)

    def step(j, mask):
        kb = kb_scr[j]
        vtb = vt_scr[j]
        ss = [_dot(kb, qts[m]) for m in range(2)]
        for m in range(2):
            s = ss[m]
            if mask is not None:
                s = jnp.where(mask, s, NEG)
            mx = m_scr[m]
            mn = jnp.maximum(mx, jnp.max(s, axis=0, keepdims=True))
            alpha = jnp.exp2(mx - mn)
            p = jnp.exp2(s - mn)
            l_scr[m] = alpha * l_scr[m] + jnp.sum(p, axis=0, keepdims=True)
            acc_scr[m] = alpha * acc_scr[m] + _dot(vtb, p.astype(BF16))
            m_scr[m] = mn

    def unmasked(j, carry):
        step(j, None)
        return carry

    lax.fori_loop(0, qi * diag_blocks, unmasked, 0)
    key_i = lax.broadcasted_iota(jnp.int32, (bk, bq), 0)
    qry_i = lax.broadcasted_iota(jnp.int32, (bk, bq), 1)
    for d in range(diag_blocks):
        step(qi * diag_blocks + d, key_i + d * bk <= qry_i)
    lam = lam_ref[0, 0]
    o = (acc_scr[0] / l_scr[0] - lam * (acc_scr[1] / l_scr[1])).T
    o_ref[...] = _attn_finish(o, lam_init, w_ref[...], z_ref[...]).astype(o_ref.dtype)


def _attn_prompt(lam, q, k, v, rest, subln_w, B, L, lam_init):
    bq = min(ATTN_BQ, L)
    bk = min(ATTN_BK, bq)
    nq = L // bq
    kern = functools.partial(_attn_kernel, bq=bq, bk=bk, lam_init=lam_init)
    return pl.pallas_call(
        kern,
        grid=(B, H_A, nq),
        in_specs=[
            pl.BlockSpec(memory_space=pltpu.SMEM),
            pl.BlockSpec((bq, LANES), lambda b, h, i: (b * nq + i, h)),
            pl.BlockSpec((L, LANES), lambda b, h, i: (b, h)),
            pl.BlockSpec((L, LANES), lambda b, h, i: (b, h)),
            pl.BlockSpec((bq, LANES), lambda b, h, i: (b * nq + i, 8 + h)),
            pl.BlockSpec((1, LANES), lambda b, h, i: (0, 0)),
        ],
        out_specs=pl.BlockSpec((bq, LANES), lambda b, h, i: (b * nq + i, h)),
        out_shape=jax.ShapeDtypeStruct((B * L, W_GROUP), BF16),
        scratch_shapes=[
            pltpu.VMEM((L // bk, bk, LANES), BF16),
            pltpu.VMEM((L // bk, DV_A, bk), BF16),
            pltpu.VMEM((2, 1, bq), F32),
            pltpu.VMEM((2, 1, bq), F32),
            pltpu.VMEM((2, DV_A, bq), F32),
        ],
        compiler_params=_cparams(("arbitrary", "arbitrary", "arbitrary")),
        name="attn_prompt",
    )(lam, q, k, v, rest, subln_w)


def _decode_kernel(pt_ref, lam_ref, q_ref, kn_ref, vn_ref, z_ref, w_ref, *refs,
                   n_pp, ls, page, lam_init):
    kp = refs[:n_pp]
    vp = refs[n_pp:2 * n_pp]
    o_ref = refs[2 * n_pp]
    qrow_scr, m_scr, l_scr, acc_scr = refs[2 * n_pp + 1:]
    p = pl.program_id(1)
    rows = 2 * ls

    @pl.when(p == 0)
    def _():
        r = lax.broadcasted_iota(jnp.int32, (rows, LANES), 0)
        c = lax.broadcasted_iota(jnp.int32, (rows, LANES), 1)
        own_map = (r // ls) == (c // DK_A)
        for h in range(H_A):
            qh = q_ref[:, h * LANES:(h + 1) * LANES]
            qrow_scr[h] = jnp.where(own_map, jnp.concatenate([qh, qh], axis=0), 0.0).astype(BF16)
        m_scr[...] = jnp.full(m_scr.shape, NEG, F32)
        l_scr[...] = jnp.zeros(l_scr.shape, F32)
        acc_scr[...] = jnp.zeros(acc_scr.shape, F32)

    def accumulate(h, s, vv):
        mx = m_scr[h]
        mn = jnp.maximum(mx, jnp.max(s, axis=-1, keepdims=True))
        alpha = jnp.exp2(mx - mn)
        pr = jnp.exp2(s - mn)
        l_scr[h] = alpha * l_scr[h] + jnp.sum(pr, axis=-1, keepdims=True)
        acc_scr[h] = alpha * acc_scr[h] + _dot(pr.astype(BF16), vv)
        m_scr[h] = mn

    for h in range(H_A):
        head_rows = pl.ds(h, page, stride=H_A)
        kh = jnp.concatenate([kp[i][head_rows, :] for i in range(n_pp)], axis=0).astype(BF16)
        vh = jnp.concatenate([vp[i][head_rows, :] for i in range(n_pp)], axis=0).astype(BF16)
        accumulate(h, _dot_nt(qrow_scr[h], kh), vh)

    @pl.when(p == pl.num_programs(1) - 1)
    def _():
        lam = lam_ref[0, 0]
        w = w_ref[...]
        jj = lax.broadcasted_iota(jnp.int32, (rows, ls), 1)
        qq = lax.broadcasted_iota(jnp.int32, (rows, ls), 0) % ls
        for h in range(H_A):
            cs = slice(h * LANES, (h + 1) * LANES)
            s = _dot_nt(qrow_scr[h], kn_ref[:, cs].astype(BF16))
            accumulate(h, jnp.where(jj <= qq, s, NEG), vn_ref[:, cs].astype(BF16))
            o = acc_scr[h] / l_scr[h]
            o = o[:ls] - lam * o[ls:]
            o_ref[:, cs] = _attn_finish(o, lam_init, w, z_ref[:, cs]).astype(o_ref.dtype)


def _attn_decode(lam, q, k, v, rest, subln_w, cache_k, cache_v, page_table, layer, Bs, Ls, lam_init):
    n_pages = page_table.shape[1]
    page = cache_k.shape[2]
    n_pp = math.gcd(n_pages, DECODE_PAGES)
    rows = 2 * Ls
    ck = cache_k.reshape(cache_k.shape[0], cache_k.shape[1], page * H_A, 2 * DK_A)
    cv = cache_v.reshape(cache_v.shape[0], cache_v.shape[1], page * H_A, DV_A)

    def page_spec(i):
        return pl.BlockSpec((None, None, page * H_A, LANES),
                            lambda b, p, pt: (layer, pt[b, p * n_pp + i], 0, 0))

    tok = pl.BlockSpec((Ls, W_GROUP), lambda b, p, pt: (b, 0))
    kern = functools.partial(_decode_kernel, n_pp=n_pp, ls=Ls, page=page, lam_init=lam_init)
    grid_spec = pltpu.PrefetchScalarGridSpec(
        num_scalar_prefetch=1,
        grid=(Bs, n_pages // n_pp),
        in_specs=[
            pl.BlockSpec(memory_space=pltpu.SMEM),
            tok, tok, tok,
            pl.BlockSpec((Ls, W_GROUP), lambda b, p, pt: (b, 2)),
            pl.BlockSpec((1, LANES), lambda b, p, pt: (0, 0)),
        ] + [page_spec(i) for i in range(n_pp)] * 2,
        out_specs=tok,
        scratch_shapes=[
            pltpu.VMEM((H_A, rows, LANES), BF16),
            pltpu.VMEM((H_A, rows, 1), F32),
            pltpu.VMEM((H_A, rows, 1), F32),
            pltpu.VMEM((H_A, rows, DV_A), F32),
        ],
    )
    return pl.pallas_call(
        kern,
        grid_spec=grid_spec,
        out_shape=jax.ShapeDtypeStruct((Bs * Ls, W_GROUP), F32),
        compiler_params=_cparams(("arbitrary", "arbitrary")),
        name="attn_decode",
    )(page_table, lam, q, k, v, rest, subln_w, *([ck] * n_pp), *([cv] * n_pp))


CONV_PAD = SUBLANES
CONV_TAIL = CONV_PAD - (CONV_W - 1)


def _conv_step(x_ref, cw_ref, cb_ref, xp_scr, q):
    xp_scr[CONV_PAD:CONV_PAD + q, :] = x_ref[...]
    y = xp_scr[CONV_TAIL:CONV_TAIL + q, :] * cw_ref[0:1, :]
    for t in range(1, CONV_W):
        y = y + xp_scr[CONV_TAIL + t:CONV_TAIL + t + q, :] * cw_ref[t:t + 1, :]
    y = y + cb_ref[...]
    xp_scr[CONV_TAIL:CONV_PAD, :] = xp_scr[q + CONV_TAIL:q + CONV_PAD, :]
    return y


def _ssd_kernel(xbc_ref, dt_ref, zb_ref, buf_ref, s0_ref, cw_ref, cb_ref, dtb_ref, alog_ref,
                dvec_ref, nw_ref, y_ref, sout_ref, cout_ref, xp_scr, s_scr, y_scr, *, q, nb):
    c = pl.program_id(1)

    @pl.when(c == 0)
    def _():
        s_scr[...] = s0_ref[...]
        xp_scr[:, CONV_TAIL:CONV_PAD, :] = buf_ref[...]

    tri = _lower_tri(q)
    tri_f = tri.astype(F32)
    a = -jnp.exp(alog_ref[...])
    heads_per_group = H_B // G_B
    for b in range(nb):
        xbc = _silu(_conv_step(xbc_ref.at[b], cw_ref, cb_ref, xp_scr.at[b], q))
        xs = xbc[:, :W_GROUP]
        dt = _softplus(dt_ref[b] + dtb_ref[...])
        cum = _dot_exact(tri_f, dt * a)
        cum_t = cum.T
        dt_t = dt.T
        last = cum[q - 1:q, :]
        ecum = jnp.exp(cum)
        wj = jnp.exp(last - cum) * dt
        elast = jnp.exp(last)
        for g in range(G_B):
            bg = xbc[:, W_GROUP + g * N_B:W_GROUP + (g + 1) * N_B]
            cg = xbc[:, W_GROUP + (G_B + g) * N_B:W_GROUP + (G_B + g + 1) * N_B]
            bgb = bg.astype(BF16)
            cb_g = _dot_nt(cg.astype(BF16), bgb)
            for hh in range(heads_per_group):
                h = g * heads_per_group + hh
                hs = slice(h * P_B, (h + 1) * P_B)
                seg = cum[:, h:h + 1] - cum_t[h:h + 1, :]
                dec = jnp.exp(jnp.where(tri, seg, NEG))
                mh = (cb_g * dec * dt_t[h:h + 1, :]).astype(BF16)
                xh = xs[:, hs]
                s_old = s_scr[b, h]
                yh = _dot(mh, xh.astype(BF16)) + _dot_nt((cg * ecum[:, h:h + 1]).astype(BF16),
                                                         s_old.astype(BF16))
                y_scr[b, :, hs] = yh
                xw = (xh * wj[:, h:h + 1]).astype(BF16)
                s_scr[b, h] = elast[:, h:h + 1] * s_old + _dot_tn(xw, bgb)
        y = y_scr[b] + dvec_ref[...] * xs
        y_ref[b] = _rmsnorm_rows(y * _silu(zb_ref[b]), nw_ref[...]).astype(y_ref.dtype)

    @pl.when(c == pl.num_programs(1) - 1)
    def _():
        sout_ref[...] = s_scr[...]
        cout_ref[...] = xp_scr[:, CONV_TAIL:CONV_PAD, :]


def _ssd(rest, small, buf, s0, cw, cb, dtb, alog, dvec, nw, B, L, out_dtype):
    q = math.gcd(L, 64)
    nc = L // q
    nb = _batch_group(B, MIXER_SEQS)
    kern = functools.partial(_ssd_kernel, q=q, nb=nb)
    full = lambda shape: pl.BlockSpec(shape, lambda b, c: (0,) * len(shape))
    rest3 = rest.reshape(B, L, REST_COLS)
    return pl.pallas_call(
        kern,
        grid=(B // nb, nc),
        in_specs=[
            pl.BlockSpec((nb, q, CONV_B_CH), lambda b, c: (b, c, 0)),
            pl.BlockSpec((nb, q, LANES), lambda b, c: (b, c, 0)),
            pl.BlockSpec((nb, q, W_GROUP), lambda b, c: (b, c, 3)),
            pl.BlockSpec((nb, CONV_W - 1, CONV_B_CH), lambda b, c: (b, 0, 0)),
            pl.BlockSpec((nb, H_B, P_B, N_B), lambda b, c: (b, 0, 0, 0)),
            full((CONV_W, CONV_B_CH)), full((1, CONV_B_CH)), full((1, LANES)), full((1, LANES)),
            full((1, W_GROUP)), full((1, W_GROUP)),
        ],
        out_specs=(
            pl.BlockSpec((nb, q, W_GROUP), lambda b, c: (b, c, 0)),
            pl.BlockSpec((nb, H_B, P_B, N_B), lambda b, c: (b, 0, 0, 0)),
            pl.BlockSpec((nb, CONV_W - 1, CONV_B_CH), lambda b, c: (b, 0, 0)),
        ),
        out_shape=(
            jax.ShapeDtypeStruct((B, L, W_GROUP), out_dtype),
            jax.ShapeDtypeStruct((B, H_B, P_B, N_B), F32),
            jax.ShapeDtypeStruct((B, CONV_W - 1, CONV_B_CH), F32),
        ),
        scratch_shapes=[
            pltpu.VMEM((nb, q + SUBLANES, CONV_B_CH), F32),
            pltpu.VMEM((nb, H_B, P_B, N_B), F32),
            pltpu.VMEM((nb, q, W_GROUP), F32),
        ],
        compiler_params=_cparams(("arbitrary", "arbitrary")),
        name="ssd",
    )(rest3, small.reshape(B, L, LANES), rest3, buf, s0, cw, cb, dtb, alog, dvec, nw)


def _gla_kernel(qk_ref, v_ref, gc_ref, sm_ref, s0_ref, w2_ref, gb_ref, nw_ref,
                y_ref, sout_ref, st_scr, *, q, nb):
    c = pl.program_id(1)

    @pl.when(c == 0)
    def _():
        st_scr[...] = s0_ref[...]

    seqs = range(nb)
    heads = range(H_C)
    tri = _lower_tri(q)
    width = H_C * DK_C
    nw = nw_ref[...]
    sm = sm_ref[...].reshape(nb * q, LANES).astype(BF16)
    gkl = _dot(sm, w2_ref[...]) + gb_ref[...]
    g = (jnp.minimum(gkl, 0.0) - jnp.log1p(jnp.exp(-jnp.abs(gkl)))) * (1.0 / GK_NORM)
    bc_all = _dot_exact(tri.astype(F32), jnp.concatenate([g[b * q:(b + 1) * q] for b in seqs], axis=1))
    bc = [bc_all[:, b * width:(b + 1) * width] for b in seqs]
    last = [x[q - 1:q, :] for x in bc]
    kc = [qk_ref[b, :, width:] for b in seqs]
    qe = [(qk_ref[b, :, :width] * (DK_C ** -0.5) * jnp.exp(bc[b])).astype(BF16) for b in seqs]
    ke = [(kc[b] * jnp.exp(-bc[b])).astype(BF16) for b in seqs]
    kl = [(kc[b] * jnp.exp(last[b] - bc[b])).astype(BF16) for b in seqs]
    el = [jnp.exp(x) for x in last]
    ks = [slice(h * DK_C, (h + 1) * DK_C) for h in heads]
    vs = [slice(h * DV_C, (h + 1) * DV_C) for h in heads]
    vh = [[v_ref[b, :, vs[h]].astype(BF16) for h in heads] for b in seqs]
    st_old = [[st_scr[b, h] for h in heads] for b in seqs]
    att = [[_dot_nt(qe[b][:, ks[h]], ke[b][:, ks[h]]) for h in heads] for b in seqs]
    o_state = [[_dot_nt(qe[b][:, ks[h]], st_old[b][h].astype(BF16)) for h in heads] for b in seqs]
    upd = [[_dot_tn(vh[b][h], kl[b][:, ks[h]]) for h in heads] for b in seqs]
    o = [[_dot(jnp.where(tri, att[b][h], 0.0).astype(BF16), vh[b][h]) + o_state[b][h]
          for h in heads] for b in seqs]
    for b in seqs:
        for h in heads:
            st_scr[b, h] = el[b][:, ks[h]] * st_old[b][h] + upd[b][h]
            y_ref[b, :, vs[h]] = (_rmsnorm_rows(o[b][h], nw)
                                  * _silu(gc_ref[b, :, vs[h]])).astype(y_ref.dtype)

    @pl.when(c == pl.num_programs(1) - 1)
    def _():
        sout_ref[...] = st_scr[...]


def _gla(rest, small, s0_t, w2p, gb, nw, B, L, out_dtype):
    q = min(L, 64)
    nc = L // q
    nb = _batch_group(B, MIXER_SEQS)
    kern = functools.partial(_gla_kernel, q=q, nb=nb)
    full = lambda shape: pl.BlockSpec(shape, lambda b, c: (0,) * len(shape))
    state = pl.BlockSpec((nb, H_C, DV_C, DK_C), lambda b, c: (b, 0, 0, 0))
    rest3 = rest.reshape(B, L, REST_COLS)
    return pl.pallas_call(
        kern,
        grid=(B // nb, nc),
        in_specs=[
            pl.BlockSpec((nb, q, W_GROUP), lambda b, c: (b, c, 4)),
            pl.BlockSpec((nb, q, W_GROUP), lambda b, c: (b, c, 5)),
            pl.BlockSpec((nb, q, W_GROUP), lambda b, c: (b, c, 6)),
            pl.BlockSpec((nb, q, LANES), lambda b, c: (b, c, 0)),
            state,
            full((LANES, H_C * DK_C)), full((1, H_C * DK_C)), full((1, DV_C)),
        ],
        out_specs=(pl.BlockSpec((nb, q, W_GROUP), lambda b, c: (b, c, 0)), state),
        out_shape=(
            jax.ShapeDtypeStruct((B, L, W_GROUP), out_dtype),
            jax.ShapeDtypeStruct((B, H_C, DV_C, DK_C), F32),
        ),
        scratch_shapes=[pltpu.VMEM((nb, H_C, DV_C, DK_C), F32)],
        compiler_params=_cparams(("arbitrary", "arbitrary")),
        name="gla",
    )(rest3, rest3, rest3, small.reshape(B, L, LANES), s0_t, w2p, gb, nw)


def _lru_kernel(xd_ref, zd_ref, buf_ref, h0_ref, cw_ref, cb_ref, wa_ref, ba_ref, wx_ref, bx_ref,
                lam_ref, y_ref, hout_ref, cout_ref, xp_scr, h_scr, a_scr, u_scr, *, q, nb):
    c = pl.program_id(1)

    @pl.when(c == 0)
    def _():
        h_scr[...] = h0_ref[...]
        xp_scr[:, CONV_TAIL:CONV_PAD, :] = buf_ref[...]

    sp_lam = _softplus(-lam_ref[...])
    for b in range(nb):
        xr = _conv_step(xd_ref.at[b], cw_ref, cb_ref, xp_scr.at[b], q)
        xb = xr.astype(BF16)
        r = _sigmoid(_dot(xb, wa_ref[...]) + ba_ref[...])
        i = _sigmoid(_dot(xb, wx_ref[...]) + bx_ref[...])
        log_a = -LRU_C * r * sp_lam
        a = jnp.exp(log_a)
        th = jnp.tanh(log_a)
        u = jnp.sqrt(-2.0 * th / (1.0 - th)) * (i * xr)
        a = a.reshape(q // SUBLANES, SUBLANES, W_D)
        u = u.reshape(q // SUBLANES, SUBLANES, W_D)
        r8 = lax.broadcasted_iota(jnp.int32, a.shape, 1)
        for s in (1, 2, 4):
            keep = r8 >= s
            a_sh = pltpu.roll(a, s, 1)
            u_sh = pltpu.roll(u, s, 1)
            u = jnp.where(keep, a * u_sh + u, u)
            a = jnp.where(keep, a * a_sh, a)
        a_scr[b] = a.reshape(q, W_D)
        u_scr[b] = u.reshape(q, W_D)
    carry = [h_scr[b] for b in range(nb)]
    for gi in range(q // SUBLANES):
        rows = slice(gi * SUBLANES, (gi + 1) * SUBLANES)
        for b in range(nb):
            hg = a_scr[b, rows, :] * carry[b] + u_scr[b, rows, :]
            y_ref[b, rows, :] = (hg * _silu(zd_ref[b, rows, :])).astype(y_ref.dtype)
            carry[b] = hg[SUBLANES - 1:SUBLANES, :]
    for b in range(nb):
        h_scr[b] = carry[b]

    @pl.when(c == pl.num_programs(1) - 1)
    def _():
        hout_ref[...] = h_scr[...]
        cout_ref[...] = xp_scr[:, CONV_TAIL:CONV_PAD, :]


def _lru(rest, buf, h0, cw, cb, wa, ba, wx, bx, lam, B, L, out_dtype):
    q = min(L, 256)
    nc = L // q
    nb = _batch_group(B, MIXER_SEQS)
    kern = functools.partial(_lru_kernel, q=q, nb=nb)
    full = lambda shape: pl.BlockSpec(shape, lambda b, c: (0,) * len(shape))
    rest3 = rest.reshape(B, L, REST_COLS)
    return pl.pallas_call(
        kern,
        grid=(B // nb, nc),
        in_specs=[
            pl.BlockSpec((nb, q, W_D), lambda b, c: (b, c, 7)),
            pl.BlockSpec((nb, q, W_D), lambda b, c: (b, c, 8)),
            pl.BlockSpec((nb, CONV_W - 1, W_D), lambda b, c: (b, 0, 0)),
            pl.BlockSpec((nb, 1, W_D), lambda b, c: (b, 0, 0)),
            full((CONV_W, W_D)), full((1, W_D)),
            full((W_D, W_D)), full((1, W_D)), full((W_D, W_D)), full((1, W_D)), full((1, W_D)),
        ],
        out_specs=(
            pl.BlockSpec((nb, q, W_D), lambda b, c: (b, c, 0)),
            pl.BlockSpec((nb, 1, W_D), lambda b, c: (b, 0, 0)),
            pl.BlockSpec((nb, CONV_W - 1, W_D), lambda b, c: (b, 0, 0)),
        ),
        out_shape=(
            jax.ShapeDtypeStruct((B, L, W_D), out_dtype),
            jax.ShapeDtypeStruct((B, 1, W_D), F32),
            jax.ShapeDtypeStruct((B, CONV_W - 1, W_D), F32),
        ),
        scratch_shapes=[
            pltpu.VMEM((nb, q + SUBLANES, W_D), F32),
            pltpu.VMEM((nb, 1, W_D), F32),
            pltpu.VMEM((nb, q, W_D), F32),
            pltpu.VMEM((nb, q, W_D), F32),
        ],
        compiler_params=_cparams(("arbitrary", "arbitrary")),
        name="lru",
    )(rest3, rest3, buf, h0, cw, cb, wa, ba, wx, bx, lam)


def _outproj_kernel(ya_ref, yb_ref, yc_ref, yd_ref, w_ref, x_ref, fw_ref, o_ref, *, final):
    out = _dot(ya_ref[...].astype(BF16), w_ref[0])
    for g, y_ref in enumerate((yb_ref, yc_ref, yd_ref), start=1):
        out = out + _dot(y_ref[...].astype(BF16), w_ref[g])
    res = x_ref[...] + out
    if final:
        res = _rmsnorm_rows(res, fw_ref[...])
    o_ref[...] = res


def _outproj(ys, w_out4, x2d, final_w, final):
    T = x2d.shape[0]
    tm = min(512, T)
    kern = functools.partial(_outproj_kernel, final=final)
    yspec = pl.BlockSpec((tm, W_GROUP), lambda i: (i, 0))
    return pl.pallas_call(
        kern,
        grid=(T // tm,),
        in_specs=[yspec] * 4 + [
            pl.BlockSpec((4, W_GROUP, D_MODEL), lambda i: (0, 0, 0)),
            pl.BlockSpec((tm, D_MODEL), lambda i: (i, 0)),
            pl.BlockSpec((1, D_MODEL), lambda i: (0, 0)),
        ],
        out_specs=pl.BlockSpec((tm, D_MODEL), lambda i: (i, 0)),
        out_shape=jax.ShapeDtypeStruct((T, D_MODEL), F32),
        compiler_params=_cparams(("arbitrary",)),
        name="outproj",
    )(*ys, w_out4, x2d, final_w)


def _rope_table(pos, rows):
    half = ROPE_DIM // 2
    inv = ROPE_THETA ** (-jnp.arange(half, dtype=F32) / half)
    ang = pos.astype(F32)[:, None] * inv[None, :]
    cos, sin = jnp.cos(ang), jnp.sin(ang)
    n = pos.shape[0]
    pad = jnp.zeros((n, DK_A - ROPE_DIM), F32)
    c64 = jnp.concatenate([cos, cos, pad + 1.0], axis=1)
    s1 = jnp.concatenate([-sin, jnp.zeros_like(sin), pad], axis=1)
    s2 = jnp.concatenate([jnp.zeros_like(sin), sin, pad], axis=1)
    tab = jnp.stack([jnp.tile(t, (1, LANES // DK_A)) for t in (c64, s1, s2)])
    return jnp.tile(tab, (1, rows // n, 1))


def _block_diag(w):
    nb, bw, _ = w.shape
    eye = jnp.eye(nb, dtype=w.dtype)
    return (eye[:, None, :, None] * w[:, :, None, :]).reshape(nb * bw, nb * bw)


def _prep_layer(l, norm_w, w_in, w_out, lam_q1, lam_k1, lam_q2, lam_k2, subln_w, ssd_conv_w,
                ssd_conv_b, ssd_dt_bias, ssd_a_log, ssd_d, ssd_norm_w, gla_gk_w2, gla_gk_b,
                gla_norm_w, lru_conv_w, lru_conv_b, lru_wa, lru_ba, lru_wx, lru_bx, lru_lambda):
    w = w_in[l]
    cols = lambda a, n: w[:, a:a + n]
    w_main = jnp.stack([
        cols(_QA, 512), cols(_KA, 512), cols(_VA, 512), cols(_XBC, 512), cols(_XBC + 512, 512),
        cols(_ZA, 512), cols(_ZB, 512), jnp.concatenate([cols(_QC, 256), cols(_KC, 256)], axis=1),
        cols(_VC, 512), cols(_GC, 512), cols(_XD, 512), cols(_ZD, 512)]).astype(BF16)
    w_small = jnp.concatenate([cols(_DT, H_B), cols(_GKLR, GK_RANK),
                               jnp.zeros((D_MODEL, LANES - H_B - GK_RANK), F32)], axis=1).astype(BF16)
    lam_init = 0.8 - 0.6 * math.exp(-0.3 * l)
    lam = (jnp.exp(jnp.sum(lam_q1[l] * lam_k1[l])) - jnp.exp(jnp.sum(lam_q2[l] * lam_k2[l]))
           + lam_init).reshape(1, 1).astype(F32)
    pad_lanes = lambda v: jnp.concatenate([v, jnp.zeros((LANES - v.shape[0],), F32)]).reshape(1, LANES)
    w2p = jnp.zeros((LANES, H_C * DK_C), F32).at[H_B:H_B + GK_RANK].set(gla_gk_w2[l]).astype(BF16)
    return dict(
        norm_w=norm_w[l].reshape(1, D_MODEL), w_main=w_main, w_small=w_small,
        w_out4=w_out[l].reshape(4, W_GROUP, D_MODEL).astype(BF16),
        lam=lam, lam_init=lam_init, subln_w=subln_w[l].reshape(1, DV_A),
        ssd_cw=ssd_conv_w[l], ssd_cb=ssd_conv_b[l].reshape(1, CONV_B_CH),
        ssd_dtb=pad_lanes(ssd_dt_bias[l]), ssd_alog=pad_lanes(ssd_a_log[l]),
        ssd_dvec=jnp.repeat(ssd_d[l], P_B).reshape(1, W_GROUP), ssd_nw=ssd_norm_w[l].reshape(1, W_GROUP),
        gla_w2p=w2p, gla_gb=gla_gk_b[l].reshape(1, H_C * DK_C), gla_nw=gla_norm_w[l].reshape(1, DV_C),
        lru_cw=lru_conv_w[l], lru_cb=lru_conv_b[l].reshape(1, W_D),
        lru_wa=_block_diag(lru_wa[l]).astype(BF16), lru_ba=lru_ba[l].reshape(1, W_D),
        lru_wx=_block_diag(lru_wx[l]).astype(BF16), lru_bx=lru_bx[l].reshape(1, W_D),
        lru_lam=lru_lambda[l].reshape(1, W_D),
    )


def _layer(x2d, B, L, rope_tab, p, states, past, layer, depth, kv_prev, final_w, final):
    ssd_s0, ssd_buf, gla_s0, lru_h0, lru_buf = states
    ydt = BF16 if L % 16 == 0 else F32
    T = B * L
    q, k, v, rest, small, kf, vf = _inproj(x2d, p["norm_w"], p["w_main"], p["w_small"], rope_tab,
                                           L, layer, depth, kv_prev)
    if past is None:
        ya = _attn_prompt(p["lam"], q, k, v, rest, p["subln_w"], B, L, p["lam_init"])
    else:
        cache_k, cache_v, page_table = past
        ya = _attn_decode(p["lam"], q, k, v, rest, p["subln_w"], cache_k, cache_v, page_table,
                          layer, B, L, p["lam_init"])
    yb, ssd_s, ssd_c = _ssd(rest, small, ssd_buf, ssd_s0, p["ssd_cw"], p["ssd_cb"], p["ssd_dtb"],
                            p["ssd_alog"], p["ssd_dvec"], p["ssd_nw"], B, L, ydt)
    yc, gla_st = _gla(rest, small, jnp.swapaxes(gla_s0, -1, -2), p["gla_w2p"], p["gla_gb"],
                      p["gla_nw"], B, L, ydt)
    yd, lru_h, lru_c = _lru(rest, lru_buf, lru_h0.reshape(B, 1, W_D), p["lru_cw"], p["lru_cb"],
                            p["lru_wa"], p["lru_ba"], p["lru_wx"], p["lru_bx"], p["lru_lam"], B, L, ydt)
    ys = (ya, yb.reshape(T, W_GROUP), yc.reshape(T, W_GROUP), yd.reshape(T, W_D))
    x_new = _outproj(ys, p["w_out4"], x2d, final_w, final)
    new_state = (ssd_s, ssd_c, jnp.swapaxes(gla_st, -1, -2), lru_h.reshape(B, W_D), lru_c)
    return x_new, new_state, (kf, vf)


def kernel(x_prompt, x_sample, cache_k, cache_v, page_table, state_ssd, state_ssd_conv, state_gla, state_lru, state_lru_conv, norm_w, w_in, w_out, lam_q1, lam_k1, lam_q2, lam_k2, subln_w, ssd_conv_w, ssd_conv_b, ssd_dt_bias, ssd_a_log, ssd_d, ssd_norm_w, gla_gk_w2, gla_gk_b, gla_norm_w, lru_conv_w, lru_conv_b, lru_wa, lru_ba, lru_wx, lru_bx, lru_lambda, final_norm_w):
    Bp, Lp, _ = x_prompt.shape
    Bs, Ls, _ = x_sample.shape
    depth = w_in.shape[0]
    past_len = page_table.shape[1] * cache_k.shape[2]
    tab_p = _rope_table(jnp.arange(Lp, dtype=jnp.int32), max(Lp, min(INPROJ_ROWS, Bp * Lp)))
    tab_s = _rope_table(past_len + jnp.arange(Ls, dtype=jnp.int32), max(Ls, min(INPROJ_ROWS, Bs * Ls)))
    final_w = final_norm_w.reshape(1, D_MODEL)
    hp = x_prompt.reshape(Bp * Lp, D_MODEL)
    hs = x_sample.reshape(Bs * Ls, D_MODEL)
    sp, ss = [], []
    kv_p = kv_s = None
    for l in range(depth):
        p = _prep_layer(l, norm_w, w_in, w_out, lam_q1, lam_k1, lam_q2, lam_k2, subln_w, ssd_conv_w,
                        ssd_conv_b, ssd_dt_bias, ssd_a_log, ssd_d, ssd_norm_w, gla_gk_w2, gla_gk_b,
                        gla_norm_w, lru_conv_w, lru_conv_b, lru_wa, lru_ba, lru_wx, lru_bx, lru_lambda)
        final = l == depth - 1
        st_p = (jnp.zeros((Bp, H_B, P_B, N_B), F32), jnp.zeros((Bp, CONV_W - 1, CONV_B_CH), F32),
                jnp.zeros((Bp, H_C, DK_C, DV_C), F32), jnp.zeros((Bp, W_D), F32),
                jnp.zeros((Bp, CONV_W - 1, W_D), F32))
        hp, new_p, kv_p = _layer(hp, Bp, Lp, tab_p, p, st_p, None, l, depth, kv_p, final_w, final)
        sp.append(new_p)
        st_s = (state_ssd[l], state_ssd_conv[l], state_gla[l], state_lru[l], state_lru_conv[l])
        hs, new_s, kv_s = _layer(hs, Bs, Ls, tab_s, p, st_s, (cache_k, cache_v, page_table), l, depth,
                                 kv_s, final_w, final)
        ss.append(new_s)
    outs_p = [x.reshape(depth, Bp, Lp, H_A, LANES) for x in kv_p]
    outs_p += [jnp.stack([s[i] for s in sp]) for i in range(5)]
    outs_s = [x.reshape(depth, Bs, Ls, H_A, LANES) for x in kv_s]
    outs_s += [jnp.stack([s[i] for s in ss]) for i in range(5)]
    return (hp.reshape(Bp, Lp, D_MODEL), hs.reshape(Bs, Ls, D_MODEL), *outs_p, *outs_s)
```

```python
import functools
import math

import jax
import jax.numpy as jnp
from jax import lax
from jax.experimental import pallas as pl
from jax.experimental.pallas import tpu as pltpu

F32 = jnp.float32
BF16 = jnp.bfloat16
EPS = 1e-6
NEG = -1e30
LOG2E = math.log2(math.e)

D_MODEL = 2048
W_GROUP = 512
H_A, DV_A, DK_A = 4, 128, 64
ROPE_DIM, ROPE_THETA = 16, 500000.0
H_B, P_B, N_B, G_B = 8, 64, 128, 2
CONV_W = 4
CONV_B_CH = W_GROUP + 2 * G_B * N_B
H_C, DK_C, DV_C = 4, 64, 128
GK_RANK, GK_NORM = 16, 16.0
W_D, NB_D, BW_D = 512, 8, 64
LRU_C = 8.0

LANES = 128
SUBLANES = 8
COL_BLK = 512
INPROJ_ROWS = 1024
ATTN_BQ = 512
ATTN_BK = 512
MIXER_SEQS = 4
VMEM_LIMIT = 56 * 1024 * 1024

_SPLITS = (512, 512, 512, 512, CONV_B_CH, H_B, W_GROUP, 256, 256, 512, GK_RANK, W_GROUP, W_D, W_D)
_OFF = [0]
for _s in _SPLITS:
    _OFF.append(_OFF[-1] + _s)
(_QA, _KA, _VA, _ZA, _XBC, _DT, _ZB, _QC, _KC, _VC, _GKLR, _GC, _XD, _ZD) = _OFF[:-1]

REST_COLS = 4608


def _cparams(sem):
    return pltpu.CompilerParams(dimension_semantics=sem, vmem_limit_bytes=VMEM_LIMIT)


def _sigmoid(x):
    return jax.nn.sigmoid(x)


def _silu(x):
    return x * _sigmoid(x)


def _softplus(x):
    return jnp.maximum(x, 0.0) + jnp.log1p(jnp.exp(-jnp.abs(x)))


def _rmsnorm_rows(x, w):
    return x * lax.rsqrt(jnp.mean(x * x, axis=-1, keepdims=True) + EPS) * w


def _dot(a, b):
    return jnp.dot(a, b, preferred_element_type=F32)


def _dot_nt(a, b):
    return lax.dot_general(a, b, (((1,), (1,)), ((), ())), preferred_element_type=F32)


def _dot_tn(a, b):
    return lax.dot_general(a, b, (((0,), (0,)), ((), ())), preferred_element_type=F32)


def _dot_exact(a, b):
    return jnp.dot(a, b, preferred_element_type=F32, precision=lax.Precision.HIGHEST)


def _lower_tri(n):
    r = lax.broadcasted_iota(jnp.int32, (n, n), 0)
    c = lax.broadcasted_iota(jnp.int32, (n, n), 1)
    return r >= c


def _batch_group(b, want):
    return math.gcd(b, want)


def _inproj_kernel(x_ref, nw_ref, w_ref, ws_ref, rope_ref,
                   q_ref, k_ref, v_ref, rest_ref, small_ref, kf_ref, vf_ref, u_scr):
    j = pl.program_id(1)
    tm = x_ref.shape[0]

    @pl.when(j == 0)
    def _():
        u = _rmsnorm_rows(x_ref[...], nw_ref[...]).astype(BF16)
        u_scr[...] = u
        small_ref[...] = _dot(u, ws_ref[...])

    acc = _dot(u_scr[...], w_ref[...])

    def rope_store(dst, scale, dst_heads=None):
        c, s1, s2 = rope_ref[0], rope_ref[1], rope_ref[2]
        for hh in range(COL_BLK // LANES):
            sl = slice(hh * LANES, (hh + 1) * LANES)
            a = acc[:, sl]
            r = (a * c + pltpu.roll(a, LANES - ROPE_DIM // 2, 1) * s1
                 + pltpu.roll(a, ROPE_DIM // 2, 1) * s2)
            dst[:, sl] = r * scale
            if dst_heads is not None:
                dst_heads[pl.ds(hh, tm, stride=H_A), :] = r

    @pl.when(j == 0)
    def _():
        rope_store(q_ref, DK_A ** -0.5 * LOG2E)

    @pl.when(j == 1)
    def _():
        rope_store(k_ref, 1.0, kf_ref)

    @pl.when(j == 2)
    def _():
        v_ref[...] = acc
        for hh in range(H_A):
            vf_ref[pl.ds(hh, tm, stride=H_A), :] = acc[:, hh * LANES:(hh + 1) * LANES]

    @pl.when(j >= 3)
    def _():
        rest_ref[...] = acc


def _inproj_kernel_aliased(x_ref, nw_ref, w_ref, ws_ref, rope_ref, kprev_ref, vprev_ref, *refs):
    del kprev_ref, vprev_ref
    _inproj_kernel(x_ref, nw_ref, w_ref, ws_ref, rope_ref, *refs)


def _inproj(x2d, norm_w, w_main, w_small, rope_tab, seq_len, layer, depth, kv_prev):
    T = x2d.shape[0]
    tm = min(INPROJ_ROWS, T)
    assert T % tm == 0 and (seq_len % tm == 0 or tm % seq_len == 0)
    n_pos_blk = max(seq_len // tm, 1)
    nj = w_main.shape[0]
    kv_shape = jax.ShapeDtypeStruct((depth, T * H_A, LANES), F32)
    out_shape = (
        jax.ShapeDtypeStruct((T, COL_BLK), F32),
        jax.ShapeDtypeStruct((T, COL_BLK), F32),
        jax.ShapeDtypeStruct((T, COL_BLK), F32),
        jax.ShapeDtypeStruct((T, REST_COLS), F32),
        jax.ShapeDtypeStruct((T, LANES), F32),
        kv_shape, kv_shape,
    )
    row_blk = pl.BlockSpec((tm, COL_BLK), lambda i, j: (i, 0))
    kv_blk = pl.BlockSpec((None, tm * H_A, LANES), lambda i, j: (layer, i, 0))
    in_specs = [
        pl.BlockSpec((tm, D_MODEL), lambda i, j: (i, 0)),
        pl.BlockSpec((1, D_MODEL), lambda i, j: (0, 0)),
        pl.BlockSpec((None, D_MODEL, COL_BLK), lambda i, j: (j, 0, 0)),
        pl.BlockSpec((D_MODEL, LANES), lambda i, j: (0, 0)),
        pl.BlockSpec((3, tm, LANES), lambda i, j: (0, i % n_pos_blk, 0)),
    ]
    args = (x2d, norm_w, w_main, w_small, rope_tab)
    aliases = {}
    kern = _inproj_kernel
    if kv_prev is not None:
        in_specs += [pl.BlockSpec(memory_space=pl.ANY)] * 2
        aliases = {len(args): 5, len(args) + 1: 6}
        args += tuple(kv_prev)
        kern = _inproj_kernel_aliased
    return pl.pallas_call(
        kern,
        grid=(T // tm, nj),
        in_specs=in_specs,
        out_specs=(
            row_blk, row_blk, row_blk,
            pl.BlockSpec((tm, COL_BLK), lambda i, j: (i, jnp.maximum(j - 3, 0))),
            pl.BlockSpec((tm, LANES), lambda i, j: (i, 0)),
            kv_blk, kv_blk,
        ),
        out_shape=out_shape,
        input_output_aliases=aliases,
        scratch_shapes=[pltpu.VMEM((tm, D_MODEL), BF16)],
        compiler_params=_cparams(("arbitrary", "arbitrary")),
        name="inproj",
    )(*args)


def _attn_finish(o, lam_init, w, z):
    o = _rmsnorm_rows(o, w) * (1.0 - lam_init)
    return o * _silu(z)


def _attn_kernel(lam_ref, q_ref, k_ref, v_ref, z_ref, w_ref, o_ref, kb_scr, vt_scr,
                 m_scr, l_scr, acc_scr, *, bq, bk, lam_init):
    qi = pl.program_id(2)
    n_kb = kb_scr.shape[0]
    diag_blocks = bq // bk

    @pl.when(qi == 0)
    def _():
        for c in range(n_kb):
            rows = slice(c * bk, (c + 1) * bk)
            kb_scr[c] = k_ref[rows, :].astype(BF16)
            vt_scr[c] = v_ref[rows, :].T.astype(BF16)

    qt = q_ref[...].T
    feat = lax.broadcasted_iota(jnp.int32, qt.shape, 0)
    qts = (jnp.where(feat < DK_A, qt, 0.0).astype(BF16),
           jnp.where(feat >= DK_A, qt, 0.0).astype(BF16))
    m_scr[...] = jnp.full(m_scr.shape, NEG, F32)
    l_scr[...] = jnp.zeros(l_scr.shape, F32)
    acc_scr[...] = jnp.zeros(acc_scr.shape, F32)

    def step(j, mask):
        kb = kb_scr[j]
        vtb = vt_scr[j]
        ss = [_dot(kb, qts[m]) for m in range(2)]
        for m in range(2):
            s = ss[m]
            if mask is not None:
                s = jnp.where(mask, s, NEG)
            mx = m_scr[m]
            mn = jnp.maximum(mx, jnp.max(s, axis=0, keepdims=True))
            alpha = jnp.exp2(mx - mn)
            p = jnp.exp2(s - mn)
            l_scr[m] = alpha * l_scr[m] + jnp.sum(p, axis=0, keepdims=True)
            acc_scr[m] = alpha * acc_scr[m] + _dot(vtb, p.astype(BF16))
            m_scr[m] = mn

    def unmasked(j, carry):
        step(j, None)
        return carry

    lax.fori_loop(0, qi * diag_blocks, unmasked, 0)
    key_i = lax.broadcasted_iota(jnp.int32, (bk, bq), 0)
    qry_i = lax.broadcasted_iota(jnp.int32, (bk, bq), 1)
    for d in range(diag_blocks):
        step(qi * diag_blocks + d, key_i + d * bk <= qry_i)
    lam = lam_ref[0, 0]
    o = (acc_scr[0] / l_scr[0] - lam * (acc_scr[1] / l_scr[1])).T
    o_ref[...] = _attn_finish(o, lam_init, w_ref[...], z_ref[...]).astype(o_ref.dtype)


def _attn_prompt(lam, q, k, v, rest, subln_w, B, L, lam_init):
    bq = min(ATTN_BQ, L)
    bk = min(ATTN_BK, bq)
    nq = L // bq
    kern = functools.partial(_attn_kernel, bq=bq, bk=bk, lam_init=lam_init)
    return pl.pallas_call(
        kern,
        grid=(B, H_A, nq),
        in_specs=[
            pl.BlockSpec(memory_space=pltpu.SMEM),
            pl.BlockSpec((bq, LANES), lambda b, h, i: (b * nq + i, h)),
            pl.BlockSpec((L, LANES), lambda b, h, i: (b, h)),
            pl.BlockSpec((L, LANES), lambda b, h, i: (b, h)),
            pl.BlockSpec((bq, LANES), lambda b, h, i: (b * nq + i, 8 + h)),
            pl.BlockSpec((1, LANES), lambda b, h, i: (0, 0)),
        ],
        out_specs=pl.BlockSpec((bq, LANES), lambda b, h, i: (b * nq + i, h)),
        out_shape=jax.ShapeDtypeStruct((B * L, W_GROUP), BF16),
        scratch_shapes=[
            pltpu.VMEM((L // bk, bk, LANES), BF16),
            pltpu.VMEM((L // bk, DV_A, bk), BF16),
            pltpu.VMEM((2, 1, bq), F32),
            pltpu.VMEM((2, 1, bq), F32),
            pltpu.VMEM((2, DV_A, bq), F32),
        ],
        compiler_params=_cparams(("arbitrary", "arbitrary", "arbitrary")),
        name="attn_prompt",
    )(lam, q, k, v, rest, subln_w)


def _decode_init(q_ref, qrow_scr, m_scr, l_scr, acc_scr, ls):
    rows = 2 * ls
    r = lax.broadcasted_iota(jnp.int32, (rows, LANES), 0)
    c = lax.broadcasted_iota(jnp.int32, (rows, LANES), 1)
    own_map = (r // ls) == (c // DK_A)
    for h in range(H_A):
        qh = q_ref[:, h * LANES:(h + 1) * LANES]
        qrow_scr[h] = jnp.where(own_map, jnp.concatenate([qh, qh], axis=0), 0.0).astype(BF16)
    m_scr[...] = jnp.full(m_scr.shape, NEG, F32)
    l_scr[...] = jnp.zeros(l_scr.shape, F32)
    acc_scr[...] = jnp.zeros(acc_scr.shape, F32)


def _decode_head_rows(pages, h, page):
    head_rows = pl.ds(h, page, stride=H_A)
    return jnp.concatenate([pg[head_rows, :] for pg in pages], axis=0).astype(BF16)


def _decode_accumulate(h, s, vv, m_scr, l_scr, acc_scr):
    mx = m_scr[h]
    mn = jnp.maximum(mx, jnp.max(s, axis=-1, keepdims=True))
    alpha = jnp.exp2(mx - mn)
    pr = jnp.exp2(s - mn)
    l_scr[h] = alpha * l_scr[h] + jnp.sum(pr, axis=-1, keepdims=True)
    acc_scr[h] = alpha * acc_scr[h] + _dot(pr.astype(BF16), vv)
    m_scr[h] = mn


def _decode_finish(lam_ref, kn_ref, vn_ref, z_ref, w_ref, o_ref, qrow_scr, m_scr, l_scr, acc_scr,
                   ls, lam_init):
    rows = 2 * ls
    lam = lam_ref[0, 0]
    w = w_ref[...]
    jj = lax.broadcasted_iota(jnp.int32, (rows, ls), 1)
    qq = lax.broadcasted_iota(jnp.int32, (rows, ls), 0) % ls
    for h in range(H_A):
        cs = slice(h * LANES, (h + 1) * LANES)
        s = _dot_nt(qrow_scr[h], kn_ref[:, cs].astype(BF16))
        _decode_accumulate(h, jnp.where(jj <= qq, s, NEG), vn_ref[:, cs].astype(BF16),
                           m_scr, l_scr, acc_scr)
        o = acc_scr[h] / l_scr[h]
        o = o[:ls] - lam * o[ls:]
        o_ref[:, cs] = _attn_finish(o, lam_init, w, z_ref[:, cs]).astype(o_ref.dtype)


CONV_PAD = SUBLANES
CONV_TAIL = CONV_PAD - (CONV_W - 1)


def _conv_step(x_ref, cw_ref, cb_ref, xp_scr, q):
    xp_scr[CONV_PAD:CONV_PAD + q, :] = x_ref[...]
    y = xp_scr[CONV_TAIL:CONV_TAIL + q, :] * cw_ref[0:1, :]
    for t in range(1, CONV_W):
        y = y + xp_scr[CONV_TAIL + t:CONV_TAIL + t + q, :] * cw_ref[t:t + 1, :]
    y = y + cb_ref[...]
    xp_scr[CONV_TAIL:CONV_PAD, :] = xp_scr[q + CONV_TAIL:q + CONV_PAD, :]
    return y


def _ssd_inputs(xbc_ref, dt_ref, cw_ref, cb_ref, dtb_ref, xp_scr, q, nb):
    xbc = [_silu(_conv_step(xbc_ref.at[b], cw_ref, cb_ref, xp_scr.at[b], q)) for b in range(nb)]
    dt = [_softplus(dt_ref[b] + dtb_ref[...]) for b in range(nb)]
    return xbc, dt


def _ssd_chunk(xbc_all, dt_all, zb_ref, alog_ref, dvec_ref, nw_ref, y_ref, s_scr, y_scr, q, nb):
    tri = _lower_tri(q)
    tri_f = tri.astype(F32)
    a = -jnp.exp(alog_ref[...])
    heads_per_group = H_B // G_B
    for b in range(nb):
        xbc, dt = xbc_all[b], dt_all[b]
        xs = xbc[:, :W_GROUP]
        cum = _dot_exact(tri_f, dt * a)
        cum_t = cum.T
        dt_t = dt.T
        last = cum[q - 1:q, :]
        ecum = jnp.exp(cum)
        wj = jnp.exp(last - cum) * dt
        elast = jnp.exp(last)
        for g in range(G_B):
            bg = xbc[:, W_GROUP + g * N_B:W_GROUP + (g + 1) * N_B]
            cg = xbc[:, W_GROUP + (G_B + g) * N_B:W_GROUP + (G_B + g + 1) * N_B]
            bgb = bg.astype(BF16)
            cb_g = _dot_nt(cg.astype(BF16), bgb)
            for hh in range(heads_per_group):
                h = g * heads_per_group + hh
                hs = slice(h * P_B, (h + 1) * P_B)
                seg = cum[:, h:h + 1] - cum_t[h:h + 1, :]
                dec = jnp.exp(jnp.where(tri, seg, NEG))
                mh = (cb_g * dec * dt_t[h:h + 1, :]).astype(BF16)
                xh = xs[:, hs]
                s_old = s_scr[b, h]
                yh = _dot(mh, xh.astype(BF16)) + _dot_nt((cg * ecum[:, h:h + 1]).astype(BF16),
                                                         s_old.astype(BF16))
                y_scr[b, :, hs] = yh
                xw = (xh * wj[:, h:h + 1]).astype(BF16)
                s_scr[b, h] = elast[:, h:h + 1] * s_old + _dot_tn(xw, bgb)
        y = y_scr[b] + dvec_ref[...] * xs
        y_ref[b] = _rmsnorm_rows(y * _silu(zb_ref[b]), nw_ref[...]).astype(y_ref.dtype)


def _ssd_kernel(xbc_ref, dt_ref, zb_ref, buf_ref, s0_ref, cw_ref, cb_ref, dtb_ref, alog_ref,
                dvec_ref, nw_ref, y_ref, sout_ref, cout_ref, xp_scr, s_scr, y_scr, *, q, nb):
    c = pl.program_id(1)

    @pl.when(c == 0)
    def _():
        s_scr[...] = s0_ref[...]
        xp_scr[:, CONV_TAIL:CONV_PAD, :] = buf_ref[...]

    xbc, dt = _ssd_inputs(xbc_ref, dt_ref, cw_ref, cb_ref, dtb_ref, xp_scr, q, nb)
    _ssd_chunk(xbc, dt, zb_ref, alog_ref, dvec_ref, nw_ref, y_ref, s_scr, y_scr, q, nb)

    @pl.when(c == pl.num_programs(1) - 1)
    def _():
        sout_ref[...] = s_scr[...]
        cout_ref[...] = xp_scr[:, CONV_TAIL:CONV_PAD, :]


def _ssd_decode_kernel(pt_ref, lam_ref, xbc_ref, dt_ref, zb_ref, buf_ref, s0_ref, cw_ref, cb_ref,
                       dtb_ref, alog_ref, dvec_ref, nw_ref, q_ref, kn_ref, vn_ref, z_ref, w_ref,
                       *refs, q, nb, n_pp, steps_per_seq, ls, page, lam_init):
    del pt_ref
    kp = refs[:n_pp]
    vp = refs[n_pp:2 * n_pp]
    y_ref, sout_ref, cout_ref, o_ref = refs[2 * n_pp:2 * n_pp + 4]
    xp_scr, s_scr, y_scr, qrow_scr, m_scr, l_scr, acc_scr = refs[2 * n_pp + 4:]
    c = pl.program_id(1)
    part = (pl.program_id(0) * pl.num_programs(1) + c) % steps_per_seq

    @pl.when(c == 0)
    def _():
        s_scr[...] = s0_ref[...]
        xp_scr[:, CONV_TAIL:CONV_PAD, :] = buf_ref[...]

    @pl.when(part == 0)
    def _():
        _decode_init(q_ref, qrow_scr, m_scr, l_scr, acc_scr, ls)

    scores = [_dot_nt(qrow_scr[h], _decode_head_rows(kp, h, page)) for h in range(H_A)]
    xbc, dt = _ssd_inputs(xbc_ref, dt_ref, cw_ref, cb_ref, dtb_ref, xp_scr, q, nb)
    for h in range(H_A):
        _decode_accumulate(h, scores[h], _decode_head_rows(vp, h, page), m_scr, l_scr, acc_scr)
    _ssd_chunk(xbc, dt, zb_ref, alog_ref, dvec_ref, nw_ref, y_ref, s_scr, y_scr, q, nb)

    @pl.when(c == pl.num_programs(1) - 1)
    def _():
        sout_ref[...] = s_scr[...]
        cout_ref[...] = xp_scr[:, CONV_TAIL:CONV_PAD, :]

    @pl.when(part == steps_per_seq - 1)
    def _():
        _decode_finish(lam_ref, kn_ref, vn_ref, z_ref, w_ref, o_ref, qrow_scr, m_scr, l_scr,
                       acc_scr, ls, lam_init)


def _ssd_specs(B, L, nb, q, index):
    full = lambda shape: pl.BlockSpec(shape, index(lambda b, c: (0,) * len(shape)))
    in_specs = [
        pl.BlockSpec((nb, q, CONV_B_CH), index(lambda b, c: (b, c, 0))),
        pl.BlockSpec((nb, q, LANES), index(lambda b, c: (b, c, 0))),
        pl.BlockSpec((nb, q, W_GROUP), index(lambda b, c: (b, c, 3))),
        pl.BlockSpec((nb, CONV_W - 1, CONV_B_CH), index(lambda b, c: (b, 0, 0))),
        pl.BlockSpec((nb, H_B, P_B, N_B), index(lambda b, c: (b, 0, 0, 0))),
        full((CONV_W, CONV_B_CH)), full((1, CONV_B_CH)), full((1, LANES)), full((1, LANES)),
        full((1, W_GROUP)), full((1, W_GROUP)),
    ]
    out_specs = [
        pl.BlockSpec((nb, q, W_GROUP), index(lambda b, c: (b, c, 0))),
        pl.BlockSpec((nb, H_B, P_B, N_B), index(lambda b, c: (b, 0, 0, 0))),
        pl.BlockSpec((nb, CONV_W - 1, CONV_B_CH), index(lambda b, c: (b, 0, 0))),
    ]
    scratch = [
        pltpu.VMEM((nb, q + SUBLANES, CONV_B_CH), F32),
        pltpu.VMEM((nb, H_B, P_B, N_B), F32),
        pltpu.VMEM((nb, q, W_GROUP), F32),
    ]
    return in_specs, out_specs, scratch


def _ssd_out_shape(B, L, out_dtype):
    return [
        jax.ShapeDtypeStruct((B, L, W_GROUP), out_dtype),
        jax.ShapeDtypeStruct((B, H_B, P_B, N_B), F32),
        jax.ShapeDtypeStruct((B, CONV_W - 1, CONV_B_CH), F32),
    ]


def _ssd(rest, small, buf, s0, cw, cb, dtb, alog, dvec, nw, B, L, out_dtype):
    q = math.gcd(L, 64)
    nc = L // q
    nb = _batch_group(B, MIXER_SEQS)
    in_specs, out_specs, scratch = _ssd_specs(B, L, nb, q, lambda f: f)
    rest3 = rest.reshape(B, L, REST_COLS)
    return pl.pallas_call(
        functools.partial(_ssd_kernel, q=q, nb=nb),
        grid=(B // nb, nc),
        in_specs=in_specs,
        out_specs=tuple(out_specs),
        out_shape=tuple(_ssd_out_shape(B, L, out_dtype)),
        scratch_shapes=scratch,
        compiler_params=_cparams(("arbitrary", "arbitrary")),
        name="ssd",
    )(rest3, small.reshape(B, L, LANES), rest3, buf, s0, cw, cb, dtb, alog, dvec, nw)


def _ssd_decode(ssd_args, B, L, out_dtype, lam, q_s, k_s, v_s, rest_s, subln_w, cache_k, cache_v,
                page_table, layer, Bs, Ls, lam_init):
    rest, small, buf, s0, cw, cb, dtb, alog, dvec, nw = ssd_args
    q = math.gcd(L, 64)
    nc = L // q
    nb = _batch_group(B, MIXER_SEQS)
    steps = (B // nb) * nc
    n_pages = page_table.shape[1]
    page = cache_k.shape[2]
    assert (Bs * n_pages) % steps == 0, "decode pages must spread evenly over the SSD grid"
    n_pp = Bs * n_pages // steps
    assert n_pages % n_pp == 0
    steps_per_seq = n_pages // n_pp
    rows = 2 * Ls
    ck = cache_k.reshape(cache_k.shape[0], cache_k.shape[1], page * H_A, 2 * DK_A)
    cv = cache_v.reshape(cache_v.shape[0], cache_v.shape[1], page * H_A, DV_A)

    def step_of(b, c):
        return b * nc + c

    def page_spec(i):
        def index(b, c, pt):
            t = step_of(b, c)
            return (layer, pt[t // steps_per_seq, (t % steps_per_seq) * n_pp + i], 0, 0)
        return pl.BlockSpec((None, None, page * H_A, LANES), index)

    with_pt = lambda f: (lambda b, c, pt: f(b, c))
    ssd_in, ssd_out, ssd_scratch = _ssd_specs(B, L, nb, q, with_pt)
    tok = pl.BlockSpec((Ls, W_GROUP), lambda b, c, pt: (step_of(b, c) // steps_per_seq, 0))
    kern = functools.partial(_ssd_decode_kernel, q=q, nb=nb, n_pp=n_pp, steps_per_seq=steps_per_seq,
                             ls=Ls, page=page, lam_init=lam_init)
    grid_spec = pltpu.PrefetchScalarGridSpec(
        num_scalar_prefetch=1,
        grid=(B // nb, nc),
        in_specs=[pl.BlockSpec(memory_space=pltpu.SMEM)] + ssd_in + [
            tok, tok, tok,
            pl.BlockSpec((Ls, W_GROUP), lambda b, c, pt: (step_of(b, c) // steps_per_seq, 2)),
            pl.BlockSpec((1, LANES), lambda b, c, pt: (0, 0)),
        ] + [page_spec(i) for i in range(n_pp)] * 2,
        out_specs=tuple(ssd_out + [tok]),
        scratch_shapes=ssd_scratch + [
            pltpu.VMEM((H_A, rows, LANES), BF16),
            pltpu.VMEM((H_A, rows, 1), F32),
            pltpu.VMEM((H_A, rows, 1), F32),
            pltpu.VMEM((H_A, rows, DV_A), F32),
        ],
    )
    rest3 = rest.reshape(B, L, REST_COLS)
    return pl.pallas_call(
        kern,
        grid_spec=grid_spec,
        out_shape=tuple(_ssd_out_shape(B, L, out_dtype)
                        + [jax.ShapeDtypeStruct((Bs * Ls, W_GROUP), F32)]),
        compiler_params=_cparams(("arbitrary", "arbitrary")),
        name="ssd_decode",
    )(page_table, lam, rest3, small.reshape(B, L, LANES), rest3, buf, s0, cw, cb, dtb, alog, dvec, nw,
      q_s, k_s, v_s, rest_s, subln_w, *([ck] * n_pp), *([cv] * n_pp))


def _gla_kernel(qk_ref, v_ref, gc_ref, sm_ref, s0_ref, w2_ref, gb_ref, nw_ref,
                y_ref, sout_ref, st_scr, *, q, nb):
    c = pl.program_id(1)

    @pl.when(c == 0)
    def _():
        st_scr[...] = s0_ref[...]

    seqs = range(nb)
    heads = range(H_C)
    tri = _lower_tri(q)
    width = H_C * DK_C
    nw = nw_ref[...]
    sm = sm_ref[...].reshape(nb * q, LANES).astype(BF16)
    gkl = _dot(sm, w2_ref[...]) + gb_ref[...]
    g = (jnp.minimum(gkl, 0.0) - jnp.log1p(jnp.exp(-jnp.abs(gkl)))) * (1.0 / GK_NORM)
    bc_all = _dot_exact(tri.astype(F32), jnp.concatenate([g[b * q:(b + 1) * q] for b in seqs], axis=1))
    bc = [bc_all[:, b * width:(b + 1) * width] for b in seqs]
    last = [x[q - 1:q, :] for x in bc]
    kc = [qk_ref[b, :, width:] for b in seqs]
    qe = [(qk_ref[b, :, :width] * (DK_C ** -0.5) * jnp.exp(bc[b])).astype(BF16) for b in seqs]
    ke = [(kc[b] * jnp.exp(-bc[b])).astype(BF16) for b in seqs]
    kl = [(kc[b] * jnp.exp(last[b] - bc[b])).astype(BF16) for b in seqs]
    el = [jnp.exp(x) for x in last]
    ks = [slice(h * DK_C, (h + 1) * DK_C) for h in heads]
    vs = [slice(h * DV_C, (h + 1) * DV_C) for h in heads]
    vh = [[v_ref[b, :, vs[h]].astype(BF16) for h in heads] for b in seqs]
    st_old = [[st_scr[b, h] for h in heads] for b in seqs]
    att = [[_dot_nt(qe[b][:, ks[h]], ke[b][:, ks[h]]) for h in heads] for b in seqs]
    o_state = [[_dot_nt(qe[b][:, ks[h]], st_old[b][h].astype(BF16)) for h in heads] for b in seqs]
    upd = [[_dot_tn(vh[b][h], kl[b][:, ks[h]]) for h in heads] for b in seqs]
    o = [[_dot(jnp.where(tri, att[b][h], 0.0).astype(BF16), vh[b][h]) + o_state[b][h]
          for h in heads] for b in seqs]
    for b in seqs:
        for h in heads:
            st_scr[b, h] = el[b][:, ks[h]] * st_old[b][h] + upd[b][h]
            y_ref[b, :, vs[h]] = (_rmsnorm_rows(o[b][h], nw)
                                  * _silu(gc_ref[b, :, vs[h]])).astype(y_ref.dtype)

    @pl.when(c == pl.num_programs(1) - 1)
    def _():
        sout_ref[...] = st_scr[...]


def _gla(rest, small, s0_t, w2p, gb, nw, B, L, out_dtype):
    q = min(L, 64)
    nc = L // q
    nb = _batch_group(B, MIXER_SEQS)
    kern = functools.partial(_gla_kernel, q=q, nb=nb)
    full = lambda shape: pl.BlockSpec(shape, lambda b, c: (0,) * len(shape))
    state = pl.BlockSpec((nb, H_C, DV_C, DK_C), lambda b, c: (b, 0, 0, 0))
    rest3 = rest.reshape(B, L, REST_COLS)
    return pl.pallas_call(
        kern,
        grid=(B // nb, nc),
        in_specs=[
            pl.BlockSpec((nb, q, W_GROUP), lambda b, c: (b, c, 4)),
            pl.BlockSpec((nb, q, W_GROUP), lambda b, c: (b, c, 5)),
            pl.BlockSpec((nb, q, W_GROUP), lambda b, c: (b, c, 6)),
            pl.BlockSpec((nb, q, LANES), lambda b, c: (b, c, 0)),
            state,
            full((LANES, H_C * DK_C)), full((1, H_C * DK_C)), full((1, DV_C)),
        ],
        out_specs=(pl.BlockSpec((nb, q, W_GROUP), lambda b, c: (b, c, 0)), state),
        out_shape=(
            jax.ShapeDtypeStruct((B, L, W_GROUP), out_dtype),
            jax.ShapeDtypeStruct((B, H_C, DV_C, DK_C), F32),
        ),
        scratch_shapes=[pltpu.VMEM((nb, H_C, DV_C, DK_C), F32)],
        compiler_params=_cparams(("arbitrary", "arbitrary")),
        name="gla",
    )(rest3, rest3, rest3, small.reshape(B, L, LANES), s0_t, w2p, gb, nw)


def _lru_kernel(xd_ref, zd_ref, buf_ref, h0_ref, cw_ref, cb_ref, wa_ref, ba_ref, wx_ref, bx_ref,
                lam_ref, y_ref, hout_ref, cout_ref, xp_scr, h_scr, a_scr, u_scr, *, q, nb):
    c = pl.program_id(1)

    @pl.when(c == 0)
    def _():
        h_scr[...] = h0_ref[...]
        xp_scr[:, CONV_TAIL:CONV_PAD, :] = buf_ref[...]

    sp_lam = _softplus(-lam_ref[...])
    for b in range(nb):
        xr = _conv_step(xd_ref.at[b], cw_ref, cb_ref, xp_scr.at[b], q)
        xb = xr.astype(BF16)
        r = _sigmoid(_dot(xb, wa_ref[...]) + ba_ref[...])
        i = _sigmoid(_dot(xb, wx_ref[...]) + bx_ref[...])
        log_a = -LRU_C * r * sp_lam
        a = jnp.exp(log_a)
        th = jnp.tanh(log_a)
        u = jnp.sqrt(-2.0 * th / (1.0 - th)) * (i * xr)
        a = a.reshape(q // SUBLANES, SUBLANES, W_D)
        u = u.reshape(q // SUBLANES, SUBLANES, W_D)
        r8 = lax.broadcasted_iota(jnp.int32, a.shape, 1)
        for s in (1, 2, 4):
            keep = r8 >= s
            a_sh = pltpu.roll(a, s, 1)
            u_sh = pltpu.roll(u, s, 1)
            u = jnp.where(keep, a * u_sh + u, u)
            a = jnp.where(keep, a * a_sh, a)
        a_scr[b] = a.reshape(q, W_D)
        u_scr[b] = u.reshape(q, W_D)
    carry = [h_scr[b] for b in range(nb)]
    for gi in range(q // SUBLANES):
        rows = slice(gi * SUBLANES, (gi + 1) * SUBLANES)
        for b in range(nb):
            hg = a_scr[b, rows, :] * carry[b] + u_scr[b, rows, :]
            y_ref[b, rows, :] = (hg * _silu(zd_ref[b, rows, :])).astype(y_ref.dtype)
            carry[b] = hg[SUBLANES - 1:SUBLANES, :]
    for b in range(nb):
        h_scr[b] = carry[b]

    @pl.when(c == pl.num_programs(1) - 1)
    def _():
        hout_ref[...] = h_scr[...]
        cout_ref[...] = xp_scr[:, CONV_TAIL:CONV_PAD, :]


def _lru(rest, buf, h0, cw, cb, wa, ba, wx, bx, lam, B, L, out_dtype):
    q = min(L, 256)
    nc = L // q
    nb = _batch_group(B, MIXER_SEQS)
    kern = functools.partial(_lru_kernel, q=q, nb=nb)
    full = lambda shape: pl.BlockSpec(shape, lambda b, c: (0,) * len(shape))
    rest3 = rest.reshape(B, L, REST_COLS)
    return pl.pallas_call(
        kern,
        grid=(B // nb, nc),
        in_specs=[
            pl.BlockSpec((nb, q, W_D), lambda b, c: (b, c, 7)),
            pl.BlockSpec((nb, q, W_D), lambda b, c: (b, c, 8)),
            pl.BlockSpec((nb, CONV_W - 1, W_D), lambda b, c: (b, 0, 0)),
            pl.BlockSpec((nb, 1, W_D), lambda b, c: (b, 0, 0)),
            full((CONV_W, W_D)), full((1, W_D)),
            full((W_D, W_D)), full((1, W_D)), full((W_D, W_D)), full((1, W_D)), full((1, W_D)),
        ],
        out_specs=(
            pl.BlockSpec((nb, q, W_D), lambda b, c: (b, c, 0)),
            pl.BlockSpec((nb, 1, W_D), lambda b, c: (b, 0, 0)),
            pl.BlockSpec((nb, CONV_W - 1, W_D), lambda b, c: (b, 0, 0)),
        ),
        out_shape=(
            jax.ShapeDtypeStruct((B, L, W_D), out_dtype),
            jax.ShapeDtypeStruct((B, 1, W_D), F32),
            jax.ShapeDtypeStruct((B, CONV_W - 1, W_D), F32),
        ),
        scratch_shapes=[
            pltpu.VMEM((nb, q + SUBLANES, W_D), F32),
            pltpu.VMEM((nb, 1, W_D), F32),
            pltpu.VMEM((nb, q, W_D), F32),
            pltpu.VMEM((nb, q, W_D), F32),
        ],
        compiler_params=_cparams(("arbitrary", "arbitrary")),
        name="lru",
    )(rest3, rest3, buf, h0, cw, cb, wa, ba, wx, bx, lam)


def _outproj_kernel(ya_ref, yb_ref, yc_ref, yd_ref, w_ref, x_ref, fw_ref, o_ref, *, final):
    out = _dot(ya_ref[...].astype(BF16), w_ref[0])
    for g, y_ref in enumerate((yb_ref, yc_ref, yd_ref), start=1):
        out = out + _dot(y_ref[...].astype(BF16), w_ref[g])
    res = x_ref[...] + out
    if final:
        res = _rmsnorm_rows(res, fw_ref[...])
    o_ref[...] = res


def _outproj(ys, w_out4, x2d, final_w, final):
    T = x2d.shape[0]
    tm = min(512, T)
    kern = functools.partial(_outproj_kernel, final=final)
    yspec = pl.BlockSpec((tm, W_GROUP), lambda i: (i, 0))
    return pl.pallas_call(
        kern,
        grid=(T // tm,),
        in_specs=[yspec] * 4 + [
            pl.BlockSpec((4, W_GROUP, D_MODEL), lambda i: (0, 0, 0)),
            pl.BlockSpec((tm, D_MODEL), lambda i: (i, 0)),
            pl.BlockSpec((1, D_MODEL), lambda i: (0, 0)),
        ],
        out_specs=pl.BlockSpec((tm, D_MODEL), lambda i: (i, 0)),
        out_shape=jax.ShapeDtypeStruct((T, D_MODEL), F32),
        compiler_params=_cparams(("arbitrary",)),
        name="outproj",
    )(*ys, w_out4, x2d, final_w)


def _rope_table(pos, rows):
    half = ROPE_DIM // 2
    inv = ROPE_THETA ** (-jnp.arange(half, dtype=F32) / half)
    ang = pos.astype(F32)[:, None] * inv[None, :]
    cos, sin = jnp.cos(ang), jnp.sin(ang)
    n = pos.shape[0]
    pad = jnp.zeros((n, DK_A - ROPE_DIM), F32)
    c64 = jnp.concatenate([cos, cos, pad + 1.0], axis=1)
    s1 = jnp.concatenate([-sin, jnp.zeros_like(sin), pad], axis=1)
    s2 = jnp.concatenate([jnp.zeros_like(sin), sin, pad], axis=1)
    tab = jnp.stack([jnp.tile(t, (1, LANES // DK_A)) for t in (c64, s1, s2)])
    return jnp.tile(tab, (1, rows // n, 1))


def _block_diag(w):
    nb, bw, _ = w.shape
    eye = jnp.eye(nb, dtype=w.dtype)
    return (eye[:, None, :, None] * w[:, :, None, :]).reshape(nb * bw, nb * bw)


def _prep_layer(l, norm_w, w_in, w_out, lam_q1, lam_k1, lam_q2, lam_k2, subln_w, ssd_conv_w,
                ssd_conv_b, ssd_dt_bias, ssd_a_log, ssd_d, ssd_norm_w, gla_gk_w2, gla_gk_b,
                gla_norm_w, lru_conv_w, lru_conv_b, lru_wa, lru_ba, lru_wx, lru_bx, lru_lambda):
    w = w_in[l]
    cols = lambda a, n: w[:, a:a + n]
    w_main = jnp.stack([
        cols(_QA, 512), cols(_KA, 512), cols(_VA, 512), cols(_XBC, 512), cols(_XBC + 512, 512),
        cols(_ZA, 512), cols(_ZB, 512), jnp.concatenate([cols(_QC, 256), cols(_KC, 256)], axis=1),
        cols(_VC, 512), cols(_GC, 512), cols(_XD, 512), cols(_ZD, 512)]).astype(BF16)
    w_small = jnp.concatenate([cols(_DT, H_B), cols(_GKLR, GK_RANK),
                               jnp.zeros((D_MODEL, LANES - H_B - GK_RANK), F32)], axis=1).astype(BF16)
    lam_init = 0.8 - 0.6 * math.exp(-0.3 * l)
    lam = (jnp.exp(jnp.sum(lam_q1[l] * lam_k1[l])) - jnp.exp(jnp.sum(lam_q2[l] * lam_k2[l]))
           + lam_init).reshape(1, 1).astype(F32)
    pad_lanes = lambda v: jnp.concatenate([v, jnp.zeros((LANES - v.shape[0],), F32)]).reshape(1, LANES)
    w2p = jnp.zeros((LANES, H_C * DK_C), F32).at[H_B:H_B + GK_RANK].set(gla_gk_w2[l]).astype(BF16)
    return dict(
        norm_w=norm_w[l].reshape(1, D_MODEL), w_main=w_main, w_small=w_small,
        w_out4=w_out[l].reshape(4, W_GROUP, D_MODEL).astype(BF16),
        lam=lam, lam_init=lam_init, subln_w=subln_w[l].reshape(1, DV_A),
        ssd_cw=ssd_conv_w[l], ssd_cb=ssd_conv_b[l].reshape(1, CONV_B_CH),
        ssd_dtb=pad_lanes(ssd_dt_bias[l]), ssd_alog=pad_lanes(ssd_a_log[l]),
        ssd_dvec=jnp.repeat(ssd_d[l], P_B).reshape(1, W_GROUP), ssd_nw=ssd_norm_w[l].reshape(1, W_GROUP),
        gla_w2p=w2p, gla_gb=gla_gk_b[l].reshape(1, H_C * DK_C), gla_nw=gla_norm_w[l].reshape(1, DV_C),
        lru_cw=lru_conv_w[l], lru_cb=lru_conv_b[l].reshape(1, W_D),
        lru_wa=_block_diag(lru_wa[l]).astype(BF16), lru_ba=lru_ba[l].reshape(1, W_D),
        lru_wx=_block_diag(lru_wx[l]).astype(BF16), lru_bx=lru_bx[l].reshape(1, W_D),
        lru_lam=lru_lambda[l].reshape(1, W_D),
    )


def _ssd_args(rest, small, states, p):
    return (rest, small, states[1], states[0], p["ssd_cw"], p["ssd_cb"], p["ssd_dtb"], p["ssd_alog"],
            p["ssd_dvec"], p["ssd_nw"])


def _mixers(rest, small, states, p, B, L, ydt, with_ssd=True):
    _, _, gla_s0, lru_h0, lru_buf = states
    out = {}
    if with_ssd:
        out["yb"], out["ssd_s"], out["ssd_c"] = _ssd(*_ssd_args(rest, small, states, p), B, L, ydt)
    yc, gla_st = _gla(rest, small, jnp.swapaxes(gla_s0, -1, -2), p["gla_w2p"], p["gla_gb"],
                      p["gla_nw"], B, L, ydt)
    yd, lru_h, lru_c = _lru(rest, lru_buf, lru_h0.reshape(B, 1, W_D), p["lru_cw"], p["lru_cb"],
                            p["lru_wa"], p["lru_ba"], p["lru_wx"], p["lru_bx"], p["lru_lam"], B, L, ydt)
    out.update(yc=yc, yd=yd, gla_s=jnp.swapaxes(gla_st, -1, -2), lru_h=lru_h.reshape(B, W_D),
               lru_c=lru_c)
    return out


def _finish_layer(x2d, ya, m, p, T, final_w, final):
    ys = (ya, m["yb"].reshape(T, W_GROUP), m["yc"].reshape(T, W_GROUP), m["yd"].reshape(T, W_D))
    x_new = _outproj(ys, p["w_out4"], x2d, final_w, final)
    return x_new, (m["ssd_s"], m["ssd_c"], m["gla_s"], m["lru_h"], m["lru_c"])


def _layer_pair(hp, hs, Bp, Lp, Bs, Ls, tab_p, tab_s, p, st_p, st_s, past, layer, depth, kv_p, kv_s,
                final_w, final):
    cache_k, cache_v, page_table = past
    ydt_p = BF16 if Lp % 16 == 0 else F32
    ydt_s = BF16 if Ls % 16 == 0 else F32
    q_p, k_p, v_p, rest_p, small_p, kf_p, vf_p = _inproj(
        hp, p["norm_w"], p["w_main"], p["w_small"], tab_p, Lp, layer, depth, kv_p)
    q_s, k_s, v_s, rest_s, small_s, kf_s, vf_s = _inproj(
        hs, p["norm_w"], p["w_main"], p["w_small"], tab_s, Ls, layer, depth, kv_s)
    yb_p, ssd_s_p, ssd_c_p, ya_s = _ssd_decode(
        _ssd_args(rest_p, small_p, st_p, p), Bp, Lp, ydt_p, p["lam"], q_s, k_s, v_s, rest_s,
        p["subln_w"], cache_k, cache_v, page_table, layer, Bs, Ls, p["lam_init"])
    ya_p = _attn_prompt(p["lam"], q_p, k_p, v_p, rest_p, p["subln_w"], Bp, Lp, p["lam_init"])
    m_p = _mixers(rest_p, small_p, st_p, p, Bp, Lp, ydt_p, with_ssd=False)
    m_p.update(yb=yb_p, ssd_s=ssd_s_p, ssd_c=ssd_c_p)
    m_s = _mixers(rest_s, small_s, st_s, p, Bs, Ls, ydt_s)
    hp, new_p = _finish_layer(hp, ya_p, m_p, p, Bp * Lp, final_w, final)
    hs, new_s = _finish_layer(hs, ya_s, m_s, p, Bs * Ls, final_w, final)
    return hp, hs, new_p, new_s, (kf_p, vf_p), (kf_s, vf_s)


def kernel(x_prompt, x_sample, cache_k, cache_v, page_table, state_ssd, state_ssd_conv, state_gla, state_lru, state_lru_conv, norm_w, w_in, w_out, lam_q1, lam_k1, lam_q2, lam_k2, subln_w, ssd_conv_w, ssd_conv_b, ssd_dt_bias, ssd_a_log, ssd_d, ssd_norm_w, gla_gk_w2, gla_gk_b, gla_norm_w, lru_conv_w, lru_conv_b, lru_wa, lru_ba, lru_wx, lru_bx, lru_lambda, final_norm_w):
    Bp, Lp, _ = x_prompt.shape
    Bs, Ls, _ = x_sample.shape
    depth = w_in.shape[0]
    past_len = page_table.shape[1] * cache_k.shape[2]
    tab_p = _rope_table(jnp.arange(Lp, dtype=jnp.int32), max(Lp, min(INPROJ_ROWS, Bp * Lp)))
    tab_s = _rope_table(past_len + jnp.arange(Ls, dtype=jnp.int32), max(Ls, min(INPROJ_ROWS, Bs * Ls)))
    final_w = final_norm_w.reshape(1, D_MODEL)
    hp = x_prompt.reshape(Bp * Lp, D_MODEL)
    hs = x_sample.reshape(Bs * Ls, D_MODEL)
    sp, ss = [], []
    kv_p = kv_s = None
    for l in range(depth):
        p = _prep_layer(l, norm_w, w_in, w_out, lam_q1, lam_k1, lam_q2, lam_k2, subln_w, ssd_conv_w,
                        ssd_conv_b, ssd_dt_bias, ssd_a_log, ssd_d, ssd_norm_w, gla_gk_w2, gla_gk_b,
                        gla_norm_w, lru_conv_w, lru_conv_b, lru_wa, lru_ba, lru_wx, lru_bx, lru_lambda)
        final = l == depth - 1
        st_p = (jnp.zeros((Bp, H_B, P_B, N_B), F32), jnp.zeros((Bp, CONV_W - 1, CONV_B_CH), F32),
                jnp.zeros((Bp, H_C, DK_C, DV_C), F32), jnp.zeros((Bp, W_D), F32),
                jnp.zeros((Bp, CONV_W - 1, W_D), F32))
        st_s = (state_ssd[l], state_ssd_conv[l], state_gla[l], state_lru[l], state_lru_conv[l])
        hp, hs, new_p, new_s, kv_p, kv_s = _layer_pair(
            hp, hs, Bp, Lp, Bs, Ls, tab_p, tab_s, p, st_p, st_s, (cache_k, cache_v, page_table), l,
            depth, kv_p, kv_s, final_w, final)
        sp.append(new_p)
        ss.append(new_s)
    outs_p = [x.reshape(depth, Bp, Lp, H_A, LANES) for x in kv_p]
    outs_p += [jnp.stack([s[i] for s in sp]) for i in range(5)]
    outs_s = [x.reshape(depth, Bs, Ls, H_A, LANES) for x in kv_s]
    outs_s += [jnp.stack([s[i] for s in ss]) for i in range(5)]
    return (hp.reshape(Bp, Lp, D_MODEL), hs.reshape(Bs, Ls, D_MODEL), *outs_p, *outs_s)
```

```python
import functools
import math

import jax
import jax.numpy as jnp
from jax import lax
from jax.experimental import pallas as pl
from jax.experimental.pallas import tpu as pltpu

F32 = jnp.float32
BF16 = jnp.bfloat16
EPS = 1e-6
NEG = -1e30
LOG2E = math.log2(math.e)

D_MODEL = 2048
W_GROUP = 512
H_A, DV_A, DK_A = 4, 128, 64
ROPE_DIM, ROPE_THETA = 16, 500000.0
H_B, P_B, N_B, G_B = 8, 64, 128, 2
CONV_W = 4
CONV_B_CH = W_GROUP + 2 * G_B * N_B
H_C, DK_C, DV_C = 4, 64, 128
GK_RANK, GK_NORM = 16, 16.0
W_D, NB_D, BW_D = 512, 8, 64
LRU_C = 8.0

LANES = 128
SUBLANES = 8
COL_BLK = 512
INPROJ_ROWS = 1024
ATTN_BQ = 512
ATTN_BK = 512
MIXER_SEQS = 4
VMEM_LIMIT = 56 * 1024 * 1024

_SPLITS = (512, 512, 512, 512, CONV_B_CH, H_B, W_GROUP, 256, 256, 512, GK_RANK, W_GROUP, W_D, W_D)
_OFF = [0]
for _s in _SPLITS:
    _OFF.append(_OFF[-1] + _s)
(_QA, _KA, _VA, _ZA, _XBC, _DT, _ZB, _QC, _KC, _VC, _GKLR, _GC, _XD, _ZD) = _OFF[:-1]

REST_COLS = 4608


def _cparams(sem):
    return pltpu.CompilerParams(dimension_semantics=sem, vmem_limit_bytes=VMEM_LIMIT)


def _sigmoid(x):
    return jax.nn.sigmoid(x)


def _silu(x):
    return x * _sigmoid(x)


def _softplus(x):
    return jnp.maximum(x, 0.0) + jnp.log1p(jnp.exp(-jnp.abs(x)))


def _rmsnorm_rows(x, w):
    return x * lax.rsqrt(jnp.mean(x * x, axis=-1, keepdims=True) + EPS) * w


def _dot(a, b):
    return jnp.dot(a, b, preferred_element_type=F32)


def _dot_nt(a, b):
    return lax.dot_general(a, b, (((1,), (1,)), ((), ())), preferred_element_type=F32)


def _dot_tn(a, b):
    return lax.dot_general(a, b, (((0,), (0,)), ((), ())), preferred_element_type=F32)


def _dot_exact(a, b):
    return jnp.dot(a, b, preferred_element_type=F32, precision=lax.Precision.HIGHEST)


def _lower_tri(n):
    r = lax.broadcasted_iota(jnp.int32, (n, n), 0)
    c = lax.broadcasted_iota(jnp.int32, (n, n), 1)
    return r >= c


def _batch_group(b, want):
    return math.gcd(b, want)


def _inproj_kernel(x_ref, nw_ref, w_ref, ws_ref, rope_ref,
                   q_ref, k_ref, v_ref, rest_ref, small_ref, kf_ref, vf_ref, u_scr):
    j = pl.program_id(1)
    tm = x_ref.shape[0]

    @pl.when(j == 0)
    def _():
        u = _rmsnorm_rows(x_ref[...], nw_ref[...]).astype(BF16)
        u_scr[...] = u
        small_ref[...] = _dot(u, ws_ref[...])

    acc = _dot(u_scr[...], w_ref[...])

    def rope_store(dst, scale, dst_heads=None):
        c, s1, s2 = rope_ref[0], rope_ref[1], rope_ref[2]
        for hh in range(COL_BLK // LANES):
            sl = slice(hh * LANES, (hh + 1) * LANES)
            a = acc[:, sl]
            r = (a * c + pltpu.roll(a, LANES - ROPE_DIM // 2, 1) * s1
                 + pltpu.roll(a, ROPE_DIM // 2, 1) * s2)
            dst[:, sl] = r * scale
            if dst_heads is not None:
                dst_heads[pl.ds(hh, tm, stride=H_A), :] = r

    @pl.when(j == 0)
    def _():
        rope_store(q_ref, DK_A ** -0.5 * LOG2E)

    @pl.when(j == 1)
    def _():
        rope_store(k_ref, 1.0, kf_ref)

    @pl.when(j == 2)
    def _():
        v_ref[...] = acc
        for hh in range(H_A):
            vf_ref[pl.ds(hh, tm, stride=H_A), :] = acc[:, hh * LANES:(hh + 1) * LANES]

    @pl.when(j >= 3)
    def _():
        rest_ref[...] = acc


def _inproj_kernel_aliased(x_ref, nw_ref, w_ref, ws_ref, rope_ref, kprev_ref, vprev_ref, *refs):
    del kprev_ref, vprev_ref
    _inproj_kernel(x_ref, nw_ref, w_ref, ws_ref, rope_ref, *refs)


def _inproj(x2d, norm_w, w_main, w_small, rope_tab, seq_len, layer, depth, kv_prev):
    T = x2d.shape[0]
    tm = min(INPROJ_ROWS, T)
    assert T % tm == 0 and (seq_len % tm == 0 or tm % seq_len == 0)
    n_pos_blk = max(seq_len // tm, 1)
    nj = w_main.shape[0]
    kv_shape = jax.ShapeDtypeStruct((depth, T * H_A, LANES), F32)
    out_shape = (
        jax.ShapeDtypeStruct((T, COL_BLK), F32),
        jax.ShapeDtypeStruct((T, COL_BLK), F32),
        jax.ShapeDtypeStruct((T, COL_BLK), F32),
        jax.ShapeDtypeStruct((T, REST_COLS), F32),
        jax.ShapeDtypeStruct((T, LANES), F32),
        kv_shape, kv_shape,
    )
    row_blk = pl.BlockSpec((tm, COL_BLK), lambda i, j: (i, 0))
    kv_blk = pl.BlockSpec((None, tm * H_A, LANES), lambda i, j: (layer, i, 0))
    in_specs = [
        pl.BlockSpec((tm, D_MODEL), lambda i, j: (i, 0)),
        pl.BlockSpec((1, D_MODEL), lambda i, j: (0, 0)),
        pl.BlockSpec((None, D_MODEL, COL_BLK), lambda i, j: (j, 0, 0)),
        pl.BlockSpec((D_MODEL, LANES), lambda i, j: (0, 0)),
        pl.BlockSpec((3, tm, LANES), lambda i, j: (0, i % n_pos_blk, 0)),
    ]
    args = (x2d, norm_w, w_main, w_small, rope_tab)
    aliases = {}
    kern = _inproj_kernel
    if kv_prev is not None:
        in_specs += [pl.BlockSpec(memory_space=pl.ANY)] * 2
        aliases = {len(args): 5, len(args) + 1: 6}
        args += tuple(kv_prev)
        kern = _inproj_kernel_aliased
    return pl.pallas_call(
        kern,
        grid=(T // tm, nj),
        in_specs=in_specs,
        out_specs=(
            row_blk, row_blk, row_blk,
            pl.BlockSpec((tm, COL_BLK), lambda i, j: (i, jnp.maximum(j - 3, 0))),
            pl.BlockSpec((tm, LANES), lambda i, j: (i, 0)),
            kv_blk, kv_blk,
        ),
        out_shape=out_shape,
        input_output_aliases=aliases,
        scratch_shapes=[pltpu.VMEM((tm, D_MODEL), BF16)],
        compiler_params=_cparams(("arbitrary", "arbitrary")),
        name="inproj",
    )(*args)


def _attn_finish(o, lam_init, w, z):
    o = _rmsnorm_rows(o, w) * (1.0 - lam_init)
    return o * _silu(z)


def _attn_kernel(lam_ref, q_ref, k_ref, v_ref, z_ref, w_ref, o_ref, kb_scr, vt_scr,
                 m_scr, l_scr, acc_scr, *, bq, bk, lam_init):
    qi = pl.program_id(2)
    n_kb = kb_scr.shape[0]
    diag_blocks = bq // bk

    @pl.when(qi == 0)
    def _():
        for c in range(n_kb):
            rows = slice(c * bk, (c + 1) * bk)
            kb_scr[c] = k_ref[rows, :].astype(BF16)
            vt_scr[c] = v_ref[rows, :].T.astype(BF16)

    qt = q_ref[...].T
    feat = lax.broadcasted_iota(jnp.int32, qt.shape, 0)
    qts = (jnp.where(feat < DK_A, qt, 0.0).astype(BF16),
           jnp.where(feat >= DK_A, qt, 0.0).astype(BF16))
    m_scr[...] = jnp.full(m_scr.shape, NEG, F32)
    l_scr[...] = jnp.zeros(l_scr.shape, F32)
    acc_scr[...] = jnp.zeros(acc_scr.shape, F32)

    halves = 2
    hw = bq // halves

    def step(j, mask):
        kb = kb_scr[j]
        vtb = vt_scr[j]
        chains = [(m, hf) for hf in range(halves) for m in range(2)]
        ss = [_dot(kb, qts[m][:, hf * hw:(hf + 1) * hw]) for m, hf in chains]
        for (m, hf), s in zip(chains, ss):
            cols = slice(hf * hw, (hf + 1) * hw)
            if mask is not None:
                s = jnp.where(mask[:, cols], s, NEG)
            mx = m_scr[m, :, cols]
            mn = jnp.maximum(mx, jnp.max(s, axis=0, keepdims=True))
            alpha = jnp.exp2(mx - mn)
            p = jnp.exp2(s - mn)
            l_scr[m, :, cols] = alpha * l_scr[m, :, cols] + jnp.sum(p, axis=0, keepdims=True)
            acc_scr[m, :, cols] = alpha * acc_scr[m, :, cols] + _dot(vtb, p.astype(BF16))
            m_scr[m, :, cols] = mn

    def unmasked(j, carry):
        step(j, None)
        return carry

    lax.fori_loop(0, qi * diag_blocks, unmasked, 0)
    key_i = lax.broadcasted_iota(jnp.int32, (bk, bq), 0)
    qry_i = lax.broadcasted_iota(jnp.int32, (bk, bq), 1)
    for d in range(diag_blocks):
        step(qi * diag_blocks + d, key_i + d * bk <= qry_i)
    lam = lam_ref[0, 0]
    o = (acc_scr[0] / l_scr[0] - lam * (acc_scr[1] / l_scr[1])).T
    o_ref[...] = _attn_finish(o, lam_init, w_ref[...], z_ref[...]).astype(o_ref.dtype)


def _attn_prompt(lam, q, k, v, rest, subln_w, B, L, lam_init):
    bq = min(ATTN_BQ, L)
    bk = min(ATTN_BK, bq)
    nq = L // bq
    kern = functools.partial(_attn_kernel, bq=bq, bk=bk, lam_init=lam_init)
    return pl.pallas_call(
        kern,
        grid=(B, H_A, nq),
        in_specs=[
            pl.BlockSpec(memory_space=pltpu.SMEM),
            pl.BlockSpec((bq, LANES), lambda b, h, i: (b * nq + i, h)),
            pl.BlockSpec((L, LANES), lambda b, h, i: (b, h)),
            pl.BlockSpec((L, LANES), lambda b, h, i: (b, h)),
            pl.BlockSpec((bq, LANES), lambda b, h, i: (b * nq + i, 8 + h)),
            pl.BlockSpec((1, LANES), lambda b, h, i: (0, 0)),
        ],
        out_specs=pl.BlockSpec((bq, LANES), lambda b, h, i: (b * nq + i, h)),
        out_shape=jax.ShapeDtypeStruct((B * L, W_GROUP), BF16),
        scratch_shapes=[
            pltpu.VMEM((L // bk, bk, LANES), BF16),
            pltpu.VMEM((L // bk, DV_A, bk), BF16),
            pltpu.VMEM((2, 1, bq), F32),
            pltpu.VMEM((2, 1, bq), F32),
            pltpu.VMEM((2, DV_A, bq), F32),
        ],
        compiler_params=_cparams(("arbitrary", "arbitrary", "arbitrary")),
        name="attn_prompt",
    )(lam, q, k, v, rest, subln_w)


def _decode_init(q_ref, qrow_scr, m_scr, l_scr, acc_scr, ls):
    rows = 2 * ls
    r = lax.broadcasted_iota(jnp.int32, (rows, LANES), 0)
    c = lax.broadcasted_iota(jnp.int32, (rows, LANES), 1)
    own_map = (r // ls) == (c // DK_A)
    for h in range(H_A):
        qh = q_ref[:, h * LANES:(h + 1) * LANES]
        qrow_scr[h] = jnp.where(own_map, jnp.concatenate([qh, qh], axis=0), 0.0).astype(BF16)
    m_scr[...] = jnp.full(m_scr.shape, NEG, F32)
    l_scr[...] = jnp.zeros(l_scr.shape, F32)
    acc_scr[...] = jnp.zeros(acc_scr.shape, F32)


def _decode_head_rows(pages, h, page):
    head_rows = pl.ds(h, page, stride=H_A)
    return jnp.concatenate([pg[head_rows, :] for pg in pages], axis=0).astype(BF16)


def _decode_accumulate(h, s, vv, m_scr, l_scr, acc_scr):
    mx = m_scr[h]
    mn = jnp.maximum(mx, jnp.max(s, axis=-1, keepdims=True))
    alpha = jnp.exp2(mx - mn)
    pr = jnp.exp2(s - mn)
    l_scr[h] = alpha * l_scr[h] + jnp.sum(pr, axis=-1, keepdims=True)
    acc_scr[h] = alpha * acc_scr[h] + _dot(pr.astype(BF16), vv)
    m_scr[h] = mn


def _decode_finish(lam_ref, kn_ref, vn_ref, z_ref, w_ref, o_ref, qrow_scr, m_scr, l_scr, acc_scr,
                   ls, lam_init):
    rows = 2 * ls
    lam = lam_ref[0, 0]
    w = w_ref[...]
    jj = lax.broadcasted_iota(jnp.int32, (rows, ls), 1)
    qq = lax.broadcasted_iota(jnp.int32, (rows, ls), 0) % ls
    for h in range(H_A):
        cs = slice(h * LANES, (h + 1) * LANES)
        s = _dot_nt(qrow_scr[h], kn_ref[:, cs].astype(BF16))
        _decode_accumulate(h, jnp.where(jj <= qq, s, NEG), vn_ref[:, cs].astype(BF16),
                           m_scr, l_scr, acc_scr)
        o = acc_scr[h] / l_scr[h]
        o = o[:ls] - lam * o[ls:]
        o_ref[:, cs] = _attn_finish(o, lam_init, w, z_ref[:, cs]).astype(o_ref.dtype)


CONV_PAD = SUBLANES
CONV_TAIL = CONV_PAD - (CONV_W - 1)


def _conv_step(x_ref, cw_ref, cb_ref, xp_scr, q):
    xp_scr[CONV_PAD:CONV_PAD + q, :] = x_ref[...]
    y = xp_scr[CONV_TAIL:CONV_TAIL + q, :] * cw_ref[0:1, :]
    for t in range(1, CONV_W):
        y = y + xp_scr[CONV_TAIL + t:CONV_TAIL + t + q, :] * cw_ref[t:t + 1, :]
    y = y + cb_ref[...]
    xp_scr[CONV_TAIL:CONV_PAD, :] = xp_scr[q + CONV_TAIL:q + CONV_PAD, :]
    return y


def _ssd_inputs(xbc_ref, dt_ref, cw_ref, cb_ref, dtb_ref, xp_scr, q, nb):
    xbc = [_silu(_conv_step(xbc_ref.at[b], cw_ref, cb_ref, xp_scr.at[b], q)) for b in range(nb)]
    dt = [_softplus(dt_ref[b] + dtb_ref[...]) for b in range(nb)]
    return xbc, dt


def _ssd_chunk_pairs(xbc_all, dt_all, zb_ref, alog_ref, dvec_ref, nw_ref, y_ref, s_scr, y_scr, q, nb):
    half = LANES // 2
    row = lax.broadcasted_iota(jnp.int32, (q, LANES), 0)
    lane = lax.broadcasted_iota(jnp.int32, (q, LANES), 1)
    hi = lane >= half
    tri2 = row >= lane % half
    a = -jnp.exp(alog_ref[...])
    tri_f = _lower_tri(q).astype(F32)
    heads_per_group = H_B // G_B
    zeros_state = jnp.zeros((P_B, N_B), BF16)
    for b in range(nb):
        xbc, dt = xbc_all[b], dt_all[b]
        xs = xbc[:, :W_GROUP]
        cum = _dot_exact(tri_f, dt * a)
        cum_t = cum.T
        dt_t = dt.T
        last = cum[q - 1:q, :]
        ecum = jnp.exp(cum)
        wj = jnp.exp(last - cum) * dt
        elast = jnp.exp(last)
        for g in range(G_B):
            bgb = xbc[:, W_GROUP + g * N_B:W_GROUP + (g + 1) * N_B].astype(BF16)
            cg = xbc[:, W_GROUP + (G_B + g) * N_B:W_GROUP + (G_B + g + 1) * N_B]
            cb2 = _dot_nt(cg.astype(BF16), jnp.concatenate([bgb, bgb], axis=0))
            for pair in range(heads_per_group // 2):
                ha = g * heads_per_group + 2 * pair
                hb = ha + 1
                lanes = slice(ha * P_B, (hb + 1) * P_B)
                pick = lambda v: jnp.where(hi, v[:, hb:hb + 1], v[:, ha:ha + 1])
                cj = jnp.concatenate([cum_t[ha:ha + 1, :], cum_t[hb:hb + 1, :]], axis=1)
                dtj = jnp.concatenate([dt_t[ha:ha + 1, :], dt_t[hb:hb + 1, :]], axis=1)
                dec = jnp.exp(jnp.where(tri2, pick(cum) - cj, NEG))
                mh = (cb2 * dec * dtj).astype(BF16)
                xp = xs[:, lanes]
                x_bd = jnp.concatenate([jnp.where(hi, 0.0, xp), jnp.where(hi, xp, 0.0)],
                                       axis=0).astype(BF16)
                sa, sb = s_scr[b, ha], s_scr[b, hb]
                s_bd = jnp.concatenate(
                    [jnp.concatenate([sa.astype(BF16), zeros_state], axis=1),
                     jnp.concatenate([zeros_state, sb.astype(BF16)], axis=1)], axis=0)
                c_dec = jnp.concatenate([cg * ecum[:, ha:ha + 1], cg * ecum[:, hb:hb + 1]],
                                        axis=1).astype(BF16)
                y_scr[b, :, lanes] = _dot(mh, x_bd) + _dot_nt(c_dec, s_bd)
                upd = _dot_tn((xp * pick(wj)).astype(BF16), bgb)
                s_scr[b, ha] = elast[:, ha:ha + 1] * sa + upd[:P_B]
                s_scr[b, hb] = elast[:, hb:hb + 1] * sb + upd[P_B:]
        y = y_scr[b] + dvec_ref[...] * xs
        y_ref[b] = _rmsnorm_rows(y * _silu(zb_ref[b]), nw_ref[...]).astype(y_ref.dtype)


def _ssd_chunk(xbc_all, dt_all, zb_ref, alog_ref, dvec_ref, nw_ref, y_ref, s_scr, y_scr, q, nb):
    if 2 * q == LANES and 2 * P_B == LANES:
        return _ssd_chunk_pairs(xbc_all, dt_all, zb_ref, alog_ref, dvec_ref, nw_ref, y_ref, s_scr,
                                y_scr, q, nb)
    tri = _lower_tri(q)
    tri_f = tri.astype(F32)
    a = -jnp.exp(alog_ref[...])
    heads_per_group = H_B // G_B
    for b in range(nb):
        xbc, dt = xbc_all[b], dt_all[b]
        xs = xbc[:, :W_GROUP]
        cum = _dot_exact(tri_f, dt * a)
        cum_t = cum.T
        dt_t = dt.T
        last = cum[q - 1:q, :]
        ecum = jnp.exp(cum)
        wj = jnp.exp(last - cum) * dt
        elast = jnp.exp(last)
        for g in range(G_B):
            bg = xbc[:, W_GROUP + g * N_B:W_GROUP + (g + 1) * N_B]
            cg = xbc[:, W_GROUP + (G_B + g) * N_B:W_GROUP + (G_B + g + 1) * N_B]
            bgb = bg.astype(BF16)
            cb_g = _dot_nt(cg.astype(BF16), bgb)
            for hh in range(heads_per_group):
                h = g * heads_per_group + hh
                hs = slice(h * P_B, (h + 1) * P_B)
                seg = cum[:, h:h + 1] - cum_t[h:h + 1, :]
                dec = jnp.exp(jnp.where(tri, seg, NEG))
                mh = (cb_g * dec * dt_t[h:h + 1, :]).astype(BF16)
                xh = xs[:, hs]
                s_old = s_scr[b, h]
                yh = _dot(mh, xh.astype(BF16)) + _dot_nt((cg * ecum[:, h:h + 1]).astype(BF16),
                                                         s_old.astype(BF16))
                y_scr[b, :, hs] = yh
                xw = (xh * wj[:, h:h + 1]).astype(BF16)
                s_scr[b, h] = elast[:, h:h + 1] * s_old + _dot_tn(xw, bgb)
        y = y_scr[b] + dvec_ref[...] * xs
        y_ref[b] = _rmsnorm_rows(y * _silu(zb_ref[b]), nw_ref[...]).astype(y_ref.dtype)


def _ssd_kernel(xbc_ref, dt_ref, zb_ref, buf_ref, s0_ref, cw_ref, cb_ref, dtb_ref, alog_ref,
                dvec_ref, nw_ref, y_ref, sout_ref, cout_ref, xp_scr, s_scr, y_scr, *, q, nb):
    c = pl.program_id(1)

    @pl.when(c == 0)
    def _():
        s_scr[...] = s0_ref[...]
        xp_scr[:, CONV_TAIL:CONV_PAD, :] = buf_ref[...]

    xbc, dt = _ssd_inputs(xbc_ref, dt_ref, cw_ref, cb_ref, dtb_ref, xp_scr, q, nb)
    _ssd_chunk(xbc, dt, zb_ref, alog_ref, dvec_ref, nw_ref, y_ref, s_scr, y_scr, q, nb)

    @pl.when(c == pl.num_programs(1) - 1)
    def _():
        sout_ref[...] = s_scr[...]
        cout_ref[...] = xp_scr[:, CONV_TAIL:CONV_PAD, :]


def _ssd_decode_kernel(pt_ref, lam_ref, xbc_ref, dt_ref, zb_ref, buf_ref, s0_ref, cw_ref, cb_ref,
                       dtb_ref, alog_ref, dvec_ref, nw_ref, q_ref, kn_ref, vn_ref, z_ref, w_ref,
                       *refs, q, nb, n_pp, steps_per_seq, ls, page, lam_init):
    del pt_ref
    kp = refs[:n_pp]
    vp = refs[n_pp:2 * n_pp]
    y_ref, sout_ref, cout_ref, o_ref = refs[2 * n_pp:2 * n_pp + 4]
    xp_scr, s_scr, y_scr, qrow_scr, m_scr, l_scr, acc_scr = refs[2 * n_pp + 4:]
    c = pl.program_id(1)
    part = (pl.program_id(0) * pl.num_programs(1) + c) % steps_per_seq

    @pl.when(c == 0)
    def _():
        s_scr[...] = s0_ref[...]
        xp_scr[:, CONV_TAIL:CONV_PAD, :] = buf_ref[...]

    @pl.when(part == 0)
    def _():
        _decode_init(q_ref, qrow_scr, m_scr, l_scr, acc_scr, ls)

    scores = [_dot_nt(qrow_scr[h], _decode_head_rows(kp, h, page)) for h in range(H_A)]
    xbc, dt = _ssd_inputs(xbc_ref, dt_ref, cw_ref, cb_ref, dtb_ref, xp_scr, q, nb)
    for h in range(H_A):
        _decode_accumulate(h, scores[h], _decode_head_rows(vp, h, page), m_scr, l_scr, acc_scr)
    _ssd_chunk(xbc, dt, zb_ref, alog_ref, dvec_ref, nw_ref, y_ref, s_scr, y_scr, q, nb)

    @pl.when(c == pl.num_programs(1) - 1)
    def _():
        sout_ref[...] = s_scr[...]
        cout_ref[...] = xp_scr[:, CONV_TAIL:CONV_PAD, :]

    @pl.when(part == steps_per_seq - 1)
    def _():
        _decode_finish(lam_ref, kn_ref, vn_ref, z_ref, w_ref, o_ref, qrow_scr, m_scr, l_scr,
                       acc_scr, ls, lam_init)


def _ssd_specs(B, L, nb, q, index):
    full = lambda shape: pl.BlockSpec(shape, index(lambda b, c: (0,) * len(shape)))
    in_specs = [
        pl.BlockSpec((nb, q, CONV_B_CH), index(lambda b, c: (b, c, 0))),
        pl.BlockSpec((nb, q, LANES), index(lambda b, c: (b, c, 0))),
        pl.BlockSpec((nb, q, W_GROUP), index(lambda b, c: (b, c, 3))),
        pl.BlockSpec((nb, CONV_W - 1, CONV_B_CH), index(lambda b, c: (b, 0, 0))),
        pl.BlockSpec((nb, H_B, P_B, N_B), index(lambda b, c: (b, 0, 0, 0))),
        full((CONV_W, CONV_B_CH)), full((1, CONV_B_CH)), full((1, LANES)), full((1, LANES)),
        full((1, W_GROUP)), full((1, W_GROUP)),
    ]
    out_specs = [
        pl.BlockSpec((nb, q, W_GROUP), index(lambda b, c: (b, c, 0))),
        pl.BlockSpec((nb, H_B, P_B, N_B), index(lambda b, c: (b, 0, 0, 0))),
        pl.BlockSpec((nb, CONV_W - 1, CONV_B_CH), index(lambda b, c: (b, 0, 0))),
    ]
    scratch = [
        pltpu.VMEM((nb, q + SUBLANES, CONV_B_CH), F32),
        pltpu.VMEM((nb, H_B, P_B, N_B), F32),
        pltpu.VMEM((nb, q, W_GROUP), F32),
    ]
    return in_specs, out_specs, scratch


def _ssd_out_shape(B, L, out_dtype):
    return [
        jax.ShapeDtypeStruct((B, L, W_GROUP), out_dtype),
        jax.ShapeDtypeStruct((B, H_B, P_B, N_B), F32),
        jax.ShapeDtypeStruct((B, CONV_W - 1, CONV_B_CH), F32),
    ]


def _ssd(rest, small, buf, s0, cw, cb, dtb, alog, dvec, nw, B, L, out_dtype):
    q = math.gcd(L, 64)
    nc = L // q
    nb = _batch_group(B, MIXER_SEQS)
    in_specs, out_specs, scratch = _ssd_specs(B, L, nb, q, lambda f: f)
    rest3 = rest.reshape(B, L, REST_COLS)
    return pl.pallas_call(
        functools.partial(_ssd_kernel, q=q, nb=nb),
        grid=(B // nb, nc),
        in_specs=in_specs,
        out_specs=tuple(out_specs),
        out_shape=tuple(_ssd_out_shape(B, L, out_dtype)),
        scratch_shapes=scratch,
        compiler_params=_cparams(("arbitrary", "arbitrary")),
        name="ssd",
    )(rest3, small.reshape(B, L, LANES), rest3, buf, s0, cw, cb, dtb, alog, dvec, nw)


def _ssd_decode(ssd_args, B, L, out_dtype, lam, q_s, k_s, v_s, rest_s, subln_w, cache_k, cache_v,
                page_table, layer, Bs, Ls, lam_init):
    rest, small, buf, s0, cw, cb, dtb, alog, dvec, nw = ssd_args
    q = math.gcd(L, 64)
    nc = L // q
    nb = _batch_group(B, MIXER_SEQS)
    steps = (B // nb) * nc
    n_pages = page_table.shape[1]
    page = cache_k.shape[2]
    assert (Bs * n_pages) % steps == 0, "decode pages must spread evenly over the SSD grid"
    n_pp = Bs * n_pages // steps
    assert n_pages % n_pp == 0
    steps_per_seq = n_pages // n_pp
    rows = 2 * Ls
    ck = cache_k.reshape(cache_k.shape[0], cache_k.shape[1], page * H_A, 2 * DK_A)
    cv = cache_v.reshape(cache_v.shape[0], cache_v.shape[1], page * H_A, DV_A)

    def step_of(b, c):
        return b * nc + c

    def page_spec(i):
        def index(b, c, pt):
            t = step_of(b, c)
            return (layer, pt[t // steps_per_seq, (t % steps_per_seq) * n_pp + i], 0, 0)
        return pl.BlockSpec((None, None, page * H_A, LANES), index)

    with_pt = lambda f: (lambda b, c, pt: f(b, c))
    ssd_in, ssd_out, ssd_scratch = _ssd_specs(B, L, nb, q, with_pt)
    tok = pl.BlockSpec((Ls, W_GROUP), lambda b, c, pt: (step_of(b, c) // steps_per_seq, 0))
    kern = functools.partial(_ssd_decode_kernel, q=q, nb=nb, n_pp=n_pp, steps_per_seq=steps_per_seq,
                             ls=Ls, page=page, lam_init=lam_init)
    grid_spec = pltpu.PrefetchScalarGridSpec(
        num_scalar_prefetch=1,
        grid=(B // nb, nc),
        in_specs=[pl.BlockSpec(memory_space=pltpu.SMEM)] + ssd_in + [
            tok, tok, tok,
            pl.BlockSpec((Ls, W_GROUP), lambda b, c, pt: (step_of(b, c) // steps_per_seq, 2)),
            pl.BlockSpec((1, LANES), lambda b, c, pt: (0, 0)),
        ] + [page_spec(i) for i in range(n_pp)] * 2,
        out_specs=tuple(ssd_out + [tok]),
        scratch_shapes=ssd_scratch + [
            pltpu.VMEM((H_A, rows, LANES), BF16),
            pltpu.VMEM((H_A, rows, 1), F32),
            pltpu.VMEM((H_A, rows, 1), F32),
            pltpu.VMEM((H_A, rows, DV_A), F32),
        ],
    )
    rest3 = rest.reshape(B, L, REST_COLS)
    return pl.pallas_call(
        kern,
        grid_spec=grid_spec,
        out_shape=tuple(_ssd_out_shape(B, L, out_dtype)
                        + [jax.ShapeDtypeStruct((Bs * Ls, W_GROUP), F32)]),
        compiler_params=_cparams(("arbitrary", "arbitrary")),
        name="ssd_decode",
    )(page_table, lam, rest3, small.reshape(B, L, LANES), rest3, buf, s0, cw, cb, dtb, alog, dvec, nw,
      q_s, k_s, v_s, rest_s, subln_w, *([ck] * n_pp), *([cv] * n_pp))


def _gla_kernel(qk_ref, v_ref, gc_ref, sm_ref, s0_ref, w2_ref, gb_ref, nw_ref,
                y_ref, sout_ref, st_scr, *, q, nb):
    c = pl.program_id(1)

    @pl.when(c == 0)
    def _():
        st_scr[...] = s0_ref[...]

    seqs = range(nb)
    heads = range(H_C)
    tri = _lower_tri(q)
    width = H_C * DK_C
    nw = nw_ref[...]
    sm = sm_ref[...].reshape(nb * q, LANES).astype(BF16)
    gkl = _dot(sm, w2_ref[...]) + gb_ref[...]
    g = (jnp.minimum(gkl, 0.0) - jnp.log1p(jnp.exp(-jnp.abs(gkl)))) * (1.0 / GK_NORM)
    bc_all = _dot_exact(tri.astype(F32), jnp.concatenate([g[b * q:(b + 1) * q] for b in seqs], axis=1))
    bc = [bc_all[:, b * width:(b + 1) * width] for b in seqs]
    last = [x[q - 1:q, :] for x in bc]
    kc = [qk_ref[b, :, width:] for b in seqs]
    qe = [(qk_ref[b, :, :width] * (DK_C ** -0.5) * jnp.exp(bc[b])).astype(BF16) for b in seqs]
    ke = [(kc[b] * jnp.exp(-bc[b])).astype(BF16) for b in seqs]
    kl = [(kc[b] * jnp.exp(last[b] - bc[b])).astype(BF16) for b in seqs]
    el = [jnp.exp(x) for x in last]
    ks = [slice(h * DK_C, (h + 1) * DK_C) for h in heads]
    vs = [slice(h * DV_C, (h + 1) * DV_C) for h in heads]
    vh = [[v_ref[b, :, vs[h]].astype(BF16) for h in heads] for b in seqs]
    st_old = [[st_scr[b, h] for h in heads] for b in seqs]
    att = [[_dot_nt(qe[b][:, ks[h]], ke[b][:, ks[h]]) for h in heads] for b in seqs]
    o_state = [[_dot_nt(qe[b][:, ks[h]], st_old[b][h].astype(BF16)) for h in heads] for b in seqs]
    upd = [[_dot_tn(vh[b][h], kl[b][:, ks[h]]) for h in heads] for b in seqs]
    o = [[_dot(jnp.where(tri, att[b][h], 0.0).astype(BF16), vh[b][h]) + o_state[b][h]
          for h in heads] for b in seqs]
    for b in seqs:
        for h in heads:
            st_scr[b, h] = el[b][:, ks[h]] * st_old[b][h] + upd[b][h]
            y_ref[b, :, vs[h]] = (_rmsnorm_rows(o[b][h], nw)
                                  * _silu(gc_ref[b, :, vs[h]])).astype(y_ref.dtype)

    @pl.when(c == pl.num_programs(1) - 1)
    def _():
        sout_ref[...] = st_scr[...]


def _gla(rest, small, s0_t, w2p, gb, nw, B, L, out_dtype):
    q = min(L, 64)
    nc = L // q
    nb = _batch_group(B, MIXER_SEQS)
    kern = functools.partial(_gla_kernel, q=q, nb=nb)
    full = lambda shape: pl.BlockSpec(shape, lambda b, c: (0,) * len(shape))
    state = pl.BlockSpec((nb, H_C, DV_C, DK_C), lambda b, c: (b, 0, 0, 0))
    rest3 = rest.reshape(B, L, REST_COLS)
    return pl.pallas_call(
        kern,
        grid=(B // nb, nc),
        in_specs=[
            pl.BlockSpec((nb, q, W_GROUP), lambda b, c: (b, c, 4)),
            pl.BlockSpec((nb, q, W_GROUP), lambda b, c: (b, c, 5)),
            pl.BlockSpec((nb, q, W_GROUP), lambda b, c: (b, c, 6)),
            pl.BlockSpec((nb, q, LANES), lambda b, c: (b, c, 0)),
            state,
            full((LANES, H_C * DK_C)), full((1, H_C * DK_C)), full((1, DV_C)),
        ],
        out_specs=(pl.BlockSpec((nb, q, W_GROUP), lambda b, c: (b, c, 0)), state),
        out_shape=(
            jax.ShapeDtypeStruct((B, L, W_GROUP), out_dtype),
            jax.ShapeDtypeStruct((B, H_C, DV_C, DK_C), F32),
        ),
        scratch_shapes=[pltpu.VMEM((nb, H_C, DV_C, DK_C), F32)],
        compiler_params=_cparams(("arbitrary", "arbitrary")),
        name="gla",
    )(rest3, rest3, rest3, small.reshape(B, L, LANES), s0_t, w2p, gb, nw)


def _lru_kernel(xd_ref, zd_ref, buf_ref, h0_ref, cw_ref, cb_ref, wa_ref, ba_ref, wx_ref, bx_ref,
                lam_ref, y_ref, hout_ref, cout_ref, xp_scr, h_scr, a_scr, u_scr, *, q, nb):
    c = pl.program_id(1)

    @pl.when(c == 0)
    def _():
        h_scr[...] = h0_ref[...]
        xp_scr[:, CONV_TAIL:CONV_PAD, :] = buf_ref[...]

    sp_lam = _softplus(-lam_ref[...])
    for b in range(nb):
        xr = _conv_step(xd_ref.at[b], cw_ref, cb_ref, xp_scr.at[b], q)
        xb = xr.astype(BF16)
        r = _sigmoid(_dot(xb, wa_ref[...]) + ba_ref[...])
        i = _sigmoid(_dot(xb, wx_ref[...]) + bx_ref[...])
        log_a = -LRU_C * r * sp_lam
        a = jnp.exp(log_a)
        th = jnp.tanh(log_a)
        u = jnp.sqrt(-2.0 * th / (1.0 - th)) * (i * xr)
        a = a.reshape(q // SUBLANES, SUBLANES, W_D)
        u = u.reshape(q // SUBLANES, SUBLANES, W_D)
        r8 = lax.broadcasted_iota(jnp.int32, a.shape, 1)
        for s in (1, 2, 4):
            keep = r8 >= s
            a_sh = pltpu.roll(a, s, 1)
            u_sh = pltpu.roll(u, s, 1)
            u = jnp.where(keep, a * u_sh + u, u)
            a = jnp.where(keep, a * a_sh, a)
        a_scr[b] = a.reshape(q, W_D)
        u_scr[b] = u.reshape(q, W_D)
    carry = [h_scr[b] for b in range(nb)]
    for gi in range(q // SUBLANES):
        rows = slice(gi * SUBLANES, (gi + 1) * SUBLANES)
        for b in range(nb):
            hg = a_scr[b, rows, :] * carry[b] + u_scr[b, rows, :]
            y_ref[b, rows, :] = (hg * _silu(zd_ref[b, rows, :])).astype(y_ref.dtype)
            carry[b] = hg[SUBLANES - 1:SUBLANES, :]
    for b in range(nb):
        h_scr[b] = carry[b]

    @pl.when(c == pl.num_programs(1) - 1)
    def _():
        hout_ref[...] = h_scr[...]
        cout_ref[...] = xp_scr[:, CONV_TAIL:CONV_PAD, :]


def _lru(rest, buf, h0, cw, cb, wa, ba, wx, bx, lam, B, L, out_dtype):
    q = min(L, 256)
    nc = L // q
    nb = _batch_group(B, MIXER_SEQS)
    kern = functools.partial(_lru_kernel, q=q, nb=nb)
    full = lambda shape: pl.BlockSpec(shape, lambda b, c: (0,) * len(shape))
    rest3 = rest.reshape(B, L, REST_COLS)
    return pl.pallas_call(
        kern,
        grid=(B // nb, nc),
        in_specs=[
            pl.BlockSpec((nb, q, W_D), lambda b, c: (b, c, 7)),
            pl.BlockSpec((nb, q, W_D), lambda b, c: (b, c, 8)),
            pl.BlockSpec((nb, CONV_W - 1, W_D), lambda b, c: (b, 0, 0)),
            pl.BlockSpec((nb, 1, W_D), lambda b, c: (b, 0, 0)),
            full((CONV_W, W_D)), full((1, W_D)),
            full((W_D, W_D)), full((1, W_D)), full((W_D, W_D)), full((1, W_D)), full((1, W_D)),
        ],
        out_specs=(
            pl.BlockSpec((nb, q, W_D), lambda b, c: (b, c, 0)),
            pl.BlockSpec((nb, 1, W_D), lambda b, c: (b, 0, 0)),
            pl.BlockSpec((nb, CONV_W - 1, W_D), lambda b, c: (b, 0, 0)),
        ),
        out_shape=(
            jax.ShapeDtypeStruct((B, L, W_D), out_dtype),
            jax.ShapeDtypeStruct((B, 1, W_D), F32),
            jax.ShapeDtypeStruct((B, CONV_W - 1, W_D), F32),
        ),
        scratch_shapes=[
            pltpu.VMEM((nb, q + SUBLANES, W_D), F32),
            pltpu.VMEM((nb, 1, W_D), F32),
            pltpu.VMEM((nb, q, W_D), F32),
            pltpu.VMEM((nb, q, W_D), F32),
        ],
        compiler_params=_cparams(("arbitrary", "arbitrary")),
        name="lru",
    )(rest3, rest3, buf, h0, cw, cb, wa, ba, wx, bx, lam)


def _outproj_kernel(ya_ref, yb_ref, yc_ref, yd_ref, w_ref, x_ref, fw_ref, o_ref, *, final):
    out = _dot(ya_ref[...].astype(BF16), w_ref[0])
    for g, y_ref in enumerate((yb_ref, yc_ref, yd_ref), start=1):
        out = out + _dot(y_ref[...].astype(BF16), w_ref[g])
    res = x_ref[...] + out
    if final:
        res = _rmsnorm_rows(res, fw_ref[...])
    o_ref[...] = res


def _outproj(ys, w_out4, x2d, final_w, final):
    T = x2d.shape[0]
    tm = min(512, T)
    kern = functools.partial(_outproj_kernel, final=final)
    yspec = pl.BlockSpec((tm, W_GROUP), lambda i: (i, 0))
    return pl.pallas_call(
        kern,
        grid=(T // tm,),
        in_specs=[yspec] * 4 + [
            pl.BlockSpec((4, W_GROUP, D_MODEL), lambda i: (0, 0, 0)),
            pl.BlockSpec((tm, D_MODEL), lambda i: (i, 0)),
            pl.BlockSpec((1, D_MODEL), lambda i: (0, 0)),
        ],
        out_specs=pl.BlockSpec((tm, D_MODEL), lambda i: (i, 0)),
        out_shape=jax.ShapeDtypeStruct((T, D_MODEL), F32),
        compiler_params=_cparams(("arbitrary",)),
        name="outproj",
    )(*ys, w_out4, x2d, final_w)


def _rope_table(pos, rows):
    half = ROPE_DIM // 2
    inv = ROPE_THETA ** (-jnp.arange(half, dtype=F32) / half)
    ang = pos.astype(F32)[:, None] * inv[None, :]
    cos, sin = jnp.cos(ang), jnp.sin(ang)
    n = pos.shape[0]
    pad = jnp.zeros((n, DK_A - ROPE_DIM), F32)
    c64 = jnp.concatenate([cos, cos, pad + 1.0], axis=1)
    s1 = jnp.concatenate([-sin, jnp.zeros_like(sin), pad], axis=1)
    s2 = jnp.concatenate([jnp.zeros_like(sin), sin, pad], axis=1)
    tab = jnp.stack([jnp.tile(t, (1, LANES // DK_A)) for t in (c64, s1, s2)])
    return jnp.tile(tab, (1, rows // n, 1))


def _block_diag(w):
    nb, bw, _ = w.shape
    eye = jnp.eye(nb, dtype=w.dtype)
    return (eye[:, None, :, None] * w[:, :, None, :]).reshape(nb * bw, nb * bw)


def _prep_layer(l, norm_w, w_in, w_out, lam_q1, lam_k1, lam_q2, lam_k2, subln_w, ssd_conv_w,
                ssd_conv_b, ssd_dt_bias, ssd_a_log, ssd_d, ssd_norm_w, gla_gk_w2, gla_gk_b,
                gla_norm_w, lru_conv_w, lru_conv_b, lru_wa, lru_ba, lru_wx, lru_bx, lru_lambda):
    w = w_in[l]
    cols = lambda a, n: w[:, a:a + n]
    w_main = jnp.stack([
        cols(_QA, 512), cols(_KA, 512), cols(_VA, 512), cols(_XBC, 512), cols(_XBC + 512, 512),
        cols(_ZA, 512), cols(_ZB, 512), jnp.concatenate([cols(_QC, 256), cols(_KC, 256)], axis=1),
        cols(_VC, 512), cols(_GC, 512), cols(_XD, 512), cols(_ZD, 512)]).astype(BF16)
    w_small = jnp.concatenate([cols(_DT, H_B), cols(_GKLR, GK_RANK),
                               jnp.zeros((D_MODEL, LANES - H_B - GK_RANK), F32)], axis=1).astype(BF16)
    lam_init = 0.8 - 0.6 * math.exp(-0.3 * l)
    lam = (jnp.exp(jnp.sum(lam_q1[l] * lam_k1[l])) - jnp.exp(jnp.sum(lam_q2[l] * lam_k2[l]))
           + lam_init).reshape(1, 1).astype(F32)
    pad_lanes = lambda v: jnp.concatenate([v, jnp.zeros((LANES - v.shape[0],), F32)]).reshape(1, LANES)
    w2p = jnp.zeros((LANES, H_C * DK_C), F32).at[H_B:H_B + GK_RANK].set(gla_gk_w2[l]).astype(BF16)
    return dict(
        norm_w=norm_w[l].reshape(1, D_MODEL), w_main=w_main, w_small=w_small,
        w_out4=w_out[l].reshape(4, W_GROUP, D_MODEL).astype(BF16),
        lam=lam, lam_init=lam_init, subln_w=subln_w[l].reshape(1, DV_A),
        ssd_cw=ssd_conv_w[l], ssd_cb=ssd_conv_b[l].reshape(1, CONV_B_CH),
        ssd_dtb=pad_lanes(ssd_dt_bias[l]), ssd_alog=pad_lanes(ssd_a_log[l]),
        ssd_dvec=jnp.repeat(ssd_d[l], P_B).reshape(1, W_GROUP), ssd_nw=ssd_norm_w[l].reshape(1, W_GROUP),
        gla_w2p=w2p, gla_gb=gla_gk_b[l].reshape(1, H_C * DK_C), gla_nw=gla_norm_w[l].reshape(1, DV_C),
        lru_cw=lru_conv_w[l], lru_cb=lru_conv_b[l].reshape(1, W_D),
        lru_wa=_block_diag(lru_wa[l]).astype(BF16), lru_ba=lru_ba[l].reshape(1, W_D),
        lru_wx=_block_diag(lru_wx[l]).astype(BF16), lru_bx=lru_bx[l].reshape(1, W_D),
        lru_lam=lru_lambda[l].reshape(1, W_D),
    )


def _ssd_args(rest, small, states, p):
    return (rest, small, states[1], states[0], p["ssd_cw"], p["ssd_cb"], p["ssd_dtb"], p["ssd_alog"],
            p["ssd_dvec"], p["ssd_nw"])


def _mixers(rest, small, states, p, B, L, ydt, with_ssd=True):
    _, _, gla_s0, lru_h0, lru_buf = states
    out = {}
    if with_ssd:
        out["yb"], out["ssd_s"], out["ssd_c"] = _ssd(*_ssd_args(rest, small, states, p), B, L, ydt)
    yc, gla_st = _gla(rest, small, jnp.swapaxes(gla_s0, -1, -2), p["gla_w2p"], p["gla_gb"],
                      p["gla_nw"], B, L, ydt)
    yd, lru_h, lru_c = _lru(rest, lru_buf, lru_h0.reshape(B, 1, W_D), p["lru_cw"], p["lru_cb"],
                            p["lru_wa"], p["lru_ba"], p["lru_wx"], p["lru_bx"], p["lru_lam"], B, L, ydt)
    out.update(yc=yc, yd=yd, gla_s=jnp.swapaxes(gla_st, -1, -2), lru_h=lru_h.reshape(B, W_D),
               lru_c=lru_c)
    return out


def _finish_layer(x2d, ya, m, p, T, final_w, final):
    ys = (ya, m["yb"].reshape(T, W_GROUP), m["yc"].reshape(T, W_GROUP), m["yd"].reshape(T, W_D))
    x_new = _outproj(ys, p["w_out4"], x2d, final_w, final)
    return x_new, (m["ssd_s"], m["ssd_c"], m["gla_s"], m["lru_h"], m["lru_c"])


def _layer_pair(hp, hs, Bp, Lp, Bs, Ls, tab_p, tab_s, p, st_p, st_s, past, layer, depth, kv_p, kv_s,
                final_w, final):
    cache_k, cache_v, page_table = past
    ydt_p = BF16 if Lp % 16 == 0 else F32
    ydt_s = BF16 if Ls % 16 == 0 else F32
    q_p, k_p, v_p, rest_p, small_p, kf_p, vf_p = _inproj(
        hp, p["norm_w"], p["w_main"], p["w_small"], tab_p, Lp, layer, depth, kv_p)
    q_s, k_s, v_s, rest_s, small_s, kf_s, vf_s = _inproj(
        hs, p["norm_w"], p["w_main"], p["w_small"], tab_s, Ls, layer, depth, kv_s)
    yb_p, ssd_s_p, ssd_c_p, ya_s = _ssd_decode(
        _ssd_args(rest_p, small_p, st_p, p), Bp, Lp, ydt_p, p["lam"], q_s, k_s, v_s, rest_s,
        p["subln_w"], cache_k, cache_v, page_table, layer, Bs, Ls, p["lam_init"])
    ya_p = _attn_prompt(p["lam"], q_p, k_p, v_p, rest_p, p["subln_w"], Bp, Lp, p["lam_init"])
    m_p = _mixers(rest_p, small_p, st_p, p, Bp, Lp, ydt_p, with_ssd=False)
    m_p.update(yb=yb_p, ssd_s=ssd_s_p, ssd_c=ssd_c_p)
    m_s = _mixers(rest_s, small_s, st_s, p, Bs, Ls, ydt_s)
    hp, new_p = _finish_layer(hp, ya_p, m_p, p, Bp * Lp, final_w, final)
    hs, new_s = _finish_layer(hs, ya_s, m_s, p, Bs * Ls, final_w, final)
    return hp, hs, new_p, new_s, (kf_p, vf_p), (kf_s, vf_s)


def kernel(x_prompt, x_sample, cache_k, cache_v, page_table, state_ssd, state_ssd_conv, state_gla, state_lru, state_lru_conv, norm_w, w_in, w_out, lam_q1, lam_k1, lam_q2, lam_k2, subln_w, ssd_conv_w, ssd_conv_b, ssd_dt_bias, ssd_a_log, ssd_d, ssd_norm_w, gla_gk_w2, gla_gk_b, gla_norm_w, lru_conv_w, lru_conv_b, lru_wa, lru_ba, lru_wx, lru_bx, lru_lambda, final_norm_w):
    Bp, Lp, _ = x_prompt.shape
    Bs, Ls, _ = x_sample.shape
    depth = w_in.shape[0]
    past_len = page_table.shape[1] * cache_k.shape[2]
    tab_p = _rope_table(jnp.arange(Lp, dtype=jnp.int32), max(Lp, min(INPROJ_ROWS, Bp * Lp)))
    tab_s = _rope_table(past_len + jnp.arange(Ls, dtype=jnp.int32), max(Ls, min(INPROJ_ROWS, Bs * Ls)))
    final_w = final_norm_w.reshape(1, D_MODEL)
    hp = x_prompt.reshape(Bp * Lp, D_MODEL)
    hs = x_sample.reshape(Bs * Ls, D_MODEL)
    sp, ss = [], []
    kv_p = kv_s = None
    for l in range(depth):
        p = _prep_layer(l, norm_w, w_in, w_out, lam_q1, lam_k1, lam_q2, lam_k2, subln_w, ssd_conv_w,
                        ssd_conv_b, ssd_dt_bias, ssd_a_log, ssd_d, ssd_norm_w, gla_gk_w2, gla_gk_b,
                        gla_norm_w, lru_conv_w, lru_conv_b, lru_wa, lru_ba, lru_wx, lru_bx, lru_lambda)
        final = l == depth - 1
        st_p = (jnp.zeros((Bp, H_B, P_B, N_B), F32), jnp.zeros((Bp, CONV_W - 1, CONV_B_CH), F32),
                jnp.zeros((Bp, H_C, DK_C, DV_C), F32), jnp.zeros((Bp, W_D), F32),
                jnp.zeros((Bp, CONV_W - 1, W_D), F32))
        st_s = (state_ssd[l], state_ssd_conv[l], state_gla[l], state_lru[l], state_lru_conv[l])
        hp, hs, new_p, new_s, kv_p, kv_s = _layer_pair(
            hp, hs, Bp, Lp, Bs, Ls, tab_p, tab_s, p, st_p, st_s, (cache_k, cache_v, page_table), l,
            depth, kv_p, kv_s, final_w, final)
        sp.append(new_p)
        ss.append(new_s)
    outs_p = [x.reshape(depth, Bp, Lp, H_A, LANES) for x in kv_p]
    outs_p += [jnp.stack([s[i] for s in sp]) for i in range(5)]
    outs_s = [x.reshape(depth, Bs, Ls, H_A, LANES) for x in kv_s]
    outs_s += [jnp.stack([s[i] for s in ss]) for i in range(5)]
    return (hp.reshape(Bp, Lp, D_MODEL), hs.reshape(Bs, Ls, D_MODEL), *outs_p, *outs_s)
```

```python
import functools
import math

import jax
import jax.numpy as jnp
from jax import lax
from jax.experimental import pallas as pl
from jax.experimental.pallas import tpu as pltpu

F32 = jnp.float32
BF16 = jnp.bfloat16
EPS = 1e-6
NEG = -1e30
LOG2E = math.log2(math.e)

D_MODEL = 2048
W_GROUP = 512
H_A, DV_A, DK_A = 4, 128, 64
ROPE_DIM, ROPE_THETA = 16, 500000.0
H_B, P_B, N_B, G_B = 8, 64, 128, 2
CONV_W = 4
CONV_B_CH = W_GROUP + 2 * G_B * N_B
H_C, DK_C, DV_C = 4, 64, 128
GK_RANK, GK_NORM = 16, 16.0
W_D, NB_D, BW_D = 512, 8, 64
LRU_C = 8.0

LANES = 128
SUBLANES = 8
COL_BLK = 512
INPROJ_ROWS = 1024
ATTN_BQ = 512
ATTN_BK = 512
MIXER_SEQS = 4
VMEM_LIMIT = 56 * 1024 * 1024

_SPLITS = (512, 512, 512, 512, CONV_B_CH, H_B, W_GROUP, 256, 256, 512, GK_RANK, W_GROUP, W_D, W_D)
_OFF = [0]
for _s in _SPLITS:
    _OFF.append(_OFF[-1] + _s)
(_QA, _KA, _VA, _ZA, _XBC, _DT, _ZB, _QC, _KC, _VC, _GKLR, _GC, _XD, _ZD) = _OFF[:-1]

REST_COLS = 4608


def _cparams(sem):
    return pltpu.CompilerParams(dimension_semantics=sem, vmem_limit_bytes=VMEM_LIMIT)


def _sigmoid(x):
    return jax.nn.sigmoid(x)


def _silu(x):
    return x * _sigmoid(x)


def _softplus(x):
    return jnp.maximum(x, 0.0) + jnp.log1p(jnp.exp(-jnp.abs(x)))


def _rmsnorm_rows(x, w):
    return x * lax.rsqrt(jnp.mean(x * x, axis=-1, keepdims=True) + EPS) * w


def _dot(a, b):
    return jnp.dot(a, b, preferred_element_type=F32)


def _dot_nt(a, b):
    return lax.dot_general(a, b, (((1,), (1,)), ((), ())), preferred_element_type=F32)


def _dot_tn(a, b):
    return lax.dot_general(a, b, (((0,), (0,)), ((), ())), preferred_element_type=F32)


def _dot_exact(a, b):
    return jnp.dot(a, b, preferred_element_type=F32, precision=lax.Precision.HIGHEST)


def _lower_tri(n):
    r = lax.broadcasted_iota(jnp.int32, (n, n), 0)
    c = lax.broadcasted_iota(jnp.int32, (n, n), 1)
    return r >= c


def _batch_group(b, want):
    return math.gcd(b, want)


def _inproj_kernel(x_ref, nw_ref, w_ref, ws_ref, rope_ref, kf_in_ref, vf_in_ref,
                   q_ref, k_ref, v_ref, rest_ref, small_ref, kf_ref, vf_ref, u_scr):
    del kf_in_ref, vf_in_ref
    j = pl.program_id(1)
    tm = x_ref.shape[0]

    @pl.when(j == 0)
    def _():
        u = _rmsnorm_rows(x_ref[...], nw_ref[...]).astype(BF16)
        u_scr[...] = u
        small_ref[...] = _dot(u, ws_ref[...])

    acc = _dot(u_scr[...], w_ref[...])

    def rope_store(dst, scale, dst_heads=None):
        c, s1, s2 = rope_ref[0], rope_ref[1], rope_ref[2]
        for hh in range(COL_BLK // LANES):
            sl = slice(hh * LANES, (hh + 1) * LANES)
            a = acc[:, sl]
            r = (a * c + pltpu.roll(a, LANES - ROPE_DIM // 2, 1) * s1
                 + pltpu.roll(a, ROPE_DIM // 2, 1) * s2)
            dst[:, sl] = r * scale
            if dst_heads is not None:
                dst_heads[pl.ds(hh, tm, stride=H_A), :] = r

    @pl.when(j == 0)
    def _():
        rope_store(q_ref, DK_A ** -0.5 * LOG2E)

    @pl.when(j == 1)
    def _():
        rope_store(k_ref, 1.0, kf_ref)

    @pl.when(j == 2)
    def _():
        v_ref[...] = acc
        for hh in range(H_A):
            vf_ref[pl.ds(hh, tm, stride=H_A), :] = acc[:, hh * LANES:(hh + 1) * LANES]

    @pl.when(j >= 3)
    def _():
        rest_ref[...] = acc


def _inproj(x2d, norm_w, w_main, w_small, rope_tab, seq_len, layer, depth, kv_prev):
    T = x2d.shape[0]
    tm = min(INPROJ_ROWS, T)
    assert T % tm == 0 and (seq_len % tm == 0 or tm % seq_len == 0)
    n_pos_blk = max(seq_len // tm, 1)
    nj = w_main.shape[0]
    kv_shape = jax.ShapeDtypeStruct((depth, T * H_A, LANES), F32)
    out_shape = (
        jax.ShapeDtypeStruct((T, COL_BLK), F32),
        jax.ShapeDtypeStruct((T, COL_BLK), F32),
        jax.ShapeDtypeStruct((T, COL_BLK), F32),
        jax.ShapeDtypeStruct((T, REST_COLS), F32),
        jax.ShapeDtypeStruct((T, LANES), F32),
        kv_shape, kv_shape,
    )
    row_blk = pl.BlockSpec((tm, COL_BLK), lambda i, j: (i, 0))
    kv_blk = pl.BlockSpec((None, tm * H_A, LANES), lambda i, j: (layer, i, 0))
    in_specs = [
        pl.BlockSpec((tm, D_MODEL), lambda i, j: (i, 0)),
        pl.BlockSpec((1, D_MODEL), lambda i, j: (0, 0)),
        pl.BlockSpec((None, D_MODEL, COL_BLK), lambda i, j: (j, 0, 0)),
        pl.BlockSpec((D_MODEL, LANES), lambda i, j: (0, 0)),
        pl.BlockSpec((3, tm, LANES), lambda i, j: (0, i % n_pos_blk, 0)),
    ]
    in_specs += [pl.BlockSpec(memory_space=pl.ANY)] * 2
    args = (x2d, norm_w, w_main, w_small, rope_tab, *kv_prev)
    aliases = {len(args) - 2: 5, len(args) - 1: 6}
    return pl.pallas_call(
        _inproj_kernel,
        grid=(T // tm, nj),
        in_specs=in_specs,
        out_specs=(
            row_blk, row_blk, row_blk,
            pl.BlockSpec((tm, COL_BLK), lambda i, j: (i, jnp.maximum(j - 3, 0))),
            pl.BlockSpec((tm, LANES), lambda i, j: (i, 0)),
            kv_blk, kv_blk,
        ),
        out_shape=out_shape,
        input_output_aliases=aliases,
        scratch_shapes=[pltpu.VMEM((tm, D_MODEL), BF16)],
        compiler_params=_cparams(("arbitrary", "arbitrary")),
        name="inproj",
    )(*args)


def _attn_finish(o, lam_init, w, z):
    o = _rmsnorm_rows(o, w) * (1.0 - lam_init)
    return o * _silu(z)


def _attn_kernel(lam_ref, q_ref, k_ref, v_ref, z_ref, w_ref, o_ref, kb_scr, vt_scr,
                 m_scr, l_scr, acc_scr, *, bq, bk, lam_init):
    qi = pl.program_id(2)
    n_kb = kb_scr.shape[0]
    diag_blocks = bq // bk

    @pl.when(qi == 0)
    def _():
        for c in range(n_kb):
            rows = slice(c * bk, (c + 1) * bk)
            kb_scr[c] = k_ref[rows, :].astype(BF16)
            vt_scr[c] = v_ref[rows, :].T.astype(BF16)

    qt = q_ref[...].T
    feat = lax.broadcasted_iota(jnp.int32, qt.shape, 0)
    qts = (jnp.where(feat < DK_A, qt, 0.0).astype(BF16),
           jnp.where(feat >= DK_A, qt, 0.0).astype(BF16))
    m_scr[...] = jnp.full(m_scr.shape, NEG, F32)
    l_scr[...] = jnp.zeros(l_scr.shape, F32)
    acc_scr[...] = jnp.zeros(acc_scr.shape, F32)

    halves = 2
    hw = bq // halves

    def step(j, mask):
        kb = kb_scr[j]
        vtb = vt_scr[j]
        chains = [(m, hf) for hf in range(halves) for m in range(2)]
        ss = [_dot(kb, qts[m][:, hf * hw:(hf + 1) * hw]) for m, hf in chains]
        for (m, hf), s in zip(chains, ss):
            cols = slice(hf * hw, (hf + 1) * hw)
            if mask is not None:
                s = jnp.where(mask[:, cols], s, NEG)
            mx = m_scr[m, :, cols]
            mn = jnp.maximum(mx, jnp.max(s, axis=0, keepdims=True))
            alpha = jnp.exp2(mx - mn)
            p = jnp.exp2(s - mn)
            l_scr[m, :, cols] = alpha * l_scr[m, :, cols] + jnp.sum(p, axis=0, keepdims=True)
            acc_scr[m, :, cols] = alpha * acc_scr[m, :, cols] + _dot(vtb, p.astype(BF16))
            m_scr[m, :, cols] = mn

    def unmasked(j, carry):
        step(j, None)
        return carry

    lax.fori_loop(0, qi * diag_blocks, unmasked, 0)
    key_i = lax.broadcasted_iota(jnp.int32, (bk, bq), 0)
    qry_i = lax.broadcasted_iota(jnp.int32, (bk, bq), 1)
    for d in range(diag_blocks):
        step(qi * diag_blocks + d, key_i + d * bk <= qry_i)
    lam = lam_ref[0, 0]
    o = (acc_scr[0] / l_scr[0] - lam * (acc_scr[1] / l_scr[1])).T
    o_ref[...] = _attn_finish(o, lam_init, w_ref[...], z_ref[...]).astype(o_ref.dtype)


def _attn_prompt(lam, q, k, v, rest, subln_w, B, L, lam_init):
    bq = min(ATTN_BQ, L)
    bk = min(ATTN_BK, bq)
    nq = L // bq
    kern = functools.partial(_attn_kernel, bq=bq, bk=bk, lam_init=lam_init)
    return pl.pallas_call(
        kern,
        grid=(B, H_A, nq),
        in_specs=[
            pl.BlockSpec(memory_space=pltpu.SMEM),
            pl.BlockSpec((bq, LANES), lambda b, h, i: (b * nq + i, h)),
            pl.BlockSpec((L, LANES), lambda b, h, i: (b, h)),
            pl.BlockSpec((L, LANES), lambda b, h, i: (b, h)),
            pl.BlockSpec((bq, LANES), lambda b, h, i: (b * nq + i, 8 + h)),
            pl.BlockSpec((1, LANES), lambda b, h, i: (0, 0)),
        ],
        out_specs=pl.BlockSpec((bq, LANES), lambda b, h, i: (b * nq + i, h)),
        out_shape=jax.ShapeDtypeStruct((B * L, W_GROUP), BF16),
        scratch_shapes=[
            pltpu.VMEM((L // bk, bk, LANES), BF16),
            pltpu.VMEM((L // bk, DV_A, bk), BF16),
            pltpu.VMEM((2, 1, bq), F32),
            pltpu.VMEM((2, 1, bq), F32),
            pltpu.VMEM((2, DV_A, bq), F32),
        ],
        compiler_params=_cparams(("arbitrary", "arbitrary", "arbitrary")),
        name="attn_prompt",
    )(lam, q, k, v, rest, subln_w)


def _decode_init(q_ref, qrow_scr, m_scr, l_scr, acc_scr, ls):
    rows = 2 * ls
    r = lax.broadcasted_iota(jnp.int32, (rows, LANES), 0)
    c = lax.broadcasted_iota(jnp.int32, (rows, LANES), 1)
    own_map = (r // ls) == (c // DK_A)
    for h in range(H_A):
        qh = q_ref[:, h * LANES:(h + 1) * LANES]
        qrow_scr[h] = jnp.where(own_map, jnp.concatenate([qh, qh], axis=0), 0.0).astype(BF16)
    m_scr[...] = jnp.full(m_scr.shape, NEG, F32)
    l_scr[...] = jnp.zeros(l_scr.shape, F32)
    acc_scr[...] = jnp.zeros(acc_scr.shape, F32)


def _decode_head_rows(pages, h, page):
    head_rows = pl.ds(h, page, stride=H_A)
    return jnp.concatenate([pg[head_rows, :] for pg in pages], axis=0).astype(BF16)


def _decode_accumulate(h, s, vv, m_scr, l_scr, acc_scr):
    mx = m_scr[h]
    mn = jnp.maximum(mx, jnp.max(s, axis=-1, keepdims=True))
    alpha = jnp.exp2(mx - mn)
    pr = jnp.exp2(s - mn)
    l_scr[h] = alpha * l_scr[h] + jnp.sum(pr, axis=-1, keepdims=True)
    acc_scr[h] = alpha * acc_scr[h] + _dot(pr.astype(BF16), vv)
    m_scr[h] = mn


def _decode_finish(lam_ref, kn_ref, vn_ref, z_ref, w_ref, o_ref, qrow_scr, m_scr, l_scr, acc_scr,
                   ls, lam_init):
    rows = 2 * ls
    lam = lam_ref[0, 0]
    w = w_ref[...]
    jj = lax.broadcasted_iota(jnp.int32, (rows, ls), 1)
    qq = lax.broadcasted_iota(jnp.int32, (rows, ls), 0) % ls
    for h in range(H_A):
        cs = slice(h * LANES, (h + 1) * LANES)
        s = _dot_nt(qrow_scr[h], kn_ref[:, cs].astype(BF16))
        _decode_accumulate(h, jnp.where(jj <= qq, s, NEG), vn_ref[:, cs].astype(BF16),
                           m_scr, l_scr, acc_scr)
        o = acc_scr[h] / l_scr[h]
        o = o[:ls] - lam * o[ls:]
        o_ref[:, cs] = _attn_finish(o, lam_init, w, z_ref[:, cs]).astype(o_ref.dtype)


CONV_PAD = SUBLANES
CONV_TAIL = CONV_PAD - (CONV_W - 1)


def _conv_step(x_ref, cw_ref, cb_ref, xp_scr, q):
    xp_scr[CONV_PAD:CONV_PAD + q, :] = x_ref[...]
    y = xp_scr[CONV_TAIL:CONV_TAIL + q, :] * cw_ref[0:1, :]
    for t in range(1, CONV_W):
        y = y + xp_scr[CONV_TAIL + t:CONV_TAIL + t + q, :] * cw_ref[t:t + 1, :]
    y = y + cb_ref[...]
    xp_scr[CONV_TAIL:CONV_PAD, :] = xp_scr[q + CONV_TAIL:q + CONV_PAD, :]
    return y


def _ssd_inputs(xbc_ref, dt_ref, cw_ref, cb_ref, dtb_ref, xp_scr, q, nb):
    xbc = [_silu(_conv_step(xbc_ref.at[b], cw_ref, cb_ref, xp_scr.at[b], q)) for b in range(nb)]
    dt = [_softplus(dt_ref[b] + dtb_ref[...]) for b in range(nb)]
    return xbc, dt


def _ssd_chunk_pairs(xbc_all, dt_all, zb_ref, alog_ref, dvec_ref, nw_ref, y_ref, s_scr, y_scr, q, nb):
    half = LANES // 2
    row = lax.broadcasted_iota(jnp.int32, (q, LANES), 0)
    lane = lax.broadcasted_iota(jnp.int32, (q, LANES), 1)
    hi = lane >= half
    tri2 = row >= lane % half
    a = -jnp.exp(alog_ref[...])
    tri_f = _lower_tri(q).astype(F32)
    heads_per_group = H_B // G_B
    zeros_state = jnp.zeros((P_B, N_B), BF16)
    for b in range(nb):
        xbc, dt = xbc_all[b], dt_all[b]
        xs = xbc[:, :W_GROUP]
        cum = _dot_exact(tri_f, dt * a)
        cum_t = cum.T
        dt_t = dt.T
        last = cum[q - 1:q, :]
        ecum = jnp.exp(cum)
        wj = jnp.exp(last - cum) * dt
        elast = jnp.exp(last)
        for g in range(G_B):
            bgb = xbc[:, W_GROUP + g * N_B:W_GROUP + (g + 1) * N_B].astype(BF16)
            cg = xbc[:, W_GROUP + (G_B + g) * N_B:W_GROUP + (G_B + g + 1) * N_B]
            cb2 = _dot_nt(cg.astype(BF16), jnp.concatenate([bgb, bgb], axis=0))
            for pair in range(heads_per_group // 2):
                ha = g * heads_per_group + 2 * pair
                hb = ha + 1
                lanes = slice(ha * P_B, (hb + 1) * P_B)
                pick = lambda v: jnp.where(hi, v[:, hb:hb + 1], v[:, ha:ha + 1])
                cj = jnp.concatenate([cum_t[ha:ha + 1, :], cum_t[hb:hb + 1, :]], axis=1)
                dtj = jnp.concatenate([dt_t[ha:ha + 1, :], dt_t[hb:hb + 1, :]], axis=1)
                dec = jnp.exp(jnp.where(tri2, pick(cum) - cj, NEG))
                mh = (cb2 * dec * dtj).astype(BF16)
                xp = xs[:, lanes]
                x_bd = jnp.concatenate([jnp.where(hi, 0.0, xp), jnp.where(hi, xp, 0.0)],
                                       axis=0).astype(BF16)
                sa, sb = s_scr[b, ha], s_scr[b, hb]
                s_bd = jnp.concatenate(
                    [jnp.concatenate([sa.astype(BF16), zeros_state], axis=1),
                     jnp.concatenate([zeros_state, sb.astype(BF16)], axis=1)], axis=0)
                c_dec = jnp.concatenate([cg * ecum[:, ha:ha + 1], cg * ecum[:, hb:hb + 1]],
                                        axis=1).astype(BF16)
                y_scr[b, :, lanes] = _dot(mh, x_bd) + _dot_nt(c_dec, s_bd)
                upd = _dot_tn((xp * pick(wj)).astype(BF16), bgb)
                s_scr[b, ha] = elast[:, ha:ha + 1] * sa + upd[:P_B]
                s_scr[b, hb] = elast[:, hb:hb + 1] * sb + upd[P_B:]
        y = y_scr[b] + dvec_ref[...] * xs
        y_ref[b] = _rmsnorm_rows(y * _silu(zb_ref[b]), nw_ref[...]).astype(y_ref.dtype)


def _ssd_chunk(xbc_all, dt_all, zb_ref, alog_ref, dvec_ref, nw_ref, y_ref, s_scr, y_scr, q, nb):
    if 2 * q == LANES and 2 * P_B == LANES:
        return _ssd_chunk_pairs(xbc_all, dt_all, zb_ref, alog_ref, dvec_ref, nw_ref, y_ref, s_scr,
                                y_scr, q, nb)
    tri = _lower_tri(q)
    tri_f = tri.astype(F32)
    a = -jnp.exp(alog_ref[...])
    heads_per_group = H_B // G_B
    for b in range(nb):
        xbc, dt = xbc_all[b], dt_all[b]
        xs = xbc[:, :W_GROUP]
        cum = _dot_exact(tri_f, dt * a)
        cum_t = cum.T
        dt_t = dt.T
        last = cum[q - 1:q, :]
        ecum = jnp.exp(cum)
        wj = jnp.exp(last - cum) * dt
        elast = jnp.exp(last)
        for g in range(G_B):
            bg = xbc[:, W_GROUP + g * N_B:W_GROUP + (g + 1) * N_B]
            cg = xbc[:, W_GROUP + (G_B + g) * N_B:W_GROUP + (G_B + g + 1) * N_B]
            bgb = bg.astype(BF16)
            cb_g = _dot_nt(cg.astype(BF16), bgb)
            for hh in range(heads_per_group):
                h = g * heads_per_group + hh
                hs = slice(h * P_B, (h + 1) * P_B)
                seg = cum[:, h:h + 1] - cum_t[h:h + 1, :]
                dec = jnp.exp(jnp.where(tri, seg, NEG))
                mh = (cb_g * dec * dt_t[h:h + 1, :]).astype(BF16)
                xh = xs[:, hs]
                s_old = s_scr[b, h]
                yh = _dot(mh, xh.astype(BF16)) + _dot_nt((cg * ecum[:, h:h + 1]).astype(BF16),
                                                         s_old.astype(BF16))
                y_scr[b, :, hs] = yh
                xw = (xh * wj[:, h:h + 1]).astype(BF16)
                s_scr[b, h] = elast[:, h:h + 1] * s_old + _dot_tn(xw, bgb)
        y = y_scr[b] + dvec_ref[...] * xs
        y_ref[b] = _rmsnorm_rows(y * _silu(zb_ref[b]), nw_ref[...]).astype(y_ref.dtype)


def _ssd_kernel(xbc_ref, dt_ref, zb_ref, buf_ref, s0_ref, cw_ref, cb_ref, dtb_ref, alog_ref,
                dvec_ref, nw_ref, y_ref, sout_ref, cout_ref, xp_scr, s_scr, y_scr, *, q, nb):
    c = pl.program_id(1)

    @pl.when(c == 0)
    def _():
        s_scr[...] = s0_ref[...]
        xp_scr[:, CONV_TAIL:CONV_PAD, :] = buf_ref[...]

    xbc, dt = _ssd_inputs(xbc_ref, dt_ref, cw_ref, cb_ref, dtb_ref, xp_scr, q, nb)
    _ssd_chunk(xbc, dt, zb_ref, alog_ref, dvec_ref, nw_ref, y_ref, s_scr, y_scr, q, nb)

    @pl.when(c == pl.num_programs(1) - 1)
    def _():
        sout_ref[...] = s_scr[...]
        cout_ref[...] = xp_scr[:, CONV_TAIL:CONV_PAD, :]


def _ssd_decode_kernel(pt_ref, lam_ref, xbc_ref, dt_ref, zb_ref, buf_ref, s0_ref, cw_ref, cb_ref,
                       dtb_ref, alog_ref, dvec_ref, nw_ref, q_ref, kn_ref, vn_ref, z_ref, w_ref,
                       ck_ref, cv_ref, y_ref, sout_ref, cout_ref, o_ref,
                       xp_scr, s_scr, y_scr, qrow_scr, m_scr, l_scr, acc_scr, kbuf, vbuf, sem,
                       *, q, nb, n_pp, steps_per_seq, ls, page, layer, lam_init):
    c = pl.program_id(1)
    t = pl.program_id(0) * pl.num_programs(1) + c
    n_steps = pl.num_programs(0) * pl.num_programs(1)
    part = t % steps_per_seq
    slot = t % 2

    def page_copies(step, buf_slot):
        seq = step // steps_per_seq
        first = (step % steps_per_seq) * n_pp
        copies = []
        for i in range(n_pp):
            pg = pt_ref[seq, first + i]
            copies.append(pltpu.make_async_copy(ck_ref.at[layer, pg], kbuf.at[buf_slot, i],
                                                sem.at[buf_slot, 0]))
            copies.append(pltpu.make_async_copy(cv_ref.at[layer, pg], vbuf.at[buf_slot, i],
                                                sem.at[buf_slot, 1]))
        return copies

    @pl.when(t == 0)
    def _():
        for cp in page_copies(0, 0):
            cp.start()

    @pl.when(c == 0)
    def _():
        s_scr[...] = s0_ref[...]
        xp_scr[:, CONV_TAIL:CONV_PAD, :] = buf_ref[...]

    @pl.when(part == 0)
    def _():
        _decode_init(q_ref, qrow_scr, m_scr, l_scr, acc_scr, ls)

    for cp in page_copies(t, slot):
        cp.wait()
    nxt = jnp.minimum(t + 1, n_steps - 1)

    kp = [kbuf.at[slot, i] for i in range(n_pp)]
    vp = [vbuf.at[slot, i] for i in range(n_pp)]
    scores = [_dot_nt(qrow_scr[h], _decode_head_rows(kp, h, page)) for h in range(H_A)]
    for cp in page_copies(nxt, 1 - slot):
        cp.start()
    xbc, dt = _ssd_inputs(xbc_ref, dt_ref, cw_ref, cb_ref, dtb_ref, xp_scr, q, nb)
    for h in range(H_A):
        _decode_accumulate(h, scores[h], _decode_head_rows(vp, h, page), m_scr, l_scr, acc_scr)
    _ssd_chunk(xbc, dt, zb_ref, alog_ref, dvec_ref, nw_ref, y_ref, s_scr, y_scr, q, nb)

    @pl.when(c == pl.num_programs(1) - 1)
    def _():
        sout_ref[...] = s_scr[...]
        cout_ref[...] = xp_scr[:, CONV_TAIL:CONV_PAD, :]

    @pl.when(part == steps_per_seq - 1)
    def _():
        _decode_finish(lam_ref, kn_ref, vn_ref, z_ref, w_ref, o_ref, qrow_scr, m_scr, l_scr,
                       acc_scr, ls, lam_init)

    @pl.when(t == n_steps - 1)
    def _():
        for cp in page_copies(nxt, 1 - slot):
            cp.wait()


def _ssd_specs(B, L, nb, q, index):
    full = lambda shape: pl.BlockSpec(shape, index(lambda b, c: (0,) * len(shape)))
    in_specs = [
        pl.BlockSpec((nb, q, CONV_B_CH), index(lambda b, c: (b, c, 0))),
        pl.BlockSpec((nb, q, LANES), index(lambda b, c: (b, c, 0))),
        pl.BlockSpec((nb, q, W_GROUP), index(lambda b, c: (b, c, 3))),
        pl.BlockSpec((nb, CONV_W - 1, CONV_B_CH), index(lambda b, c: (b, 0, 0))),
        pl.BlockSpec((nb, H_B, P_B, N_B), index(lambda b, c: (b, 0, 0, 0))),
        full((CONV_W, CONV_B_CH)), full((1, CONV_B_CH)), full((1, LANES)), full((1, LANES)),
        full((1, W_GROUP)), full((1, W_GROUP)),
    ]
    out_specs = [
        pl.BlockSpec((nb, q, W_GROUP), index(lambda b, c: (b, c, 0))),
        pl.BlockSpec((nb, H_B, P_B, N_B), index(lambda b, c: (b, 0, 0, 0))),
        pl.BlockSpec((nb, CONV_W - 1, CONV_B_CH), index(lambda b, c: (b, 0, 0))),
    ]
    scratch = [
        pltpu.VMEM((nb, q + SUBLANES, CONV_B_CH), F32),
        pltpu.VMEM((nb, H_B, P_B, N_B), F32),
        pltpu.VMEM((nb, q, W_GROUP), F32),
    ]
    return in_specs, out_specs, scratch


def _ssd_out_shape(B, L, out_dtype):
    return [
        jax.ShapeDtypeStruct((B, L, W_GROUP), out_dtype),
        jax.ShapeDtypeStruct((B, H_B, P_B, N_B), F32),
        jax.ShapeDtypeStruct((B, CONV_W - 1, CONV_B_CH), F32),
    ]


def _ssd(rest, small, buf, s0, cw, cb, dtb, alog, dvec, nw, B, L, out_dtype):
    q = math.gcd(L, 64)
    nc = L // q
    nb = _batch_group(B, MIXER_SEQS)
    in_specs, out_specs, scratch = _ssd_specs(B, L, nb, q, lambda f: f)
    rest3 = rest.reshape(B, L, REST_COLS)
    return pl.pallas_call(
        functools.partial(_ssd_kernel, q=q, nb=nb),
        grid=(B // nb, nc),
        in_specs=in_specs,
        out_specs=tuple(out_specs),
        out_shape=tuple(_ssd_out_shape(B, L, out_dtype)),
        scratch_shapes=scratch,
        compiler_params=_cparams(("arbitrary", "arbitrary")),
        name="ssd",
    )(rest3, small.reshape(B, L, LANES), rest3, buf, s0, cw, cb, dtb, alog, dvec, nw)


def _ssd_decode(ssd_args, B, L, out_dtype, lam, q_s, k_s, v_s, rest_s, subln_w, cache_k, cache_v,
                page_table, layer, Bs, Ls, lam_init):
    rest, small, buf, s0, cw, cb, dtb, alog, dvec, nw = ssd_args
    q = math.gcd(L, 64)
    nc = L // q
    nb = _batch_group(B, MIXER_SEQS)
    steps = (B // nb) * nc
    n_pages = page_table.shape[1]
    page = cache_k.shape[2]
    assert (Bs * n_pages) % steps == 0, "decode pages must spread evenly over the SSD grid"
    n_pp = Bs * n_pages // steps
    assert n_pages % n_pp == 0
    steps_per_seq = n_pages // n_pp
    rows = 2 * Ls
    ck = cache_k.reshape(cache_k.shape[0], cache_k.shape[1], page * H_A, 2 * DK_A)
    cv = cache_v.reshape(cache_v.shape[0], cache_v.shape[1], page * H_A, DV_A)

    def step_of(b, c):
        return b * nc + c

    with_pt = lambda f: (lambda b, c, pt: f(b, c))
    ssd_in, ssd_out, ssd_scratch = _ssd_specs(B, L, nb, q, with_pt)
    tok = pl.BlockSpec((Ls, W_GROUP), lambda b, c, pt: (step_of(b, c) // steps_per_seq, 0))
    kern = functools.partial(_ssd_decode_kernel, q=q, nb=nb, n_pp=n_pp, steps_per_seq=steps_per_seq,
                             ls=Ls, page=page, layer=layer, lam_init=lam_init)
    grid_spec = pltpu.PrefetchScalarGridSpec(
        num_scalar_prefetch=1,
        grid=(B // nb, nc),
        in_specs=[pl.BlockSpec(memory_space=pltpu.SMEM)] + ssd_in + [
            tok, tok, tok,
            pl.BlockSpec((Ls, W_GROUP), lambda b, c, pt: (step_of(b, c) // steps_per_seq, 2)),
            pl.BlockSpec((1, LANES), lambda b, c, pt: (0, 0)),
            pl.BlockSpec(memory_space=pl.ANY),
            pl.BlockSpec(memory_space=pl.ANY),
        ],
        out_specs=tuple(ssd_out + [tok]),
        scratch_shapes=ssd_scratch + [
            pltpu.VMEM((H_A, rows, LANES), BF16),
            pltpu.VMEM((H_A, rows, 1), F32),
            pltpu.VMEM((H_A, rows, 1), F32),
            pltpu.VMEM((H_A, rows, DV_A), F32),
            pltpu.VMEM((2, n_pp, page * H_A, LANES), F32),
            pltpu.VMEM((2, n_pp, page * H_A, LANES), F32),
            pltpu.SemaphoreType.DMA((2, 2)),
        ],
    )
    rest3 = rest.reshape(B, L, REST_COLS)
    return pl.pallas_call(
        kern,
        grid_spec=grid_spec,
        out_shape=tuple(_ssd_out_shape(B, L, out_dtype)
                        + [jax.ShapeDtypeStruct((Bs * Ls, W_GROUP), F32)]),
        compiler_params=_cparams(("arbitrary", "arbitrary")),
        name="ssd_decode",
    )(page_table, lam, rest3, small.reshape(B, L, LANES), rest3, buf, s0, cw, cb, dtb, alog, dvec, nw,
      q_s, k_s, v_s, rest_s, subln_w, ck, cv)


def _gla_kernel(qk_ref, v_ref, gc_ref, sm_ref, s0_ref, w2_ref, gb_ref, nw_ref,
                y_ref, sout_ref, st_scr, *, q, nb):
    c = pl.program_id(1)

    @pl.when(c == 0)
    def _():
        st_scr[...] = s0_ref[...]

    seqs = range(nb)
    heads = range(H_C)
    tri = _lower_tri(q)
    width = H_C * DK_C
    nw = nw_ref[...]
    sm = sm_ref[...].reshape(nb * q, LANES).astype(BF16)
    gkl = _dot(sm, w2_ref[...]) + gb_ref[...]
    g = (jnp.minimum(gkl, 0.0) - jnp.log1p(jnp.exp(-jnp.abs(gkl)))) * (1.0 / GK_NORM)
    bc_all = _dot_exact(tri.astype(F32), jnp.concatenate([g[b * q:(b + 1) * q] for b in seqs], axis=1))
    bc = [bc_all[:, b * width:(b + 1) * width] for b in seqs]
    last = [x[q - 1:q, :] for x in bc]
    kc = [qk_ref[b, :, width:] for b in seqs]
    qe = [(qk_ref[b, :, :width] * (DK_C ** -0.5) * jnp.exp(bc[b])).astype(BF16) for b in seqs]
    ke = [(kc[b] * jnp.exp(-bc[b])).astype(BF16) for b in seqs]
    kl = [(kc[b] * jnp.exp(last[b] - bc[b])).astype(BF16) for b in seqs]
    el = [jnp.exp(x) for x in last]
    ks = [slice(h * DK_C, (h + 1) * DK_C) for h in heads]
    vs = [slice(h * DV_C, (h + 1) * DV_C) for h in heads]
    vh = [[v_ref[b, :, vs[h]].astype(BF16) for h in heads] for b in seqs]
    st_old = [[st_scr[b, h] for h in heads] for b in seqs]
    att = [[_dot_nt(qe[b][:, ks[h]], ke[b][:, ks[h]]) for h in heads] for b in seqs]
    o_state = [[_dot_nt(qe[b][:, ks[h]], st_old[b][h].astype(BF16)) for h in heads] for b in seqs]
    upd = [[_dot_tn(vh[b][h], kl[b][:, ks[h]]) for h in heads] for b in seqs]
    o = [[_dot(jnp.where(tri, att[b][h], 0.0).astype(BF16), vh[b][h]) + o_state[b][h]
          for h in heads] for b in seqs]
    for b in seqs:
        for h in heads:
            st_scr[b, h] = el[b][:, ks[h]] * st_old[b][h] + upd[b][h]
            y_ref[b, :, vs[h]] = (_rmsnorm_rows(o[b][h], nw)
                                  * _silu(gc_ref[b, :, vs[h]])).astype(y_ref.dtype)

    @pl.when(c == pl.num_programs(1) - 1)
    def _():
        sout_ref[...] = st_scr[...]


def _gla(rest, small, s0_t, w2p, gb, nw, B, L, out_dtype):
    q = min(L, 64)
    nc = L // q
    nb = _batch_group(B, MIXER_SEQS)
    kern = functools.partial(_gla_kernel, q=q, nb=nb)
    full = lambda shape: pl.BlockSpec(shape, lambda b, c: (0,) * len(shape))
    state = pl.BlockSpec((nb, H_C, DV_C, DK_C), lambda b, c: (b, 0, 0, 0))
    rest3 = rest.reshape(B, L, REST_COLS)
    return pl.pallas_call(
        kern,
        grid=(B // nb, nc),
        in_specs=[
            pl.BlockSpec((nb, q, W_GROUP), lambda b, c: (b, c, 4)),
            pl.BlockSpec((nb, q, W_GROUP), lambda b, c: (b, c, 5)),
            pl.BlockSpec((nb, q, W_GROUP), lambda b, c: (b, c, 6)),
            pl.BlockSpec((nb, q, LANES), lambda b, c: (b, c, 0)),
            state,
            full((LANES, H_C * DK_C)), full((1, H_C * DK_C)), full((1, DV_C)),
        ],
        out_specs=(pl.BlockSpec((nb, q, W_GROUP), lambda b, c: (b, c, 0)), state),
        out_shape=(
            jax.ShapeDtypeStruct((B, L, W_GROUP), out_dtype),
            jax.ShapeDtypeStruct((B, H_C, DV_C, DK_C), F32),
        ),
        scratch_shapes=[pltpu.VMEM((nb, H_C, DV_C, DK_C), F32)],
        compiler_params=_cparams(("arbitrary", "arbitrary")),
        name="gla",
    )(rest3, rest3, rest3, small.reshape(B, L, LANES), s0_t, w2p, gb, nw)


def _lru_kernel(xd_ref, zd_ref, buf_ref, h0_ref, cw_ref, cb_ref, wa_ref, ba_ref, wx_ref, bx_ref,
                lam_ref, y_ref, hout_ref, cout_ref, xp_scr, h_scr, a_scr, u_scr, *, q, nb):
    c = pl.program_id(1)

    @pl.when(c == 0)
    def _():
        h_scr[...] = h0_ref[...]
        xp_scr[:, CONV_TAIL:CONV_PAD, :] = buf_ref[...]

    sp_lam = _softplus(-lam_ref[...])
    for b in range(nb):
        xr = _conv_step(xd_ref.at[b], cw_ref, cb_ref, xp_scr.at[b], q)
        xb = xr.astype(BF16)
        r = _sigmoid(_dot(xb, wa_ref[...]) + ba_ref[...])
        i = _sigmoid(_dot(xb, wx_ref[...]) + bx_ref[...])
        log_a = -LRU_C * r * sp_lam
        a = jnp.exp(log_a)
        th = jnp.tanh(log_a)
        u = jnp.sqrt(-2.0 * th / (1.0 - th)) * (i * xr)
        a = a.reshape(q // SUBLANES, SUBLANES, W_D)
        u = u.reshape(q // SUBLANES, SUBLANES, W_D)
        r8 = lax.broadcasted_iota(jnp.int32, a.shape, 1)
        for s in (1, 2, 4):
            keep = r8 >= s
            a_sh = pltpu.roll(a, s, 1)
            u_sh = pltpu.roll(u, s, 1)
            u = jnp.where(keep, a * u_sh + u, u)
            a = jnp.where(keep, a * a_sh, a)
        a_scr[b] = a.reshape(q, W_D)
        u_scr[b] = u.reshape(q, W_D)
    carry = [h_scr[b] for b in range(nb)]
    for gi in range(q // SUBLANES):
        rows = slice(gi * SUBLANES, (gi + 1) * SUBLANES)
        for b in range(nb):
            hg = a_scr[b, rows, :] * carry[b] + u_scr[b, rows, :]
            y_ref[b, rows, :] = (hg * _silu(zd_ref[b, rows, :])).astype(y_ref.dtype)
            carry[b] = hg[SUBLANES - 1:SUBLANES, :]
    for b in range(nb):
        h_scr[b] = carry[b]

    @pl.when(c == pl.num_programs(1) - 1)
    def _():
        hout_ref[...] = h_scr[...]
        cout_ref[...] = xp_scr[:, CONV_TAIL:CONV_PAD, :]


def _lru(rest, buf, h0, cw, cb, wa, ba, wx, bx, lam, B, L, out_dtype):
    q = min(L, 256)
    nc = L // q
    nb = _batch_group(B, MIXER_SEQS)
    kern = functools.partial(_lru_kernel, q=q, nb=nb)
    full = lambda shape: pl.BlockSpec(shape, lambda b, c: (0,) * len(shape))
    rest3 = rest.reshape(B, L, REST_COLS)
    return pl.pallas_call(
        kern,
        grid=(B // nb, nc),
        in_specs=[
            pl.BlockSpec((nb, q, W_D), lambda b, c: (b, c, 7)),
            pl.BlockSpec((nb, q, W_D), lambda b, c: (b, c, 8)),
            pl.BlockSpec((nb, CONV_W - 1, W_D), lambda b, c: (b, 0, 0)),
            pl.BlockSpec((nb, 1, W_D), lambda b, c: (b, 0, 0)),
            full((CONV_W, W_D)), full((1, W_D)),
            full((W_D, W_D)), full((1, W_D)), full((W_D, W_D)), full((1, W_D)), full((1, W_D)),
        ],
        out_specs=(
            pl.BlockSpec((nb, q, W_D), lambda b, c: (b, c, 0)),
            pl.BlockSpec((nb, 1, W_D), lambda b, c: (b, 0, 0)),
            pl.BlockSpec((nb, CONV_W - 1, W_D), lambda b, c: (b, 0, 0)),
        ),
        out_shape=(
            jax.ShapeDtypeStruct((B, L, W_D), out_dtype),
            jax.ShapeDtypeStruct((B, 1, W_D), F32),
            jax.ShapeDtypeStruct((B, CONV_W - 1, W_D), F32),
        ),
        scratch_shapes=[
            pltpu.VMEM((nb, q + SUBLANES, W_D), F32),
            pltpu.VMEM((nb, 1, W_D), F32),
            pltpu.VMEM((nb, q, W_D), F32),
            pltpu.VMEM((nb, q, W_D), F32),
        ],
        compiler_params=_cparams(("arbitrary", "arbitrary")),
        name="lru",
    )(rest3, rest3, buf, h0, cw, cb, wa, ba, wx, bx, lam)


def _outproj_kernel(ya_ref, yb_ref, yc_ref, yd_ref, w_ref, x_ref, fw_ref, o_ref, *, final):
    out = _dot(ya_ref[...].astype(BF16), w_ref[0])
    for g, y_ref in enumerate((yb_ref, yc_ref, yd_ref), start=1):
        out = out + _dot(y_ref[...].astype(BF16), w_ref[g])
    res = x_ref[...] + out
    if final:
        res = _rmsnorm_rows(res, fw_ref[...])
    o_ref[...] = res


def _outproj(ys, w_out4, x2d, final_w, final):
    T = x2d.shape[0]
    tm = min(512, T)
    kern = functools.partial(_outproj_kernel, final=final)
    yspec = pl.BlockSpec((tm, W_GROUP), lambda i: (i, 0))
    return pl.pallas_call(
        kern,
        grid=(T // tm,),
        in_specs=[yspec] * 4 + [
            pl.BlockSpec((4, W_GROUP, D_MODEL), lambda i: (0, 0, 0)),
            pl.BlockSpec((tm, D_MODEL), lambda i: (i, 0)),
            pl.BlockSpec((1, D_MODEL), lambda i: (0, 0)),
        ],
        out_specs=pl.BlockSpec((tm, D_MODEL), lambda i: (i, 0)),
        out_shape=jax.ShapeDtypeStruct((T, D_MODEL), F32),
        compiler_params=_cparams(("arbitrary",)),
        name="outproj",
    )(*ys, w_out4, x2d, final_w)


def _rope_table(pos, rows):
    half = ROPE_DIM // 2
    inv = ROPE_THETA ** (-jnp.arange(half, dtype=F32) / half)
    ang = pos.astype(F32)[:, None] * inv[None, :]
    cos, sin = jnp.cos(ang), jnp.sin(ang)
    n = pos.shape[0]
    pad = jnp.zeros((n, DK_A - ROPE_DIM), F32)
    c64 = jnp.concatenate([cos, cos, pad + 1.0], axis=1)
    s1 = jnp.concatenate([-sin, jnp.zeros_like(sin), pad], axis=1)
    s2 = jnp.concatenate([jnp.zeros_like(sin), sin, pad], axis=1)
    tab = jnp.stack([jnp.tile(t, (1, LANES // DK_A)) for t in (c64, s1, s2)])
    return jnp.tile(tab, (1, rows // n, 1))


def _block_diag(w):
    nb, bw, _ = w.shape
    eye = jnp.eye(nb, dtype=w.dtype)
    return (eye[:, None, :, None] * w[:, :, None, :]).reshape(nb * bw, nb * bw)


def _prep_layer(l, norm_w, w_in, w_out, lam_q1, lam_k1, lam_q2, lam_k2, subln_w, ssd_conv_w,
                ssd_conv_b, ssd_dt_bias, ssd_a_log, ssd_d, ssd_norm_w, gla_gk_w2, gla_gk_b,
                gla_norm_w, lru_conv_w, lru_conv_b, lru_wa, lru_ba, lru_wx, lru_bx, lru_lambda):
    w = w_in[l]
    cols = lambda a, n: w[:, a:a + n]
    w_main = jnp.stack([
        cols(_QA, 512), cols(_KA, 512), cols(_VA, 512), cols(_XBC, 512), cols(_XBC + 512, 512),
        cols(_ZA, 512), cols(_ZB, 512), jnp.concatenate([cols(_QC, 256), cols(_KC, 256)], axis=1),
        cols(_VC, 512), cols(_GC, 512), cols(_XD, 512), cols(_ZD, 512)]).astype(BF16)
    w_small = jnp.concatenate([cols(_DT, H_B), cols(_GKLR, GK_RANK),
                               jnp.zeros((D_MODEL, LANES - H_B - GK_RANK), F32)], axis=1).astype(BF16)
    lam_init = 0.8 - 0.6 * math.exp(-0.3 * l)
    lam = (jnp.exp(jnp.sum(lam_q1[l] * lam_k1[l])) - jnp.exp(jnp.sum(lam_q2[l] * lam_k2[l]))
           + lam_init).reshape(1, 1).astype(F32)
    pad_lanes = lambda v: jnp.concatenate([v, jnp.zeros((LANES - v.shape[0],), F32)]).reshape(1, LANES)
    w2p = jnp.zeros((LANES, H_C * DK_C), F32).at[H_B:H_B + GK_RANK].set(gla_gk_w2[l]).astype(BF16)
    return dict(
        norm_w=norm_w[l].reshape(1, D_MODEL), w_main=w_main, w_small=w_small,
        w_out4=w_out[l].reshape(4, W_GROUP, D_MODEL).astype(BF16),
        lam=lam, lam_init=lam_init, subln_w=subln_w[l].reshape(1, DV_A),
        ssd_cw=ssd_conv_w[l], ssd_cb=ssd_conv_b[l].reshape(1, CONV_B_CH),
        ssd_dtb=pad_lanes(ssd_dt_bias[l]), ssd_alog=pad_lanes(ssd_a_log[l]),
        ssd_dvec=jnp.repeat(ssd_d[l], P_B).reshape(1, W_GROUP), ssd_nw=ssd_norm_w[l].reshape(1, W_GROUP),
        gla_w2p=w2p, gla_gb=gla_gk_b[l].reshape(1, H_C * DK_C), gla_nw=gla_norm_w[l].reshape(1, DV_C),
        lru_cw=lru_conv_w[l], lru_cb=lru_conv_b[l].reshape(1, W_D),
        lru_wa=_block_diag(lru_wa[l]).astype(BF16), lru_ba=lru_ba[l].reshape(1, W_D),
        lru_wx=_block_diag(lru_wx[l]).astype(BF16), lru_bx=lru_bx[l].reshape(1, W_D),
        lru_lam=lru_lambda[l].reshape(1, W_D),
    )


def _ssd_args(rest, small, states, p):
    return (rest, small, states[1], states[0], p["ssd_cw"], p["ssd_cb"], p["ssd_dtb"], p["ssd_alog"],
            p["ssd_dvec"], p["ssd_nw"])


def _mixers(rest, small, states, p, B, L, ydt, with_ssd=True):
    _, _, gla_s0, lru_h0, lru_buf = states
    out = {}
    if with_ssd:
        out["yb"], out["ssd_s"], out["ssd_c"] = _ssd(*_ssd_args(rest, small, states, p), B, L, ydt)
    yc, gla_st = _gla(rest, small, jnp.swapaxes(gla_s0, -1, -2), p["gla_w2p"], p["gla_gb"],
                      p["gla_nw"], B, L, ydt)
    yd, lru_h, lru_c = _lru(rest, lru_buf, lru_h0.reshape(B, 1, W_D), p["lru_cw"], p["lru_cb"],
                            p["lru_wa"], p["lru_ba"], p["lru_wx"], p["lru_bx"], p["lru_lam"], B, L, ydt)
    out.update(yc=yc, yd=yd, gla_s=jnp.swapaxes(gla_st, -1, -2), lru_h=lru_h.reshape(B, W_D),
               lru_c=lru_c)
    return out


def _finish_layer(x2d, ya, m, p, T, final_w, final):
    ys = (ya, m["yb"].reshape(T, W_GROUP), m["yc"].reshape(T, W_GROUP), m["yd"].reshape(T, W_D))
    x_new = _outproj(ys, p["w_out4"], x2d, final_w, final)
    return x_new, (m["ssd_s"], m["ssd_c"], m["gla_s"], m["lru_h"], m["lru_c"])


def _layer_pair(hp, hs, Bp, Lp, Bs, Ls, tab_p, tab_s, p, st_p, st_s, past, layer, depth, kv_p, kv_s,
                final_w, final):
    cache_k, cache_v, page_table = past
    ydt_p = BF16 if Lp % 16 == 0 else F32
    ydt_s = BF16 if Ls % 16 == 0 else F32
    q_p, k_p, v_p, rest_p, small_p, kf_p, vf_p = _inproj(
        hp, p["norm_w"], p["w_main"], p["w_small"], tab_p, Lp, layer, depth, kv_p)
    q_s, k_s, v_s, rest_s, small_s, kf_s, vf_s = _inproj(
        hs, p["norm_w"], p["w_main"], p["w_small"], tab_s, Ls, layer, depth, kv_s)
    yb_p, ssd_s_p, ssd_c_p, ya_s = _ssd_decode(
        _ssd_args(rest_p, small_p, st_p, p), Bp, Lp, ydt_p, p["lam"], q_s, k_s, v_s, rest_s,
        p["subln_w"], cache_k, cache_v, page_table, layer, Bs, Ls, p["lam_init"])
    ya_p = _attn_prompt(p["lam"], q_p, k_p, v_p, rest_p, p["subln_w"], Bp, Lp, p["lam_init"])
    m_p = _mixers(rest_p, small_p, st_p, p, Bp, Lp, ydt_p, with_ssd=False)
    m_p.update(yb=yb_p, ssd_s=ssd_s_p, ssd_c=ssd_c_p)
    m_s = _mixers(rest_s, small_s, st_s, p, Bs, Ls, ydt_s)
    hp, new_p = _finish_layer(hp, ya_p, m_p, p, Bp * Lp, final_w, final)
    hs, new_s = _finish_layer(hs, ya_s, m_s, p, Bs * Ls, final_w, final)
    return hp, hs, new_p, new_s, (kf_p, vf_p), (kf_s, vf_s)


def kernel(x_prompt, x_sample, cache_k, cache_v, page_table, state_ssd, state_ssd_conv, state_gla, state_lru, state_lru_conv, norm_w, w_in, w_out, lam_q1, lam_k1, lam_q2, lam_k2, subln_w, ssd_conv_w, ssd_conv_b, ssd_dt_bias, ssd_a_log, ssd_d, ssd_norm_w, gla_gk_w2, gla_gk_b, gla_norm_w, lru_conv_w, lru_conv_b, lru_wa, lru_ba, lru_wx, lru_bx, lru_lambda, final_norm_w):
    Bp, Lp, _ = x_prompt.shape
    Bs, Ls, _ = x_sample.shape
    depth = w_in.shape[0]
    past_len = page_table.shape[1] * cache_k.shape[2]
    tab_p = _rope_table(jnp.arange(Lp, dtype=jnp.int32), max(Lp, min(INPROJ_ROWS, Bp * Lp)))
    tab_s = _rope_table(past_len + jnp.arange(Ls, dtype=jnp.int32), max(Ls, min(INPROJ_ROWS, Bs * Ls)))
    final_w = final_norm_w.reshape(1, D_MODEL)
    hp = x_prompt.reshape(Bp * Lp, D_MODEL)
    hs = x_sample.reshape(Bs * Ls, D_MODEL)
    sp, ss = [], []
    kv_p = tuple(jnp.zeros((depth, Bp * Lp * H_A, LANES), F32) for _ in range(2))
    kv_s = tuple(jnp.zeros((depth, Bs * Ls * H_A, LANES), F32) for _ in range(2))
    for l in range(depth):
        p = _prep_layer(l, norm_w, w_in, w_out, lam_q1, lam_k1, lam_q2, lam_k2, subln_w, ssd_conv_w,
                        ssd_conv_b, ssd_dt_bias, ssd_a_log, ssd_d, ssd_norm_w, gla_gk_w2, gla_gk_b,
                        gla_norm_w, lru_conv_w, lru_conv_b, lru_wa, lru_ba, lru_wx, lru_bx, lru_lambda)
        final = l == depth - 1
        st_p = (jnp.zeros((Bp, H_B, P_B, N_B), F32), jnp.zeros((Bp, CONV_W - 1, CONV_B_CH), F32),
                jnp.zeros((Bp, H_C, DK_C, DV_C), F32), jnp.zeros((Bp, W_D), F32),
                jnp.zeros((Bp, CONV_W - 1, W_D), F32))
        st_s = (state_ssd[l], state_ssd_conv[l], state_gla[l], state_lru[l], state_lru_conv[l])
        hp, hs, new_p, new_s, kv_p, kv_s = _layer_pair(
            hp, hs, Bp, Lp, Bs, Ls, tab_p, tab_s, p, st_p, st_s, (cache_k, cache_v, page_table), l,
            depth, kv_p, kv_s, final_w, final)
        sp.append(new_p)
        ss.append(new_s)
    outs_p = [x.reshape(depth, Bp, Lp, H_A, LANES) for x in kv_p]
    outs_p += [jnp.stack([s[i] for s in sp]) for i in range(5)]
    outs_s = [x.reshape(depth, Bs, Ls, H_A, LANES) for x in kv_s]
    outs_s += [jnp.stack([s[i] for s in ss]) for i in range(5)]
    return (hp.reshape(Bp, Lp, D_MODEL), hs.reshape(Bs, Ls, D_MODEL), *outs_p, *outs_s)
```

```python
import functools
import math

import jax
import jax.numpy as jnp
from jax import lax
from jax.experimental import pallas as pl
from jax.experimental.pallas import tpu as pltpu

F32 = jnp.float32
BF16 = jnp.bfloat16
EPS = 1e-6
NEG = -1e30
LOG2E = math.log2(math.e)

D_MODEL = 2048
W_GROUP = 512
H_A, DV_A, DK_A = 4, 128, 64
ROPE_DIM, ROPE_THETA = 16, 500000.0
H_B, P_B, N_B, G_B = 8, 64, 128, 2
CONV_W = 4
CONV_B_CH = W_GROUP + 2 * G_B * N_B
H_C, DK_C, DV_C = 4, 64, 128
GK_RANK, GK_NORM = 16, 16.0
W_D, NB_D, BW_D = 512, 8, 64
LRU_C = 8.0

LANES = 128
SUBLANES = 8
COL_BLK = 512
INPROJ_ROWS = 1024
ATTN_BQ = 512
ATTN_BK = 512
MIXER_SEQS = 4
VMEM_LIMIT = 56 * 1024 * 1024

_SPLITS = (512, 512, 512, 512, CONV_B_CH, H_B, W_GROUP, 256, 256, 512, GK_RANK, W_GROUP, W_D, W_D)
_OFF = [0]
for _s in _SPLITS:
    _OFF.append(_OFF[-1] + _s)
(_QA, _KA, _VA, _ZA, _XBC, _DT, _ZB, _QC, _KC, _VC, _GKLR, _GC, _XD, _ZD) = _OFF[:-1]

REST_COLS = 4608


def _cparams(sem):
    return pltpu.CompilerParams(dimension_semantics=sem, vmem_limit_bytes=VMEM_LIMIT)


def _sigmoid(x):
    return jax.nn.sigmoid(x)


def _silu(x):
    return x * _sigmoid(x)


def _softplus(x):
    return jnp.maximum(x, 0.0) + jnp.log1p(jnp.exp(-jnp.abs(x)))


def _rmsnorm_rows(x, w):
    return x * lax.rsqrt(jnp.mean(x * x, axis=-1, keepdims=True) + EPS) * w


def _dot(a, b):
    return jnp.dot(a, b, preferred_element_type=F32)


def _dot_nt(a, b):
    return lax.dot_general(a, b, (((1,), (1,)), ((), ())), preferred_element_type=F32)


def _dot_tn(a, b):
    return lax.dot_general(a, b, (((0,), (0,)), ((), ())), preferred_element_type=F32)


def _dot_exact(a, b):
    return jnp.dot(a, b, preferred_element_type=F32, precision=lax.Precision.HIGHEST)


def _lower_tri(n):
    r = lax.broadcasted_iota(jnp.int32, (n, n), 0)
    c = lax.broadcasted_iota(jnp.int32, (n, n), 1)
    return r >= c


def _batch_group(b, want):
    return math.gcd(b, want)


def _inproj_kernel(x_ref, nw_ref, w_ref, ws_ref, rope_ref, kf_in_ref, vf_in_ref,
                   q_ref, k_ref, v_ref, rest_ref, small_ref, kf_ref, vf_ref, u_scr):
    del kf_in_ref, vf_in_ref
    j = pl.program_id(1)
    tm = x_ref.shape[0]

    @pl.when(j == 0)
    def _():
        u = _rmsnorm_rows(x_ref[...], nw_ref[...]).astype(BF16)
        u_scr[...] = u
        small_ref[...] = _dot(u, ws_ref[...])

    acc = _dot(u_scr[...], w_ref[...])

    def rope_store(dst, scale, dst_heads=None):
        c, s1, s2 = rope_ref[0], rope_ref[1], rope_ref[2]
        for hh in range(COL_BLK // LANES):
            sl = slice(hh * LANES, (hh + 1) * LANES)
            a = acc[:, sl]
            r = (a * c + pltpu.roll(a, LANES - ROPE_DIM // 2, 1) * s1
                 + pltpu.roll(a, ROPE_DIM // 2, 1) * s2)
            dst[:, sl] = r * scale
            if dst_heads is not None:
                dst_heads[pl.ds(hh, tm, stride=H_A), :] = r

    @pl.when(j == 0)
    def _():
        rope_store(q_ref, DK_A ** -0.5 * LOG2E)

    @pl.when(j == 1)
    def _():
        rope_store(k_ref, 1.0, kf_ref)

    @pl.when(j == 2)
    def _():
        v_ref[...] = acc
        for hh in range(H_A):
            vf_ref[pl.ds(hh, tm, stride=H_A), :] = acc[:, hh * LANES:(hh + 1) * LANES]

    @pl.when(j >= 3)
    def _():
        rest_ref[...] = acc


def _inproj(x2d, norm_w, w_main, w_small, rope_tab, seq_len, layer, depth, kv_prev):
    T = x2d.shape[0]
    tm = min(INPROJ_ROWS, T)
    assert T % tm == 0 and (seq_len % tm == 0 or tm % seq_len == 0)
    n_pos_blk = max(seq_len // tm, 1)
    nj = w_main.shape[0]
    kv_shape = jax.ShapeDtypeStruct((depth, T * H_A, LANES), F32)
    out_shape = (
        jax.ShapeDtypeStruct((T, COL_BLK), F32),
        jax.ShapeDtypeStruct((T, COL_BLK), F32),
        jax.ShapeDtypeStruct((T, COL_BLK), F32),
        jax.ShapeDtypeStruct((T, REST_COLS), F32),
        jax.ShapeDtypeStruct((T, LANES), F32),
        kv_shape, kv_shape,
    )
    row_blk = pl.BlockSpec((tm, COL_BLK), lambda i, j: (i, 0))
    kv_blk = pl.BlockSpec((None, tm * H_A, LANES), lambda i, j: (layer, i, 0))
    in_specs = [
        pl.BlockSpec((tm, D_MODEL), lambda i, j: (i, 0)),
        pl.BlockSpec((1, D_MODEL), lambda i, j: (0, 0)),
        pl.BlockSpec((None, D_MODEL, COL_BLK), lambda i, j: (j, 0, 0)),
        pl.BlockSpec((D_MODEL, LANES), lambda i, j: (0, 0)),
        pl.BlockSpec((3, tm, LANES), lambda i, j: (0, i % n_pos_blk, 0)),
    ]
    in_specs += [pl.BlockSpec(memory_space=pl.ANY)] * 2
    args = (x2d, norm_w, w_main, w_small, rope_tab, *kv_prev)
    aliases = {len(args) - 2: 5, len(args) - 1: 6}
    return pl.pallas_call(
        _inproj_kernel,
        grid=(T // tm, nj),
        in_specs=in_specs,
        out_specs=(
            row_blk, row_blk, row_blk,
            pl.BlockSpec((tm, COL_BLK), lambda i, j: (i, jnp.maximum(j - 3, 0))),
            pl.BlockSpec((tm, LANES), lambda i, j: (i, 0)),
            kv_blk, kv_blk,
        ),
        out_shape=out_shape,
        input_output_aliases=aliases,
        scratch_shapes=[pltpu.VMEM((tm, D_MODEL), BF16)],
        compiler_params=_cparams(("arbitrary", "arbitrary")),
        name="inproj",
    )(*args)


def _attn_finish(o, lam_init, w, z):
    o = _rmsnorm_rows(o, w) * (1.0 - lam_init)
    return o * _silu(z)


def _attn_kernel(lam_ref, q_ref, k_ref, v_ref, z_ref, w_ref, o_ref, kb_scr, vt_scr,
                 m_scr, l_scr, acc_scr, *, bq, bk, lam_init):
    qi = pl.program_id(2)
    n_kb = kb_scr.shape[0]
    diag_blocks = bq // bk

    @pl.when(qi == 0)
    def _():
        for c in range(n_kb):
            rows = slice(c * bk, (c + 1) * bk)
            kb_scr[c] = k_ref[rows, :].astype(BF16)
            vt_scr[c] = v_ref[rows, :].T.astype(BF16)

    qt = q_ref[...].T
    feat = lax.broadcasted_iota(jnp.int32, qt.shape, 0)
    qts = (jnp.where(feat < DK_A, qt, 0.0).astype(BF16),
           jnp.where(feat >= DK_A, qt, 0.0).astype(BF16))
    m_scr[...] = jnp.full(m_scr.shape, NEG, F32)
    l_scr[...] = jnp.zeros(l_scr.shape, F32)
    acc_scr[...] = jnp.zeros(acc_scr.shape, F32)

    halves = 2
    hw = bq // halves

    def step(j, mask):
        kb = kb_scr[j]
        vtb = vt_scr[j]
        chains = [(m, hf) for hf in range(halves) for m in range(2)]
        n_keys = [(hf + 1) * hw if (mask is not None and diag_blocks == 1) else bk
                  for _, hf in chains]
        ss = [_dot(kb[:nk], qts[m][:, hf * hw:(hf + 1) * hw])
              for (m, hf), nk in zip(chains, n_keys)]
        for (m, hf), nk, s in zip(chains, n_keys, ss):
            cols = slice(hf * hw, (hf + 1) * hw)
            if mask is not None:
                s = jnp.where(mask[:nk, cols], s, NEG)
            mx = m_scr[m, :, cols]
            mn = jnp.maximum(mx, jnp.max(s, axis=0, keepdims=True))
            alpha = jnp.exp2(mx - mn)
            p = jnp.exp2(s - mn)
            l_scr[m, :, cols] = alpha * l_scr[m, :, cols] + jnp.sum(p, axis=0, keepdims=True)
            acc_scr[m, :, cols] = alpha * acc_scr[m, :, cols] + _dot(vtb[:, :nk], p.astype(BF16))
            m_scr[m, :, cols] = mn

    def unmasked(j, carry):
        step(j, None)
        return carry

    lax.fori_loop(0, qi * diag_blocks, unmasked, 0)
    key_i = lax.broadcasted_iota(jnp.int32, (bk, bq), 0)
    qry_i = lax.broadcasted_iota(jnp.int32, (bk, bq), 1)
    for d in range(diag_blocks):
        step(qi * diag_blocks + d, key_i + d * bk <= qry_i)
    lam = lam_ref[0, 0]
    o = (acc_scr[0] / l_scr[0] - lam * (acc_scr[1] / l_scr[1])).T
    o_ref[...] = _attn_finish(o, lam_init, w_ref[...], z_ref[...]).astype(o_ref.dtype)


def _attn_prompt(lam, q, k, v, rest, subln_w, B, L, lam_init):
    bq = min(ATTN_BQ, L)
    bk = min(ATTN_BK, bq)
    nq = L // bq
    kern = functools.partial(_attn_kernel, bq=bq, bk=bk, lam_init=lam_init)
    return pl.pallas_call(
        kern,
        grid=(B, H_A, nq),
        in_specs=[
            pl.BlockSpec(memory_space=pltpu.SMEM),
            pl.BlockSpec((bq, LANES), lambda b, h, i: (b * nq + i, h)),
            pl.BlockSpec((L, LANES), lambda b, h, i: (b, h)),
            pl.BlockSpec((L, LANES), lambda b, h, i: (b, h)),
            pl.BlockSpec((bq, LANES), lambda b, h, i: (b * nq + i, 8 + h)),
            pl.BlockSpec((1, LANES), lambda b, h, i: (0, 0)),
        ],
        out_specs=pl.BlockSpec((bq, LANES), lambda b, h, i: (b * nq + i, h)),
        out_shape=jax.ShapeDtypeStruct((B * L, W_GROUP), BF16),
        scratch_shapes=[
            pltpu.VMEM((L // bk, bk, LANES), BF16),
            pltpu.VMEM((L // bk, DV_A, bk), BF16),
            pltpu.VMEM((2, 1, bq), F32),
            pltpu.VMEM((2, 1, bq), F32),
            pltpu.VMEM((2, DV_A, bq), F32),
        ],
        compiler_params=_cparams(("arbitrary", "arbitrary", "arbitrary")),
        name="attn_prompt",
    )(lam, q, k, v, rest, subln_w)


def _decode_init(q_ref, qrow_scr, m_scr, l_scr, acc_scr, ls):
    rows = 2 * ls
    r = lax.broadcasted_iota(jnp.int32, (rows, LANES), 0)
    c = lax.broadcasted_iota(jnp.int32, (rows, LANES), 1)
    own_map = (r // ls) == (c // DK_A)
    for h in range(H_A):
        qh = q_ref[:, h * LANES:(h + 1) * LANES]
        qrow_scr[h] = jnp.where(own_map, jnp.concatenate([qh, qh], axis=0), 0.0).astype(BF16)
    m_scr[...] = jnp.full(m_scr.shape, NEG, F32)
    l_scr[...] = jnp.zeros(l_scr.shape, F32)
    acc_scr[...] = jnp.zeros(acc_scr.shape, F32)


def _decode_head_rows(pages, h, page):
    head_rows = pl.ds(h, page, stride=H_A)
    return jnp.concatenate([pg[head_rows, :] for pg in pages], axis=0).astype(BF16)


def _decode_accumulate(h, s, vv, m_scr, l_scr, acc_scr):
    mx = m_scr[h]
    mn = jnp.maximum(mx, jnp.max(s, axis=-1, keepdims=True))
    alpha = jnp.exp2(mx - mn)
    pr = jnp.exp2(s - mn)
    l_scr[h] = alpha * l_scr[h] + jnp.sum(pr, axis=-1, keepdims=True)
    acc_scr[h] = alpha * acc_scr[h] + _dot(pr.astype(BF16), vv)
    m_scr[h] = mn


def _decode_finish(lam_ref, kn_ref, vn_ref, z_ref, w_ref, o_ref, qrow_scr, m_scr, l_scr, acc_scr,
                   ls, lam_init):
    rows = 2 * ls
    lam = lam_ref[0, 0]
    w = w_ref[...]
    jj = lax.broadcasted_iota(jnp.int32, (rows, ls), 1)
    qq = lax.broadcasted_iota(jnp.int32, (rows, ls), 0) % ls
    for h in range(H_A):
        cs = slice(h * LANES, (h + 1) * LANES)
        s = _dot_nt(qrow_scr[h], kn_ref[:, cs].astype(BF16))
        _decode_accumulate(h, jnp.where(jj <= qq, s, NEG), vn_ref[:, cs].astype(BF16),
                           m_scr, l_scr, acc_scr)
        o = acc_scr[h] / l_scr[h]
        o = o[:ls] - lam * o[ls:]
        o_ref[:, cs] = _attn_finish(o, lam_init, w, z_ref[:, cs]).astype(o_ref.dtype)


CONV_PAD = SUBLANES
CONV_TAIL = CONV_PAD - (CONV_W - 1)


def _conv_step(x_ref, cw_ref, cb_ref, xp_scr, q):
    xp_scr[CONV_PAD:CONV_PAD + q, :] = x_ref[...]
    y = xp_scr[CONV_TAIL:CONV_TAIL + q, :] * cw_ref[0:1, :]
    for t in range(1, CONV_W):
        y = y + xp_scr[CONV_TAIL + t:CONV_TAIL + t + q, :] * cw_ref[t:t + 1, :]
    y = y + cb_ref[...]
    xp_scr[CONV_TAIL:CONV_PAD, :] = xp_scr[q + CONV_TAIL:q + CONV_PAD, :]
    return y


def _ssd_inputs(xbc_ref, dt_ref, cw_ref, cb_ref, dtb_ref, xp_scr, q, nb):
    xbc = [_silu(_conv_step(xbc_ref.at[b], cw_ref, cb_ref, xp_scr.at[b], q)) for b in range(nb)]
    dt = [_softplus(dt_ref[b] + dtb_ref[...]) for b in range(nb)]
    return xbc, dt


def _ssd_chunk_pairs(xbc_all, dt_all, zb_ref, alog_ref, dvec_ref, nw_ref, y_ref, s_scr, y_scr, q, nb):
    half = LANES // 2
    row = lax.broadcasted_iota(jnp.int32, (q, LANES), 0)
    lane = lax.broadcasted_iota(jnp.int32, (q, LANES), 1)
    hi = lane >= half
    tri2 = row >= lane % half
    a = -jnp.exp(alog_ref[...])
    tri_f = _lower_tri(q).astype(F32)
    heads_per_group = H_B // G_B
    zeros_state = jnp.zeros((P_B, N_B), BF16)
    for b in range(nb):
        xbc, dt = xbc_all[b], dt_all[b]
        xs = xbc[:, :W_GROUP]
        cum = _dot_exact(tri_f, dt * a)
        cum_t = cum.T
        dt_t = dt.T
        last = cum[q - 1:q, :]
        ecum = jnp.exp(cum)
        wj = jnp.exp(last - cum) * dt
        elast = jnp.exp(last)
        for g in range(G_B):
            bgb = xbc[:, W_GROUP + g * N_B:W_GROUP + (g + 1) * N_B].astype(BF16)
            cg = xbc[:, W_GROUP + (G_B + g) * N_B:W_GROUP + (G_B + g + 1) * N_B]
            cb2 = _dot_nt(cg.astype(BF16), jnp.concatenate([bgb, bgb], axis=0))
            for pair in range(heads_per_group // 2):
                ha = g * heads_per_group + 2 * pair
                hb = ha + 1
                lanes = slice(ha * P_B, (hb + 1) * P_B)
                pick = lambda v: jnp.where(hi, v[:, hb:hb + 1], v[:, ha:ha + 1])
                cj = jnp.concatenate([cum_t[ha:ha + 1, :], cum_t[hb:hb + 1, :]], axis=1)
                dtj = jnp.concatenate([dt_t[ha:ha + 1, :], dt_t[hb:hb + 1, :]], axis=1)
                dec = jnp.exp(jnp.where(tri2, pick(cum) - cj, NEG))
                mh = (cb2 * dec * dtj).astype(BF16)
                xp = xs[:, lanes]
                x_bd = jnp.concatenate([jnp.where(hi, 0.0, xp), jnp.where(hi, xp, 0.0)],
                                       axis=0).astype(BF16)
                sa, sb = s_scr[b, ha], s_scr[b, hb]
                s_bd = jnp.concatenate(
                    [jnp.concatenate([sa.astype(BF16), zeros_state], axis=1),
                     jnp.concatenate([zeros_state, sb.astype(BF16)], axis=1)], axis=0)
                c_dec = jnp.concatenate([cg * ecum[:, ha:ha + 1], cg * ecum[:, hb:hb + 1]],
                                        axis=1).astype(BF16)
                y_scr[b, :, lanes] = _dot(mh, x_bd) + _dot_nt(c_dec, s_bd)
                upd = _dot_tn((xp * pick(wj)).astype(BF16), bgb)
                s_scr[b, ha] = elast[:, ha:ha + 1] * sa + upd[:P_B]
                s_scr[b, hb] = elast[:, hb:hb + 1] * sb + upd[P_B:]
        y = y_scr[b] + dvec_ref[...] * xs
        y_ref[b] = _rmsnorm_rows(y * _silu(zb_ref[b]), nw_ref[...]).astype(y_ref.dtype)


def _ssd_chunk(xbc_all, dt_all, zb_ref, alog_ref, dvec_ref, nw_ref, y_ref, s_scr, y_scr, q, nb):
    if 2 * q == LANES and 2 * P_B == LANES:
        return _ssd_chunk_pairs(xbc_all, dt_all, zb_ref, alog_ref, dvec_ref, nw_ref, y_ref, s_scr,
                                y_scr, q, nb)
    tri = _lower_tri(q)
    tri_f = tri.astype(F32)
    a = -jnp.exp(alog_ref[...])
    heads_per_group = H_B // G_B
    for b in range(nb):
        xbc, dt = xbc_all[b], dt_all[b]
        xs = xbc[:, :W_GROUP]
        cum = _dot_exact(tri_f, dt * a)
        cum_t = cum.T
        dt_t = dt.T
        last = cum[q - 1:q, :]
        ecum = jnp.exp(cum)
        wj = jnp.exp(last - cum) * dt
        elast = jnp.exp(last)
        for g in range(G_B):
            bg = xbc[:, W_GROUP + g * N_B:W_GROUP + (g + 1) * N_B]
            cg = xbc[:, W_GROUP + (G_B + g) * N_B:W_GROUP + (G_B + g + 1) * N_B]
            bgb = bg.astype(BF16)
            cb_g = _dot_nt(cg.astype(BF16), bgb)
            for hh in range(heads_per_group):
                h = g * heads_per_group + hh
                hs = slice(h * P_B, (h + 1) * P_B)
                seg = cum[:, h:h + 1] - cum_t[h:h + 1, :]
                dec = jnp.exp(jnp.where(tri, seg, NEG))
                mh = (cb_g * dec * dt_t[h:h + 1, :]).astype(BF16)
                xh = xs[:, hs]
                s_old = s_scr[b, h]
                yh = _dot(mh, xh.astype(BF16)) + _dot_nt((cg * ecum[:, h:h + 1]).astype(BF16),
                                                         s_old.astype(BF16))
                y_scr[b, :, hs] = yh
                xw = (xh * wj[:, h:h + 1]).astype(BF16)
                s_scr[b, h] = elast[:, h:h + 1] * s_old + _dot_tn(xw, bgb)
        y = y_scr[b] + dvec_ref[...] * xs
        y_ref[b] = _rmsnorm_rows(y * _silu(zb_ref[b]), nw_ref[...]).astype(y_ref.dtype)


def _ssd_kernel(xbc_ref, dt_ref, zb_ref, buf_ref, s0_ref, cw_ref, cb_ref, dtb_ref, alog_ref,
                dvec_ref, nw_ref, y_ref, sout_ref, cout_ref, xp_scr, s_scr, y_scr, *, q, nb):
    c = pl.program_id(1)

    @pl.when(c == 0)
    def _():
        s_scr[...] = s0_ref[...]
        xp_scr[:, CONV_TAIL:CONV_PAD, :] = buf_ref[...]

    xbc, dt = _ssd_inputs(xbc_ref, dt_ref, cw_ref, cb_ref, dtb_ref, xp_scr, q, nb)
    _ssd_chunk(xbc, dt, zb_ref, alog_ref, dvec_ref, nw_ref, y_ref, s_scr, y_scr, q, nb)

    @pl.when(c == pl.num_programs(1) - 1)
    def _():
        sout_ref[...] = s_scr[...]
        cout_ref[...] = xp_scr[:, CONV_TAIL:CONV_PAD, :]


def _ssd_decode_kernel(pt_ref, lam_ref, xbc_ref, dt_ref, zb_ref, buf_ref, s0_ref, cw_ref, cb_ref,
                       dtb_ref, alog_ref, dvec_ref, nw_ref, q_ref, kn_ref, vn_ref, z_ref, w_ref,
                       ck_ref, cv_ref, y_ref, sout_ref, cout_ref, o_ref,
                       xp_scr, s_scr, y_scr, qrow_scr, m_scr, l_scr, acc_scr, kbuf, vbuf, sem,
                       *, q, nb, n_pp, steps_per_seq, ls, page, layer, lam_init):
    c = pl.program_id(1)
    t = pl.program_id(0) * pl.num_programs(1) + c
    n_steps = pl.num_programs(0) * pl.num_programs(1)
    part = t % steps_per_seq
    slot = t % 2

    def page_copies(step, buf_slot):
        seq = step // steps_per_seq
        first = (step % steps_per_seq) * n_pp
        pages = [pt_ref[seq, first + i] for i in range(n_pp)]
        k_copies = [pltpu.make_async_copy(ck_ref.at[layer, pg], kbuf.at[buf_slot, i],
                                          sem.at[buf_slot, 0]) for i, pg in enumerate(pages)]
        v_copies = [pltpu.make_async_copy(cv_ref.at[layer, pg], vbuf.at[buf_slot, i],
                                          sem.at[buf_slot, 1]) for i, pg in enumerate(pages)]
        return k_copies, v_copies

    @pl.when(t == 0)
    def _():
        k_copies, v_copies = page_copies(0, 0)
        for cp in k_copies + v_copies:
            cp.start()

    @pl.when(c == 0)
    def _():
        s_scr[...] = s0_ref[...]
        xp_scr[:, CONV_TAIL:CONV_PAD, :] = buf_ref[...]

    @pl.when(part == 0)
    def _():
        _decode_init(q_ref, qrow_scr, m_scr, l_scr, acc_scr, ls)

    k_now, v_now = page_copies(t, slot)
    nxt = jnp.minimum(t + 1, n_steps - 1)
    k_next, v_next = page_copies(nxt, 1 - slot)

    kp = [kbuf.at[slot, i] for i in range(n_pp)]
    vp = [vbuf.at[slot, i] for i in range(n_pp)]
    for cp in k_next + v_next:
        cp.start()
    for cp in k_now + v_now:
        cp.wait()
    scores = [_dot_nt(qrow_scr[h], _decode_head_rows(kp, h, page)) for h in range(H_A)]
    xbc, dt = _ssd_inputs(xbc_ref, dt_ref, cw_ref, cb_ref, dtb_ref, xp_scr, q, nb)
    for h in range(H_A):
        _decode_accumulate(h, scores[h], _decode_head_rows(vp, h, page), m_scr, l_scr, acc_scr)
    _ssd_chunk(xbc, dt, zb_ref, alog_ref, dvec_ref, nw_ref, y_ref, s_scr, y_scr, q, nb)

    @pl.when(c == pl.num_programs(1) - 1)
    def _():
        sout_ref[...] = s_scr[...]
        cout_ref[...] = xp_scr[:, CONV_TAIL:CONV_PAD, :]

    @pl.when(part == steps_per_seq - 1)
    def _():
        _decode_finish(lam_ref, kn_ref, vn_ref, z_ref, w_ref, o_ref, qrow_scr, m_scr, l_scr,
                       acc_scr, ls, lam_init)

    @pl.when(t == n_steps - 1)
    def _():
        for cp in k_next + v_next:
            cp.wait()


def _ssd_specs(B, L, nb, q, index):
    full = lambda shape: pl.BlockSpec(shape, index(lambda b, c: (0,) * len(shape)))
    in_specs = [
        pl.BlockSpec((nb, q, CONV_B_CH), index(lambda b, c: (b, c, 0))),
        pl.BlockSpec((nb, q, LANES), index(lambda b, c: (b, c, 0))),
        pl.BlockSpec((nb, q, W_GROUP), index(lambda b, c: (b, c, 3))),
        pl.BlockSpec((nb, CONV_W - 1, CONV_B_CH), index(lambda b, c: (b, 0, 0))),
        pl.BlockSpec((nb, H_B, P_B, N_B), index(lambda b, c: (b, 0, 0, 0))),
        full((CONV_W, CONV_B_CH)), full((1, CONV_B_CH)), full((1, LANES)), full((1, LANES)),
        full((1, W_GROUP)), full((1, W_GROUP)),
    ]
    out_specs = [
        pl.BlockSpec((nb, q, W_GROUP), index(lambda b, c: (b, c, 0))),
        pl.BlockSpec((nb, H_B, P_B, N_B), index(lambda b, c: (b, 0, 0, 0))),
        pl.BlockSpec((nb, CONV_W - 1, CONV_B_CH), index(lambda b, c: (b, 0, 0))),
    ]
    scratch = [
        pltpu.VMEM((nb, q + SUBLANES, CONV_B_CH), F32),
        pltpu.VMEM((nb, H_B, P_B, N_B), F32),
        pltpu.VMEM((nb, q, W_GROUP), F32),
    ]
    return in_specs, out_specs, scratch


def _ssd_out_shape(B, L, out_dtype):
    return [
        jax.ShapeDtypeStruct((B, L, W_GROUP), out_dtype),
        jax.ShapeDtypeStruct((B, H_B, P_B, N_B), F32),
        jax.ShapeDtypeStruct((B, CONV_W - 1, CONV_B_CH), F32),
    ]


def _ssd(rest, small, buf, s0, cw, cb, dtb, alog, dvec, nw, B, L, out_dtype):
    q = math.gcd(L, 64)
    nc = L // q
    nb = _batch_group(B, MIXER_SEQS)
    in_specs, out_specs, scratch = _ssd_specs(B, L, nb, q, lambda f: f)
    rest3 = rest.reshape(B, L, REST_COLS)
    return pl.pallas_call(
        functools.partial(_ssd_kernel, q=q, nb=nb),
        grid=(B // nb, nc),
        in_specs=in_specs,
        out_specs=tuple(out_specs),
        out_shape=tuple(_ssd_out_shape(B, L, out_dtype)),
        scratch_shapes=scratch,
        compiler_params=_cparams(("arbitrary", "arbitrary")),
        name="ssd",
    )(rest3, small.reshape(B, L, LANES), rest3, buf, s0, cw, cb, dtb, alog, dvec, nw)


def _ssd_decode(ssd_args, B, L, out_dtype, lam, q_s, k_s, v_s, rest_s, subln_w, cache_k, cache_v,
                page_table, layer, Bs, Ls, lam_init):
    rest, small, buf, s0, cw, cb, dtb, alog, dvec, nw = ssd_args
    q = math.gcd(L, 64)
    nc = L // q
    nb = _batch_group(B, MIXER_SEQS)
    steps = (B // nb) * nc
    n_pages = page_table.shape[1]
    page = cache_k.shape[2]
    assert (Bs * n_pages) % steps == 0, "decode pages must spread evenly over the SSD grid"
    n_pp = Bs * n_pages // steps
    assert n_pages % n_pp == 0
    steps_per_seq = n_pages // n_pp
    rows = 2 * Ls
    ck = cache_k.reshape(cache_k.shape[0], cache_k.shape[1], page * H_A, 2 * DK_A)
    cv = cache_v.reshape(cache_v.shape[0], cache_v.shape[1], page * H_A, DV_A)

    def step_of(b, c):
        return b * nc + c

    with_pt = lambda f: (lambda b, c, pt: f(b, c))
    ssd_in, ssd_out, ssd_scratch = _ssd_specs(B, L, nb, q, with_pt)
    tok = pl.BlockSpec((Ls, W_GROUP), lambda b, c, pt: (step_of(b, c) // steps_per_seq, 0))
    kern = functools.partial(_ssd_decode_kernel, q=q, nb=nb, n_pp=n_pp, steps_per_seq=steps_per_seq,
                             ls=Ls, page=page, layer=layer, lam_init=lam_init)
    grid_spec = pltpu.PrefetchScalarGridSpec(
        num_scalar_prefetch=1,
        grid=(B // nb, nc),
        in_specs=[pl.BlockSpec(memory_space=pltpu.SMEM)] + ssd_in + [
            tok, tok, tok,
            pl.BlockSpec((Ls, W_GROUP), lambda b, c, pt: (step_of(b, c) // steps_per_seq, 2)),
            pl.BlockSpec((1, LANES), lambda b, c, pt: (0, 0)),
            pl.BlockSpec(memory_space=pl.ANY),
            pl.BlockSpec(memory_space=pl.ANY),
        ],
        out_specs=tuple(ssd_out + [tok]),
        scratch_shapes=ssd_scratch + [
            pltpu.VMEM((H_A, rows, LANES), BF16),
            pltpu.VMEM((H_A, rows, 1), F32),
            pltpu.VMEM((H_A, rows, 1), F32),
            pltpu.VMEM((H_A, rows, DV_A), F32),
            pltpu.VMEM((2, n_pp, page * H_A, LANES), F32),
            pltpu.VMEM((2, n_pp, page * H_A, LANES), F32),
            pltpu.SemaphoreType.DMA((2, 2)),
        ],
    )
    rest3 = rest.reshape(B, L, REST_COLS)
    return pl.pallas_call(
        kern,
        grid_spec=grid_spec,
        out_shape=tuple(_ssd_out_shape(B, L, out_dtype)
                        + [jax.ShapeDtypeStruct((Bs * Ls, W_GROUP), F32)]),
        compiler_params=_cparams(("arbitrary", "arbitrary")),
        name="ssd_decode",
    )(page_table, lam, rest3, small.reshape(B, L, LANES), rest3, buf, s0, cw, cb, dtb, alog, dvec, nw,
      q_s, k_s, v_s, rest_s, subln_w, ck, cv)


def _gla_kernel(qk_ref, v_ref, gc_ref, sm_ref, s0_ref, w2_ref, gb_ref, nw_ref,
                y_ref, sout_ref, st_scr, *, q, nb):
    c = pl.program_id(1)

    @pl.when(c == 0)
    def _():
        st_scr[...] = s0_ref[...]

    seqs = range(nb)
    heads = range(H_C)
    tri = _lower_tri(q)
    width = H_C * DK_C
    nw = nw_ref[...]
    sm = sm_ref[...].reshape(nb * q, LANES).astype(BF16)
    gkl = _dot(sm, w2_ref[...]) + gb_ref[...]
    g = (jnp.minimum(gkl, 0.0) - jnp.log1p(jnp.exp(-jnp.abs(gkl)))) * (1.0 / GK_NORM)
    bc_all = _dot_exact(tri.astype(F32), jnp.concatenate([g[b * q:(b + 1) * q] for b in seqs], axis=1))
    bc = [bc_all[:, b * width:(b + 1) * width] for b in seqs]
    last = [x[q - 1:q, :] for x in bc]
    kc = [qk_ref[b, :, width:] for b in seqs]
    qe = [(qk_ref[b, :, :width] * (DK_C ** -0.5) * jnp.exp(bc[b])).astype(BF16) for b in seqs]
    ke = [(kc[b] * jnp.exp(-bc[b])).astype(BF16) for b in seqs]
    kl = [(kc[b] * jnp.exp(last[b] - bc[b])).astype(BF16) for b in seqs]
    el = [jnp.exp(x) for x in last]
    ks = [slice(h * DK_C, (h + 1) * DK_C) for h in heads]
    vs = [slice(h * DV_C, (h + 1) * DV_C) for h in heads]
    vh = [[v_ref[b, :, vs[h]].astype(BF16) for h in heads] for b in seqs]
    st_old = [[st_scr[b, h] for h in heads] for b in seqs]
    att = [[_dot_nt(qe[b][:, ks[h]], ke[b][:, ks[h]]) for h in heads] for b in seqs]
    o_state = [[_dot_nt(qe[b][:, ks[h]], st_old[b][h].astype(BF16)) for h in heads] for b in seqs]
    upd = [[_dot_tn(vh[b][h], kl[b][:, ks[h]]) for h in heads] for b in seqs]
    o = [[_dot(jnp.where(tri, att[b][h], 0.0).astype(BF16), vh[b][h]) + o_state[b][h]
          for h in heads] for b in seqs]
    for b in seqs:
        for h in heads:
            st_scr[b, h] = el[b][:, ks[h]] * st_old[b][h] + upd[b][h]
            y_ref[b, :, vs[h]] = (_rmsnorm_rows(o[b][h], nw)
                                  * _silu(gc_ref[b, :, vs[h]])).astype(y_ref.dtype)

    @pl.when(c == pl.num_programs(1) - 1)
    def _():
        sout_ref[...] = st_scr[...]


def _gla(rest, small, s0_t, w2p, gb, nw, B, L, out_dtype):
    q = min(L, 64)
    nc = L // q
    nb = _batch_group(B, MIXER_SEQS)
    kern = functools.partial(_gla_kernel, q=q, nb=nb)
    full = lambda shape: pl.BlockSpec(shape, lambda b, c: (0,) * len(shape))
    state = pl.BlockSpec((nb, H_C, DV_C, DK_C), lambda b, c: (b, 0, 0, 0))
    rest3 = rest.reshape(B, L, REST_COLS)
    return pl.pallas_call(
        kern,
        grid=(B // nb, nc),
        in_specs=[
            pl.BlockSpec((nb, q, W_GROUP), lambda b, c: (b, c, 4)),
            pl.BlockSpec((nb, q, W_GROUP), lambda b, c: (b, c, 5)),
            pl.BlockSpec((nb, q, W_GROUP), lambda b, c: (b, c, 6)),
            pl.BlockSpec((nb, q, LANES), lambda b, c: (b, c, 0)),
            state,
            full((LANES, H_C * DK_C)), full((1, H_C * DK_C)), full((1, DV_C)),
        ],
        out_specs=(pl.BlockSpec((nb, q, W_GROUP), lambda b, c: (b, c, 0)), state),
        out_shape=(
            jax.ShapeDtypeStruct((B, L, W_GROUP), out_dtype),
            jax.ShapeDtypeStruct((B, H_C, DV_C, DK_C), F32),
        ),
        scratch_shapes=[pltpu.VMEM((nb, H_C, DV_C, DK_C), F32)],
        compiler_params=_cparams(("arbitrary", "arbitrary")),
        name="gla",
    )(rest3, rest3, rest3, small.reshape(B, L, LANES), s0_t, w2p, gb, nw)


def _lru_kernel(xd_ref, zd_ref, buf_ref, h0_ref, cw_ref, cb_ref, wa_ref, ba_ref, wx_ref, bx_ref,
                lam_ref, y_ref, hout_ref, cout_ref, xp_scr, h_scr, a_scr, u_scr, *, q, nb):
    c = pl.program_id(1)

    @pl.when(c == 0)
    def _():
        h_scr[...] = h0_ref[...]
        xp_scr[:, CONV_TAIL:CONV_PAD, :] = buf_ref[...]

    sp_lam = _softplus(-lam_ref[...])
    for b in range(nb):
        xr = _conv_step(xd_ref.at[b], cw_ref, cb_ref, xp_scr.at[b], q)
        xb = xr.astype(BF16)
        r = _sigmoid(_dot(xb, wa_ref[...]) + ba_ref[...])
        i = _sigmoid(_dot(xb, wx_ref[...]) + bx_ref[...])
        log_a = -LRU_C * r * sp_lam
        a = jnp.exp(log_a)
        th = jnp.tanh(log_a)
        u = jnp.sqrt(-2.0 * th / (1.0 - th)) * (i * xr)
        a = a.reshape(q // SUBLANES, SUBLANES, W_D)
        u = u.reshape(q // SUBLANES, SUBLANES, W_D)
        r8 = lax.broadcasted_iota(jnp.int32, a.shape, 1)
        for s in (1, 2, 4):
            keep = r8 >= s
            a_sh = pltpu.roll(a, s, 1)
            u_sh = pltpu.roll(u, s, 1)
            u = jnp.where(keep, a * u_sh + u, u)
            a = jnp.where(keep, a * a_sh, a)
        a_scr[b] = a.reshape(q, W_D)
        u_scr[b] = u.reshape(q, W_D)
    carry = [h_scr[b] for b in range(nb)]
    for gi in range(q // SUBLANES):
        rows = slice(gi * SUBLANES, (gi + 1) * SUBLANES)
        for b in range(nb):
            hg = a_scr[b, rows, :] * carry[b] + u_scr[b, rows, :]
            y_ref[b, rows, :] = (hg * _silu(zd_ref[b, rows, :])).astype(y_ref.dtype)
            carry[b] = hg[SUBLANES - 1:SUBLANES, :]
    for b in range(nb):
        h_scr[b] = carry[b]

    @pl.when(c == pl.num_programs(1) - 1)
    def _():
        hout_ref[...] = h_scr[...]
        cout_ref[...] = xp_scr[:, CONV_TAIL:CONV_PAD, :]


def _lru(rest, buf, h0, cw, cb, wa, ba, wx, bx, lam, B, L, out_dtype):
    q = min(L, 256)
    nc = L // q
    nb = _batch_group(B, MIXER_SEQS)
    kern = functools.partial(_lru_kernel, q=q, nb=nb)
    full = lambda shape: pl.BlockSpec(shape, lambda b, c: (0,) * len(shape))
    rest3 = rest.reshape(B, L, REST_COLS)
    return pl.pallas_call(
        kern,
        grid=(B // nb, nc),
        in_specs=[
            pl.BlockSpec((nb, q, W_D), lambda b, c: (b, c, 7)),
            pl.BlockSpec((nb, q, W_D), lambda b, c: (b, c, 8)),
            pl.BlockSpec((nb, CONV_W - 1, W_D), lambda b, c: (b, 0, 0)),
            pl.BlockSpec((nb, 1, W_D), lambda b, c: (b, 0, 0)),
            full((CONV_W, W_D)), full((1, W_D)),
            full((W_D, W_D)), full((1, W_D)), full((W_D, W_D)), full((1, W_D)), full((1, W_D)),
        ],
        out_specs=(
            pl.BlockSpec((nb, q, W_D), lambda b, c: (b, c, 0)),
            pl.BlockSpec((nb, 1, W_D), lambda b, c: (b, 0, 0)),
            pl.BlockSpec((nb, CONV_W - 1, W_D), lambda b, c: (b, 0, 0)),
        ),
        out_shape=(
            jax.ShapeDtypeStruct((B, L, W_D), out_dtype),
            jax.ShapeDtypeStruct((B, 1, W_D), F32),
            jax.ShapeDtypeStruct((B, CONV_W - 1, W_D), F32),
        ),
        scratch_shapes=[
            pltpu.VMEM((nb, q + SUBLANES, W_D), F32),
            pltpu.VMEM((nb, 1, W_D), F32),
            pltpu.VMEM((nb, q, W_D), F32),
            pltpu.VMEM((nb, q, W_D), F32),
        ],
        compiler_params=_cparams(("arbitrary", "arbitrary")),
        name="lru",
    )(rest3, rest3, buf, h0, cw, cb, wa, ba, wx, bx, lam)


def _outproj_kernel(ya_ref, yb_ref, yc_ref, yd_ref, w_ref, x_ref, fw_ref, o_ref, *, final):
    out = _dot(ya_ref[...].astype(BF16), w_ref[0])
    for g, y_ref in enumerate((yb_ref, yc_ref, yd_ref), start=1):
        out = out + _dot(y_ref[...].astype(BF16), w_ref[g])
    res = x_ref[...] + out
    if final:
        res = _rmsnorm_rows(res, fw_ref[...])
    o_ref[...] = res


def _outproj(ys, w_out4, x2d, final_w, final):
    T = x2d.shape[0]
    tm = min(512, T)
    kern = functools.partial(_outproj_kernel, final=final)
    yspec = pl.BlockSpec((tm, W_GROUP), lambda i: (i, 0))
    return pl.pallas_call(
        kern,
        grid=(T // tm,),
        in_specs=[yspec] * 4 + [
            pl.BlockSpec((4, W_GROUP, D_MODEL), lambda i: (0, 0, 0)),
            pl.BlockSpec((tm, D_MODEL), lambda i: (i, 0)),
            pl.BlockSpec((1, D_MODEL), lambda i: (0, 0)),
        ],
        out_specs=pl.BlockSpec((tm, D_MODEL), lambda i: (i, 0)),
        out_shape=jax.ShapeDtypeStruct((T, D_MODEL), F32),
        compiler_params=_cparams(("arbitrary",)),
        name="outproj",
    )(*ys, w_out4, x2d, final_w)


def _rope_table(pos, rows):
    half = ROPE_DIM // 2
    inv = ROPE_THETA ** (-jnp.arange(half, dtype=F32) / half)
    ang = pos.astype(F32)[:, None] * inv[None, :]
    cos, sin = jnp.cos(ang), jnp.sin(ang)
    n = pos.shape[0]
    pad = jnp.zeros((n, DK_A - ROPE_DIM), F32)
    c64 = jnp.concatenate([cos, cos, pad + 1.0], axis=1)
    s1 = jnp.concatenate([-sin, jnp.zeros_like(sin), pad], axis=1)
    s2 = jnp.concatenate([jnp.zeros_like(sin), sin, pad], axis=1)
    tab = jnp.stack([jnp.tile(t, (1, LANES // DK_A)) for t in (c64, s1, s2)])
    return jnp.tile(tab, (1, rows // n, 1))


def _block_diag(w):
    nb, bw, _ = w.shape
    eye = jnp.eye(nb, dtype=w.dtype)
    return (eye[:, None, :, None] * w[:, :, None, :]).reshape(nb * bw, nb * bw)


def _prep_layer(l, norm_w, w_in, w_out, lam_q1, lam_k1, lam_q2, lam_k2, subln_w, ssd_conv_w,
                ssd_conv_b, ssd_dt_bias, ssd_a_log, ssd_d, ssd_norm_w, gla_gk_w2, gla_gk_b,
                gla_norm_w, lru_conv_w, lru_conv_b, lru_wa, lru_ba, lru_wx, lru_bx, lru_lambda):
    w = w_in[l]
    cols = lambda a, n: w[:, a:a + n]
    w_main = jnp.stack([
        cols(_QA, 512), cols(_KA, 512), cols(_VA, 512), cols(_XBC, 512), cols(_XBC + 512, 512),
        cols(_ZA, 512), cols(_ZB, 512), jnp.concatenate([cols(_QC, 256), cols(_KC, 256)], axis=1),
        cols(_VC, 512), cols(_GC, 512), cols(_XD, 512), cols(_ZD, 512)]).astype(BF16)
    w_small = jnp.concatenate([cols(_DT, H_B), cols(_GKLR, GK_RANK),
                               jnp.zeros((D_MODEL, LANES - H_B - GK_RANK), F32)], axis=1).astype(BF16)
    lam_init = 0.8 - 0.6 * math.exp(-0.3 * l)
    lam = (jnp.exp(jnp.sum(lam_q1[l] * lam_k1[l])) - jnp.exp(jnp.sum(lam_q2[l] * lam_k2[l]))
           + lam_init).reshape(1, 1).astype(F32)
    pad_lanes = lambda v: jnp.concatenate([v, jnp.zeros((LANES - v.shape[0],), F32)]).reshape(1, LANES)
    w2p = jnp.zeros((LANES, H_C * DK_C), F32).at[H_B:H_B + GK_RANK].set(gla_gk_w2[l]).astype(BF16)
    return dict(
        norm_w=norm_w[l].reshape(1, D_MODEL), w_main=w_main, w_small=w_small,
        w_out4=w_out[l].reshape(4, W_GROUP, D_MODEL).astype(BF16),
        lam=lam, lam_init=lam_init, subln_w=subln_w[l].reshape(1, DV_A),
        ssd_cw=ssd_conv_w[l], ssd_cb=ssd_conv_b[l].reshape(1, CONV_B_CH),
        ssd_dtb=pad_lanes(ssd_dt_bias[l]), ssd_alog=pad_lanes(ssd_a_log[l]),
        ssd_dvec=jnp.repeat(ssd_d[l], P_B).reshape(1, W_GROUP), ssd_nw=ssd_norm_w[l].reshape(1, W_GROUP),
        gla_w2p=w2p, gla_gb=gla_gk_b[l].reshape(1, H_C * DK_C), gla_nw=gla_norm_w[l].reshape(1, DV_C),
        lru_cw=lru_conv_w[l], lru_cb=lru_conv_b[l].reshape(1, W_D),
        lru_wa=_block_diag(lru_wa[l]).astype(BF16), lru_ba=lru_ba[l].reshape(1, W_D),
        lru_wx=_block_diag(lru_wx[l]).astype(BF16), lru_bx=lru_bx[l].reshape(1, W_D),
        lru_lam=lru_lambda[l].reshape(1, W_D),
    )


def _ssd_args(rest, small, states, p):
    return (rest, small, states[1], states[0], p["ssd_cw"], p["ssd_cb"], p["ssd_dtb"], p["ssd_alog"],
            p["ssd_dvec"], p["ssd_nw"])


def _mixers(rest, small, states, p, B, L, ydt, with_ssd=True):
    _, _, gla_s0, lru_h0, lru_buf = states
    out = {}
    if with_ssd:
        out["yb"], out["ssd_s"], out["ssd_c"] = _ssd(*_ssd_args(rest, small, states, p), B, L, ydt)
    yc, gla_st = _gla(rest, small, jnp.swapaxes(gla_s0, -1, -2), p["gla_w2p"], p["gla_gb"],
                      p["gla_nw"], B, L, ydt)
    yd, lru_h, lru_c = _lru(rest, lru_buf, lru_h0.reshape(B, 1, W_D), p["lru_cw"], p["lru_cb"],
                            p["lru_wa"], p["lru_ba"], p["lru_wx"], p["lru_bx"], p["lru_lam"], B, L, ydt)
    out.update(yc=yc, yd=yd, gla_s=jnp.swapaxes(gla_st, -1, -2), lru_h=lru_h.reshape(B, W_D),
               lru_c=lru_c)
    return out


def _finish_layer(x2d, ya, m, p, T, final_w, final):
    ys = (ya, m["yb"].reshape(T, W_GROUP), m["yc"].reshape(T, W_GROUP), m["yd"].reshape(T, W_D))
    x_new = _outproj(ys, p["w_out4"], x2d, final_w, final)
    return x_new, (m["ssd_s"], m["ssd_c"], m["gla_s"], m["lru_h"], m["lru_c"])


def _layer_pair(hp, hs, Bp, Lp, Bs, Ls, tab_p, tab_s, p, st_p, st_s, past, layer, depth, kv_p, kv_s,
                final_w, final):
    cache_k, cache_v, page_table = past
    ydt_p = BF16 if Lp % 16 == 0 else F32
    ydt_s = BF16 if Ls % 16 == 0 else F32
    q_p, k_p, v_p, rest_p, small_p, kf_p, vf_p = _inproj(
        hp, p["norm_w"], p["w_main"], p["w_small"], tab_p, Lp, layer, depth, kv_p)
    q_s, k_s, v_s, rest_s, small_s, kf_s, vf_s = _inproj(
        hs, p["norm_w"], p["w_main"], p["w_small"], tab_s, Ls, layer, depth, kv_s)
    yb_p, ssd_s_p, ssd_c_p, ya_s = _ssd_decode(
        _ssd_args(rest_p, small_p, st_p, p), Bp, Lp, ydt_p, p["lam"], q_s, k_s, v_s, rest_s,
        p["subln_w"], cache_k, cache_v, page_table, layer, Bs, Ls, p["lam_init"])
    ya_p = _attn_prompt(p["lam"], q_p, k_p, v_p, rest_p, p["subln_w"], Bp, Lp, p["lam_init"])
    m_p = _mixers(rest_p, small_p, st_p, p, Bp, Lp, ydt_p, with_ssd=False)
    m_p.update(yb=yb_p, ssd_s=ssd_s_p, ssd_c=ssd_c_p)
    m_s = _mixers(rest_s, small_s, st_s, p, Bs, Ls, ydt_s)
    hp, new_p = _finish_layer(hp, ya_p, m_p, p, Bp * Lp, final_w, final)
    hs, new_s = _finish_layer(hs, ya_s, m_s, p, Bs * Ls, final_w, final)
    return hp, hs, new_p, new_s, (kf_p, vf_p), (kf_s, vf_s)


def kernel(x_prompt, x_sample, cache_k, cache_v, page_table, state_ssd, state_ssd_conv, state_gla, state_lru, state_lru_conv, norm_w, w_in, w_out, lam_q1, lam_k1, lam_q2, lam_k2, subln_w, ssd_conv_w, ssd_conv_b, ssd_dt_bias, ssd_a_log, ssd_d, ssd_norm_w, gla_gk_w2, gla_gk_b, gla_norm_w, lru_conv_w, lru_conv_b, lru_wa, lru_ba, lru_wx, lru_bx, lru_lambda, final_norm_w):
    Bp, Lp, _ = x_prompt.shape
    Bs, Ls, _ = x_sample.shape
    depth = w_in.shape[0]
    past_len = page_table.shape[1] * cache_k.shape[2]
    tab_p = _rope_table(jnp.arange(Lp, dtype=jnp.int32), max(Lp, min(INPROJ_ROWS, Bp * Lp)))
    tab_s = _rope_table(past_len + jnp.arange(Ls, dtype=jnp.int32), max(Ls, min(INPROJ_ROWS, Bs * Ls)))
    final_w = final_norm_w.reshape(1, D_MODEL)
    hp = x_prompt.reshape(Bp * Lp, D_MODEL)
    hs = x_sample.reshape(Bs * Ls, D_MODEL)
    sp, ss = [], []
    kv_p = tuple(jnp.zeros((depth, Bp * Lp * H_A, LANES), F32) for _ in range(2))
    kv_s = tuple(jnp.zeros((depth, Bs * Ls * H_A, LANES), F32) for _ in range(2))
    for l in range(depth):
        p = _prep_layer(l, norm_w, w_in, w_out, lam_q1, lam_k1, lam_q2, lam_k2, subln_w, ssd_conv_w,
                        ssd_conv_b, ssd_dt_bias, ssd_a_log, ssd_d, ssd_norm_w, gla_gk_w2, gla_gk_b,
                        gla_norm_w, lru_conv_w, lru_conv_b, lru_wa, lru_ba, lru_wx, lru_bx, lru_lambda)
        final = l == depth - 1
        st_p = (jnp.zeros((Bp, H_B, P_B, N_B), F32), jnp.zeros((Bp, CONV_W - 1, CONV_B_CH), F32),
                jnp.zeros((Bp, H_C, DK_C, DV_C), F32), jnp.zeros((Bp, W_D), F32),
                jnp.zeros((Bp, CONV_W - 1, W_D), F32))
        st_s = (state_ssd[l], state_ssd_conv[l], state_gla[l], state_lru[l], state_lru_conv[l])
        hp, hs, new_p, new_s, kv_p, kv_s = _layer_pair(
            hp, hs, Bp, Lp, Bs, Ls, tab_p, tab_s, p, st_p, st_s, (cache_k, cache_v, page_table), l,
            depth, kv_p, kv_s, final_w, final)
        sp.append(new_p)
        ss.append(new_s)
    outs_p = [x.reshape(depth, Bp, Lp, H_A, LANES) for x in kv_p]
    outs_p += [jnp.stack([s[i] for s in sp]) for i in range(5)]
    outs_s = [x.reshape(depth, Bs, Ls, H_A, LANES) for x in kv_s]
    outs_s += [jnp.stack([s[i] for s in ss]) for i in range(5)]
    return (hp.reshape(Bp, Lp, D_MODEL), hs.reshape(Bs, Ls, D_MODEL), *outs_p, *outs_s)
```

```python
import functools
import math

import jax
import jax.numpy as jnp
from jax import lax
from jax.experimental import pallas as pl
from jax.experimental.pallas import tpu as pltpu

F32 = jnp.float32
BF16 = jnp.bfloat16
EPS = 1e-6
NEG = -1e30
LOG2E = math.log2(math.e)

D_MODEL = 2048
W_GROUP = 512
H_A, DV_A, DK_A = 4, 128, 64
ROPE_DIM, ROPE_THETA = 16, 500000.0
H_B, P_B, N_B, G_B = 8, 64, 128, 2
CONV_W = 4
CONV_B_CH = W_GROUP + 2 * G_B * N_B
H_C, DK_C, DV_C = 4, 64, 128
GK_RANK, GK_NORM = 16, 16.0
W_D, NB_D, BW_D = 512, 8, 64
LRU_C = 8.0

LANES = 128
SUBLANES = 8
COL_BLK = 512
INPROJ_ROWS = 1024
ATTN_BQ = 512
ATTN_BK = 512
MIXER_SEQS = 4
VMEM_LIMIT = 56 * 1024 * 1024

_SPLITS = (512, 512, 512, 512, CONV_B_CH, H_B, W_GROUP, 256, 256, 512, GK_RANK, W_GROUP, W_D, W_D)
_OFF = [0]
for _s in _SPLITS:
    _OFF.append(_OFF[-1] + _s)
(_QA, _KA, _VA, _ZA, _XBC, _DT, _ZB, _QC, _KC, _VC, _GKLR, _GC, _XD, _ZD) = _OFF[:-1]

REST_COLS = 4608


def _cparams(sem):
    return pltpu.CompilerParams(dimension_semantics=sem, vmem_limit_bytes=VMEM_LIMIT)


def _sigmoid(x):
    return jax.nn.sigmoid(x)


def _silu(x):
    return x * _sigmoid(x)


def _softplus(x):
    return jnp.maximum(x, 0.0) + jnp.log1p(jnp.exp(-jnp.abs(x)))


def _rmsnorm_rows(x, w):
    return x * lax.rsqrt(jnp.mean(x * x, axis=-1, keepdims=True) + EPS) * w


def _dot(a, b):
    return jnp.dot(a, b, preferred_element_type=F32)


def _dot_nt(a, b):
    return lax.dot_general(a, b, (((1,), (1,)), ((), ())), preferred_element_type=F32)


def _dot_tn(a, b):
    return lax.dot_general(a, b, (((0,), (0,)), ((), ())), preferred_element_type=F32)


def _dot_exact(a, b):
    return jnp.dot(a, b, preferred_element_type=F32, precision=lax.Precision.HIGHEST)


def _lower_tri(n):
    r = lax.broadcasted_iota(jnp.int32, (n, n), 0)
    c = lax.broadcasted_iota(jnp.int32, (n, n), 1)
    return r >= c


def _batch_group(b, want):
    return math.gcd(b, want)


def _inproj_kernel(x_ref, nw_ref, w_ref, ws_ref, rope_ref, kf_in_ref, vf_in_ref,
                   q_ref, k_ref, v_ref, rest_ref, small_ref, kf_ref, vf_ref, u_scr):
    del kf_in_ref, vf_in_ref
    j = pl.program_id(1)
    tm = x_ref.shape[0]

    @pl.when(j == 0)
    def _():
        u = _rmsnorm_rows(x_ref[...], nw_ref[...]).astype(BF16)
        u_scr[...] = u
        small_ref[...] = _dot(u, ws_ref[...])

    acc = _dot(u_scr[...], w_ref[...])

    def rope_store(dst, scale, dst_heads=None):
        c, s1, s2 = rope_ref[0], rope_ref[1], rope_ref[2]
        for hh in range(COL_BLK // LANES):
            sl = slice(hh * LANES, (hh + 1) * LANES)
            a = acc[:, sl]
            r = (a * c + pltpu.roll(a, LANES - ROPE_DIM // 2, 1) * s1
                 + pltpu.roll(a, ROPE_DIM // 2, 1) * s2)
            dst[:, sl] = r * scale
            if dst_heads is not None:
                dst_heads[pl.ds(hh, tm, stride=H_A), :] = r

    @pl.when(j == 0)
    def _():
        rope_store(q_ref, DK_A ** -0.5 * LOG2E)

    @pl.when(j == 1)
    def _():
        rope_store(k_ref, 1.0, kf_ref)

    @pl.when(j == 2)
    def _():
        v_ref[...] = acc
        for hh in range(H_A):
            vf_ref[pl.ds(hh, tm, stride=H_A), :] = acc[:, hh * LANES:(hh + 1) * LANES]

    @pl.when(j >= 3)
    def _():
        rest_ref[...] = acc


def _inproj(x2d, norm_w, w_main, w_small, rope_tab, seq_len, layer, depth, kv_prev):
    T = x2d.shape[0]
    tm = min(INPROJ_ROWS, T)
    assert T % tm == 0 and (seq_len % tm == 0 or tm % seq_len == 0)
    n_pos_blk = max(seq_len // tm, 1)
    nj = w_main.shape[0]
    kv_shape = jax.ShapeDtypeStruct((depth, T * H_A, LANES), F32)
    out_shape = (
        jax.ShapeDtypeStruct((T, COL_BLK), F32),
        jax.ShapeDtypeStruct((T, COL_BLK), F32),
        jax.ShapeDtypeStruct((T, COL_BLK), F32),
        jax.ShapeDtypeStruct((T, REST_COLS), F32),
        jax.ShapeDtypeStruct((T, LANES), F32),
        kv_shape, kv_shape,
    )
    row_blk = pl.BlockSpec((tm, COL_BLK), lambda i, j: (i, 0))
    kv_blk = pl.BlockSpec((None, tm * H_A, LANES), lambda i, j: (layer, i, 0))
    in_specs = [
        pl.BlockSpec((tm, D_MODEL), lambda i, j: (i, 0)),
        pl.BlockSpec((1, D_MODEL), lambda i, j: (0, 0)),
        pl.BlockSpec((None, D_MODEL, COL_BLK), lambda i, j: (j, 0, 0)),
        pl.BlockSpec((D_MODEL, LANES), lambda i, j: (0, 0)),
        pl.BlockSpec((3, tm, LANES), lambda i, j: (0, i % n_pos_blk, 0)),
    ]
    in_specs += [pl.BlockSpec(memory_space=pl.ANY)] * 2
    args = (x2d, norm_w, w_main, w_small, rope_tab, *kv_prev)
    aliases = {len(args) - 2: 5, len(args) - 1: 6}
    return pl.pallas_call(
        _inproj_kernel,
        grid=(T // tm, nj),
        in_specs=in_specs,
        out_specs=(
            row_blk, row_blk, row_blk,
            pl.BlockSpec((tm, COL_BLK), lambda i, j: (i, jnp.maximum(j - 3, 0))),
            pl.BlockSpec((tm, LANES), lambda i, j: (i, 0)),
            kv_blk, kv_blk,
        ),
        out_shape=out_shape,
        input_output_aliases=aliases,
        scratch_shapes=[pltpu.VMEM((tm, D_MODEL), BF16)],
        compiler_params=_cparams(("arbitrary", "arbitrary")),
        name="inproj",
    )(*args)


def _attn_finish(o, lam_init, w, z):
    o = _rmsnorm_rows(o, w) * (1.0 - lam_init)
    return o * _silu(z)


def _attn_kernel(lam_ref, q_ref, k_ref, v_ref, z_ref, w_ref, o_ref, kb_scr, vt_scr,
                 m_scr, l_scr, acc_scr, *, bq, bk, lam_init):
    qi = pl.program_id(2)
    n_kb = kb_scr.shape[0]
    diag_blocks = bq // bk

    @pl.when(qi == 0)
    def _():
        for c in range(n_kb):
            rows = slice(c * bk, (c + 1) * bk)
            kb_scr[c] = k_ref[rows, :].astype(BF16)
            vt_scr[c] = v_ref[rows, :].T.astype(BF16)

    qt = q_ref[...].T
    feat = lax.broadcasted_iota(jnp.int32, qt.shape, 0)
    qts = (jnp.where(feat < DK_A, qt, 0.0).astype(BF16),
           jnp.where(feat >= DK_A, qt, 0.0).astype(BF16))
    m_scr[...] = jnp.full(m_scr.shape, NEG, F32)
    l_scr[...] = jnp.zeros(l_scr.shape, F32)
    acc_scr[...] = jnp.zeros(acc_scr.shape, F32)

    halves = 2
    hw = bq // halves

    def step(j, mask):
        kb = kb_scr[j]
        vtb = vt_scr[j]
        chains = [(m, hf) for hf in range(halves) for m in range(2)]
        n_keys = [(hf + 1) * hw if (mask is not None and diag_blocks == 1) else bk
                  for _, hf in chains]
        ss = [_dot(kb[:nk], qts[m][:, hf * hw:(hf + 1) * hw])
              for (m, hf), nk in zip(chains, n_keys)]
        for (m, hf), nk, s in zip(chains, n_keys, ss):
            cols = slice(hf * hw, (hf + 1) * hw)
            if mask is not None:
                s = jnp.where(mask[:nk, cols], s, NEG)
            mx = m_scr[m, :, cols]
            mn = jnp.maximum(mx, jnp.max(s, axis=0, keepdims=True))
            alpha = jnp.exp2(mx - mn)
            p = jnp.exp2(s - mn)
            l_scr[m, :, cols] = alpha * l_scr[m, :, cols] + jnp.sum(p, axis=0, keepdims=True)
            acc_scr[m, :, cols] = alpha * acc_scr[m, :, cols] + _dot(vtb[:, :nk], p.astype(BF16))
            m_scr[m, :, cols] = mn

    def unmasked(j, carry):
        step(j, None)
        return carry

    lax.fori_loop(0, qi * diag_blocks, unmasked, 0)
    key_i = lax.broadcasted_iota(jnp.int32, (bk, bq), 0)
    qry_i = lax.broadcasted_iota(jnp.int32, (bk, bq), 1)
    for d in range(diag_blocks):
        step(qi * diag_blocks + d, key_i + d * bk <= qry_i)
    lam = lam_ref[0, 0]
    o = (acc_scr[0] / l_scr[0] - lam * (acc_scr[1] / l_scr[1])).T
    o_ref[...] = _attn_finish(o, lam_init, w_ref[...], z_ref[...]).astype(o_ref.dtype)


def _attn_prompt(lam, q, k, v, rest, subln_w, B, L, lam_init):
    bq = min(ATTN_BQ, L)
    bk = min(ATTN_BK, bq)
    nq = L // bq
    kern = functools.partial(_attn_kernel, bq=bq, bk=bk, lam_init=lam_init)
    return pl.pallas_call(
        kern,
        grid=(B, H_A, nq),
        in_specs=[
            pl.BlockSpec(memory_space=pltpu.SMEM),
            pl.BlockSpec((bq, LANES), lambda b, h, i: (b * nq + i, h)),
            pl.BlockSpec((L, LANES), lambda b, h, i: (b, h)),
            pl.BlockSpec((L, LANES), lambda b, h, i: (b, h)),
            pl.BlockSpec((bq, LANES), lambda b, h, i: (b * nq + i, 8 + h)),
            pl.BlockSpec((1, LANES), lambda b, h, i: (0, 0)),
        ],
        out_specs=pl.BlockSpec((bq, LANES), lambda b, h, i: (b * nq + i, h)),
        out_shape=jax.ShapeDtypeStruct((B * L, W_GROUP), BF16),
        scratch_shapes=[
            pltpu.VMEM((L // bk, bk, LANES), BF16),
            pltpu.VMEM((L // bk, DV_A, bk), BF16),
            pltpu.VMEM((2, 1, bq), F32),
            pltpu.VMEM((2, 1, bq), F32),
            pltpu.VMEM((2, DV_A, bq), F32),
        ],
        compiler_params=_cparams(("arbitrary", "arbitrary", "arbitrary")),
        name="attn_prompt",
    )(lam, q, k, v, rest, subln_w)


def _decode_init(q_ref, qrow_scr, m_scr, l_scr, acc_scr, ls):
    rows = 2 * ls
    r = lax.broadcasted_iota(jnp.int32, (rows, LANES), 0)
    c = lax.broadcasted_iota(jnp.int32, (rows, LANES), 1)
    own_map = (r // ls) == (c // DK_A)
    for h in range(H_A):
        qh = q_ref[:, h * LANES:(h + 1) * LANES]
        qrow_scr[h] = jnp.where(own_map, jnp.concatenate([qh, qh], axis=0), 0.0).astype(BF16)
    m_scr[...] = jnp.full(m_scr.shape, NEG, F32)
    l_scr[...] = jnp.zeros(l_scr.shape, F32)
    acc_scr[...] = jnp.zeros(acc_scr.shape, F32)


def _decode_head_rows(pages, h, page):
    head_rows = pl.ds(h, page, stride=H_A)
    return jnp.concatenate([pg[head_rows, :] for pg in pages], axis=0).astype(BF16)


def _decode_accumulate(h, s, vv, m_scr, l_scr, acc_scr):
    mx = m_scr[h]
    mn = jnp.maximum(mx, jnp.max(s, axis=-1, keepdims=True))
    alpha = jnp.exp2(mx - mn)
    pr = jnp.exp2(s - mn)
    l_scr[h] = alpha * l_scr[h] + jnp.sum(pr, axis=-1, keepdims=True)
    acc_scr[h] = alpha * acc_scr[h] + _dot(pr.astype(BF16), vv)
    m_scr[h] = mn


def _decode_finish(lam_ref, kn_ref, vn_ref, z_ref, w_ref, o_ref, qrow_scr, m_scr, l_scr, acc_scr,
                   ls, lam_init):
    rows = 2 * ls
    lam = lam_ref[0, 0]
    w = w_ref[...]
    jj = lax.broadcasted_iota(jnp.int32, (rows, ls), 1)
    qq = lax.broadcasted_iota(jnp.int32, (rows, ls), 0) % ls
    for h in range(H_A):
        cs = slice(h * LANES, (h + 1) * LANES)
        s = _dot_nt(qrow_scr[h], kn_ref[:, cs].astype(BF16))
        _decode_accumulate(h, jnp.where(jj <= qq, s, NEG), vn_ref[:, cs].astype(BF16),
                           m_scr, l_scr, acc_scr)
        o = acc_scr[h] / l_scr[h]
        o = o[:ls] - lam * o[ls:]
        o_ref[:, cs] = _attn_finish(o, lam_init, w, z_ref[:, cs]).astype(o_ref.dtype)


CONV_PAD = SUBLANES
CONV_TAIL = CONV_PAD - (CONV_W - 1)


def _conv_step(x_ref, cw_ref, cb_ref, xp_scr, q):
    xp_scr[CONV_PAD:CONV_PAD + q, :] = x_ref[...]
    y = xp_scr[CONV_TAIL:CONV_TAIL + q, :] * cw_ref[0:1, :]
    for t in range(1, CONV_W):
        y = y + xp_scr[CONV_TAIL + t:CONV_TAIL + t + q, :] * cw_ref[t:t + 1, :]
    y = y + cb_ref[...]
    xp_scr[CONV_TAIL:CONV_PAD, :] = xp_scr[q + CONV_TAIL:q + CONV_PAD, :]
    return y


def _ssd_inputs(xbc_ref, dt_ref, cw_ref, cb_ref, dtb_ref, xp_scr, q, nb):
    xbc = [_silu(_conv_step(xbc_ref.at[b], cw_ref, cb_ref, xp_scr.at[b], q)) for b in range(nb)]
    dt = [_softplus(dt_ref[b] + dtb_ref[...]) for b in range(nb)]
    return xbc, dt


def _ssd_chunk_pairs(xbc_all, dt_all, zb_ref, alog_ref, dvec_ref, nw_ref, y_ref, s_scr, y_scr, q, nb):
    half = LANES // 2
    row = lax.broadcasted_iota(jnp.int32, (q, LANES), 0)
    lane = lax.broadcasted_iota(jnp.int32, (q, LANES), 1)
    hi = lane >= half
    tri2 = row >= lane % half
    a = -jnp.exp(alog_ref[...])
    tri_f = _lower_tri(q).astype(F32)
    heads_per_group = H_B // G_B
    zeros_state = jnp.zeros((P_B, N_B), BF16)
    for b in range(nb):
        xbc, dt = xbc_all[b], dt_all[b]
        xs = xbc[:, :W_GROUP]
        cum = _dot_exact(tri_f, dt * a)
        cum_t = cum.T
        dt_t = dt.T
        last = cum[q - 1:q, :]
        ecum = jnp.exp(cum)
        wj = jnp.exp(last - cum) * dt
        elast = jnp.exp(last)
        for g in range(G_B):
            bgb = xbc[:, W_GROUP + g * N_B:W_GROUP + (g + 1) * N_B].astype(BF16)
            cg = xbc[:, W_GROUP + (G_B + g) * N_B:W_GROUP + (G_B + g + 1) * N_B]
            cb2 = _dot_nt(cg.astype(BF16), jnp.concatenate([bgb, bgb], axis=0))
            for pair in range(heads_per_group // 2):
                ha = g * heads_per_group + 2 * pair
                hb = ha + 1
                lanes = slice(ha * P_B, (hb + 1) * P_B)
                pick = lambda v: jnp.where(hi, v[:, hb:hb + 1], v[:, ha:ha + 1])
                cj = jnp.concatenate([cum_t[ha:ha + 1, :], cum_t[hb:hb + 1, :]], axis=1)
                dtj = jnp.concatenate([dt_t[ha:ha + 1, :], dt_t[hb:hb + 1, :]], axis=1)
                dec = jnp.exp(jnp.where(tri2, pick(cum) - cj, NEG))
                mh = (cb2 * dec * dtj).astype(BF16)
                xp = xs[:, lanes]
                x_bd = jnp.concatenate([jnp.where(hi, 0.0, xp), jnp.where(hi, xp, 0.0)],
                                       axis=0).astype(BF16)
                sa, sb = s_scr[b, ha], s_scr[b, hb]
                s_bd = jnp.concatenate(
                    [jnp.concatenate([sa.astype(BF16), zeros_state], axis=1),
                     jnp.concatenate([zeros_state, sb.astype(BF16)], axis=1)], axis=0)
                c_dec = jnp.concatenate([cg * ecum[:, ha:ha + 1], cg * ecum[:, hb:hb + 1]],
                                        axis=1).astype(BF16)
                y_scr[b, :, lanes] = _dot(mh, x_bd) + _dot_nt(c_dec, s_bd)
                upd = _dot_tn((xp * pick(wj)).astype(BF16), bgb)
                s_scr[b, ha] = elast[:, ha:ha + 1] * sa + upd[:P_B]
                s_scr[b, hb] = elast[:, hb:hb + 1] * sb + upd[P_B:]
        y = y_scr[b] + dvec_ref[...] * xs
        y_ref[b] = _rmsnorm_rows(y * _silu(zb_ref[b]), nw_ref[...]).astype(y_ref.dtype)


def _ssd_chunk(xbc_all, dt_all, zb_ref, alog_ref, dvec_ref, nw_ref, y_ref, s_scr, y_scr, q, nb):
    if 2 * q == LANES and 2 * P_B == LANES:
        return _ssd_chunk_pairs(xbc_all, dt_all, zb_ref, alog_ref, dvec_ref, nw_ref, y_ref, s_scr,
                                y_scr, q, nb)
    tri = _lower_tri(q)
    tri_f = tri.astype(F32)
    a = -jnp.exp(alog_ref[...])
    heads_per_group = H_B // G_B
    for b in range(nb):
        xbc, dt = xbc_all[b], dt_all[b]
        xs = xbc[:, :W_GROUP]
        cum = _dot_exact(tri_f, dt * a)
        cum_t = cum.T
        dt_t = dt.T
        last = cum[q - 1:q, :]
        ecum = jnp.exp(cum)
        wj = jnp.exp(last - cum) * dt
        elast = jnp.exp(last)
        for g in range(G_B):
            bg = xbc[:, W_GROUP + g * N_B:W_GROUP + (g + 1) * N_B]
            cg = xbc[:, W_GROUP + (G_B + g) * N_B:W_GROUP + (G_B + g + 1) * N_B]
            bgb = bg.astype(BF16)
            cb_g = _dot_nt(cg.astype(BF16), bgb)
            for hh in range(heads_per_group):
                h = g * heads_per_group + hh
                hs = slice(h * P_B, (h + 1) * P_B)
                seg = cum[:, h:h + 1] - cum_t[h:h + 1, :]
                dec = jnp.exp(jnp.where(tri, seg, NEG))
                mh = (cb_g * dec * dt_t[h:h + 1, :]).astype(BF16)
                xh = xs[:, hs]
                s_old = s_scr[b, h]
                yh = _dot(mh, xh.astype(BF16)) + _dot_nt((cg * ecum[:, h:h + 1]).astype(BF16),
                                                         s_old.astype(BF16))
                y_scr[b, :, hs] = yh
                xw = (xh * wj[:, h:h + 1]).astype(BF16)
                s_scr[b, h] = elast[:, h:h + 1] * s_old + _dot_tn(xw, bgb)
        y = y_scr[b] + dvec_ref[...] * xs
        y_ref[b] = _rmsnorm_rows(y * _silu(zb_ref[b]), nw_ref[...]).astype(y_ref.dtype)


def _ssd_kernel(xbc_ref, dt_ref, zb_ref, buf_ref, s0_ref, cw_ref, cb_ref, dtb_ref, alog_ref,
                dvec_ref, nw_ref, y_ref, sout_ref, cout_ref, xp_scr, s_scr, y_scr, *, q, nb):
    c = pl.program_id(1)

    @pl.when(c == 0)
    def _():
        s_scr[...] = s0_ref[...]
        xp_scr[:, CONV_TAIL:CONV_PAD, :] = buf_ref[...]

    xbc, dt = _ssd_inputs(xbc_ref, dt_ref, cw_ref, cb_ref, dtb_ref, xp_scr, q, nb)
    _ssd_chunk(xbc, dt, zb_ref, alog_ref, dvec_ref, nw_ref, y_ref, s_scr, y_scr, q, nb)

    @pl.when(c == pl.num_programs(1) - 1)
    def _():
        sout_ref[...] = s_scr[...]
        cout_ref[...] = xp_scr[:, CONV_TAIL:CONV_PAD, :]


def _ssd_decode_kernel(pt_ref, lam_ref, xbc_ref, dt_ref, zb_ref, buf_ref, s0_ref, cw_ref, cb_ref,
                       dtb_ref, alog_ref, dvec_ref, nw_ref, q_ref, kn_ref, vn_ref, z_ref, w_ref,
                       ck_ref, cv_ref, y_ref, sout_ref, cout_ref, o_ref,
                       xp_scr, s_scr, y_scr, qrow_scr, m_scr, l_scr, acc_scr, kbuf, vbuf, sem,
                       *, q, nb, n_pp, steps_per_seq, ls, page, layer, lam_init):
    c = pl.program_id(1)
    t = pl.program_id(0) * pl.num_programs(1) + c
    n_steps = pl.num_programs(0) * pl.num_programs(1)
    part = t % steps_per_seq
    slot = t % 2

    def page_copies(step, buf_slot):
        seq = step // steps_per_seq
        first = (step % steps_per_seq) * n_pp
        pages = [pt_ref[seq, first + i] for i in range(n_pp)]
        k_copies = [pltpu.make_async_copy(ck_ref.at[layer, pg], kbuf.at[buf_slot, i],
                                          sem.at[buf_slot, 0]) for i, pg in enumerate(pages)]
        v_copies = [pltpu.make_async_copy(cv_ref.at[layer, pg], vbuf.at[buf_slot, i],
                                          sem.at[buf_slot, 1]) for i, pg in enumerate(pages)]
        return k_copies, v_copies

    @pl.when(t == 0)
    def _():
        k_copies, v_copies = page_copies(0, 0)
        for cp in k_copies + v_copies:
            cp.start()

    @pl.when(c == 0)
    def _():
        s_scr[...] = s0_ref[...]
        xp_scr[:, CONV_TAIL:CONV_PAD, :] = buf_ref[...]

    @pl.when(part == 0)
    def _():
        _decode_init(q_ref, qrow_scr, m_scr, l_scr, acc_scr, ls)

    k_now, v_now = page_copies(t, slot)
    nxt = jnp.minimum(t + 1, n_steps - 1)
    k_next, v_next = page_copies(nxt, 1 - slot)

    kp = [kbuf.at[slot, i] for i in range(n_pp)]
    vp = [vbuf.at[slot, i] for i in range(n_pp)]
    for cp in k_next + v_next:
        cp.start()
    for cp in k_now + v_now:
        cp.wait()
    scores = [_dot_nt(qrow_scr[h], _decode_head_rows(kp, h, page)) for h in range(H_A)]
    xbc, dt = _ssd_inputs(xbc_ref, dt_ref, cw_ref, cb_ref, dtb_ref, xp_scr, q, nb)
    for h in range(H_A):
        _decode_accumulate(h, scores[h], _decode_head_rows(vp, h, page), m_scr, l_scr, acc_scr)
    _ssd_chunk(xbc, dt, zb_ref, alog_ref, dvec_ref, nw_ref, y_ref, s_scr, y_scr, q, nb)

    @pl.when(c == pl.num_programs(1) - 1)
    def _():
        sout_ref[...] = s_scr[...]
        cout_ref[...] = xp_scr[:, CONV_TAIL:CONV_PAD, :]

    @pl.when(part == steps_per_seq - 1)
    def _():
        _decode_finish(lam_ref, kn_ref, vn_ref, z_ref, w_ref, o_ref, qrow_scr, m_scr, l_scr,
                       acc_scr, ls, lam_init)

    @pl.when(t == n_steps - 1)
    def _():
        for cp in k_next + v_next:
            cp.wait()


def _ssd_specs(B, L, nb, q, index):
    full = lambda shape: pl.BlockSpec(shape, index(lambda b, c: (0,) * len(shape)))
    in_specs = [
        pl.BlockSpec((nb, q, CONV_B_CH), index(lambda b, c: (b, c, 0))),
        pl.BlockSpec((nb, q, LANES), index(lambda b, c: (b, c, 0))),
        pl.BlockSpec((nb, q, W_GROUP), index(lambda b, c: (b, c, 3))),
        pl.BlockSpec((nb, CONV_W - 1, CONV_B_CH), index(lambda b, c: (b, 0, 0))),
        pl.BlockSpec((nb, H_B, P_B, N_B), index(lambda b, c: (b, 0, 0, 0))),
        full((CONV_W, CONV_B_CH)), full((1, CONV_B_CH)), full((1, LANES)), full((1, LANES)),
        full((1, W_GROUP)), full((1, W_GROUP)),
    ]
    out_specs = [
        pl.BlockSpec((nb, q, W_GROUP), index(lambda b, c: (b, c, 0))),
        pl.BlockSpec((nb, H_B, P_B, N_B), index(lambda b, c: (b, 0, 0, 0))),
        pl.BlockSpec((nb, CONV_W - 1, CONV_B_CH), index(lambda b, c: (b, 0, 0))),
    ]
    scratch = [
        pltpu.VMEM((nb, q + SUBLANES, CONV_B_CH), F32),
        pltpu.VMEM((nb, H_B, P_B, N_B), F32),
        pltpu.VMEM((nb, q, W_GROUP), F32),
    ]
    return in_specs, out_specs, scratch


def _ssd_out_shape(B, L, out_dtype):
    return [
        jax.ShapeDtypeStruct((B, L, W_GROUP), out_dtype),
        jax.ShapeDtypeStruct((B, H_B, P_B, N_B), F32),
        jax.ShapeDtypeStruct((B, CONV_W - 1, CONV_B_CH), F32),
    ]


def _ssd(rest, small, buf, s0, cw, cb, dtb, alog, dvec, nw, B, L, out_dtype):
    q = math.gcd(L, 64)
    nc = L // q
    nb = _batch_group(B, MIXER_SEQS)
    in_specs, out_specs, scratch = _ssd_specs(B, L, nb, q, lambda f: f)
    rest3 = rest.reshape(B, L, REST_COLS)
    return pl.pallas_call(
        functools.partial(_ssd_kernel, q=q, nb=nb),
        grid=(B // nb, nc),
        in_specs=in_specs,
        out_specs=tuple(out_specs),
        out_shape=tuple(_ssd_out_shape(B, L, out_dtype)),
        scratch_shapes=scratch,
        compiler_params=_cparams(("arbitrary", "arbitrary")),
        name="ssd",
    )(rest3, small.reshape(B, L, LANES), rest3, buf, s0, cw, cb, dtb, alog, dvec, nw)


def _ssd_decode(ssd_args, B, L, out_dtype, lam, q_s, k_s, v_s, rest_s, subln_w, cache_k, cache_v,
                page_table, layer, Bs, Ls, lam_init):
    rest, small, buf, s0, cw, cb, dtb, alog, dvec, nw = ssd_args
    q = math.gcd(L, 64)
    nc = L // q
    nb = _batch_group(B, MIXER_SEQS)
    steps = (B // nb) * nc
    n_pages = page_table.shape[1]
    page = cache_k.shape[2]
    assert (Bs * n_pages) % steps == 0, "decode pages must spread evenly over the SSD grid"
    n_pp = Bs * n_pages // steps
    assert n_pages % n_pp == 0
    steps_per_seq = n_pages // n_pp
    rows = 2 * Ls
    ck = cache_k.reshape(cache_k.shape[0], cache_k.shape[1], page * H_A, 2 * DK_A)
    cv = cache_v.reshape(cache_v.shape[0], cache_v.shape[1], page * H_A, DV_A)

    def step_of(b, c):
        return b * nc + c

    with_pt = lambda f: (lambda b, c, pt: f(b, c))
    ssd_in, ssd_out, ssd_scratch = _ssd_specs(B, L, nb, q, with_pt)
    tok = pl.BlockSpec((Ls, W_GROUP), lambda b, c, pt: (step_of(b, c) // steps_per_seq, 0))
    kern = functools.partial(_ssd_decode_kernel, q=q, nb=nb, n_pp=n_pp, steps_per_seq=steps_per_seq,
                             ls=Ls, page=page, layer=layer, lam_init=lam_init)
    grid_spec = pltpu.PrefetchScalarGridSpec(
        num_scalar_prefetch=1,
        grid=(B // nb, nc),
        in_specs=[pl.BlockSpec(memory_space=pltpu.SMEM)] + ssd_in + [
            tok, tok, tok,
            pl.BlockSpec((Ls, W_GROUP), lambda b, c, pt: (step_of(b, c) // steps_per_seq, 2)),
            pl.BlockSpec((1, LANES), lambda b, c, pt: (0, 0)),
            pl.BlockSpec(memory_space=pl.ANY),
            pl.BlockSpec(memory_space=pl.ANY),
        ],
        out_specs=tuple(ssd_out + [tok]),
        scratch_shapes=ssd_scratch + [
            pltpu.VMEM((H_A, rows, LANES), BF16),
            pltpu.VMEM((H_A, rows, 1), F32),
            pltpu.VMEM((H_A, rows, 1), F32),
            pltpu.VMEM((H_A, rows, DV_A), F32),
            pltpu.VMEM((2, n_pp, page * H_A, LANES), F32),
            pltpu.VMEM((2, n_pp, page * H_A, LANES), F32),
            pltpu.SemaphoreType.DMA((2, 2)),
        ],
    )
    rest3 = rest.reshape(B, L, REST_COLS)
    return pl.pallas_call(
        kern,
        grid_spec=grid_spec,
        out_shape=tuple(_ssd_out_shape(B, L, out_dtype)
                        + [jax.ShapeDtypeStruct((Bs * Ls, W_GROUP), F32)]),
        compiler_params=_cparams(("arbitrary", "arbitrary")),
        name="ssd_decode",
    )(page_table, lam, rest3, small.reshape(B, L, LANES), rest3, buf, s0, cw, cb, dtb, alog, dvec, nw,
      q_s, k_s, v_s, rest_s, subln_w, ck, cv)


def _gla_kernel(qk_ref, v_ref, gc_ref, sm_ref, s0_ref, w2_ref, gb_ref, nw_ref,
                y_ref, sout_ref, st_scr, *, q, nb):
    c = pl.program_id(1)

    @pl.when(c == 0)
    def _():
        st_scr[...] = s0_ref[...]

    seqs = range(nb)
    pairs = range(H_C // 2)
    width = H_C * DK_C
    nw = nw_ref[...]
    hi = lax.broadcasted_iota(jnp.int32, (q, LANES), 1) >= DK_C
    row2 = lax.broadcasted_iota(jnp.int32, (2 * q, q), 0)
    col2 = lax.broadcasted_iota(jnp.int32, (2 * q, q), 1)
    causal2 = jnp.where(row2 >= q, row2 - q, row2) >= col2
    sm = sm_ref[...].reshape(nb * q, LANES).astype(BF16)
    gkl = _dot(sm, w2_ref[...]) + gb_ref[...]
    g = (jnp.minimum(gkl, 0.0) - jnp.log1p(jnp.exp(-jnp.abs(gkl)))) * (1.0 / GK_NORM)
    bc_all = _dot_exact(_lower_tri(q).astype(F32),
                        jnp.concatenate([g[b * q:(b + 1) * q] for b in seqs], axis=1))
    bc = [bc_all[:, b * width:(b + 1) * width] for b in seqs]
    last = [x[q - 1:q, :] for x in bc]
    kc = [qk_ref[b, :, width:] for b in seqs]
    qe = [qk_ref[b, :, :width] * (DK_C ** -0.5) * jnp.exp(bc[b]) for b in seqs]
    ke = [(kc[b] * jnp.exp(-bc[b])).astype(BF16) for b in seqs]
    kl = [kc[b] * jnp.exp(last[b] - bc[b]) for b in seqs]
    el = [jnp.exp(x) for x in last]
    lanes = [slice(p * LANES, (p + 1) * LANES) for p in pairs]
    vs = [slice(h * DV_C, (h + 1) * DV_C) for h in range(H_C)]
    q2 = [[jnp.concatenate([jnp.where(hi, 0.0, qe[b][:, lanes[p]]),
                            jnp.where(hi, qe[b][:, lanes[p]], 0.0)], axis=0).astype(BF16)
           for p in pairs] for b in seqs]
    kl_a = [[jnp.where(hi, 0.0, kl[b][:, lanes[p]]).astype(BF16) for p in pairs] for b in seqs]
    kl_b = [[jnp.where(hi, kl[b][:, lanes[p]], 0.0).astype(BF16) for p in pairs] for b in seqs]
    vh = [[v_ref[b, :, vs[h]].astype(BF16) for h in range(H_C)] for b in seqs]
    st_old = [[st_scr[b, p] for p in pairs] for b in seqs]
    att = [[_dot_nt(q2[b][p], ke[b][:, lanes[p]]) for p in pairs] for b in seqs]
    o_state = [[_dot_nt(q2[b][p], st_old[b][p].astype(BF16)) for p in pairs] for b in seqs]
    upd = [[_dot_tn(vh[b][2 * p], kl_a[b][p]) + _dot_tn(vh[b][2 * p + 1], kl_b[b][p])
            for p in pairs] for b in seqs]
    for b in seqs:
        for p in pairs:
            attm = jnp.where(causal2, att[b][p], 0.0).astype(BF16)
            st_scr[b, p] = el[b][:, lanes[p]] * st_old[b][p] + upd[b][p]
            for j in range(2):
                h = 2 * p + j
                rows = slice(j * q, (j + 1) * q)
                o = _dot(attm[rows], vh[b][h]) + o_state[b][p][rows]
                y_ref[b, :, vs[h]] = (_rmsnorm_rows(o, nw)
                                      * _silu(gc_ref[b, :, vs[h]])).astype(y_ref.dtype)

    @pl.when(c == pl.num_programs(1) - 1)
    def _():
        sout_ref[...] = st_scr[...]


def _gla(rest, small, s0_t, w2p, gb, nw, B, L, out_dtype):
    q = min(L, 64)
    nc = L // q
    nb = _batch_group(B, MIXER_SEQS)
    kern = functools.partial(_gla_kernel, q=q, nb=nb)
    full = lambda shape: pl.BlockSpec(shape, lambda b, c: (0,) * len(shape))
    state = pl.BlockSpec((nb, H_C // 2, DV_C, 2 * DK_C), lambda b, c: (b, 0, 0, 0))
    rest3 = rest.reshape(B, L, REST_COLS)
    return pl.pallas_call(
        kern,
        grid=(B // nb, nc),
        in_specs=[
            pl.BlockSpec((nb, q, W_GROUP), lambda b, c: (b, c, 4)),
            pl.BlockSpec((nb, q, W_GROUP), lambda b, c: (b, c, 5)),
            pl.BlockSpec((nb, q, W_GROUP), lambda b, c: (b, c, 6)),
            pl.BlockSpec((nb, q, LANES), lambda b, c: (b, c, 0)),
            state,
            full((LANES, H_C * DK_C)), full((1, H_C * DK_C)), full((1, DV_C)),
        ],
        out_specs=(pl.BlockSpec((nb, q, W_GROUP), lambda b, c: (b, c, 0)), state),
        out_shape=(
            jax.ShapeDtypeStruct((B, L, W_GROUP), out_dtype),
            jax.ShapeDtypeStruct((B, H_C // 2, DV_C, 2 * DK_C), F32),
        ),
        scratch_shapes=[pltpu.VMEM((nb, H_C // 2, DV_C, 2 * DK_C), F32)],
        compiler_params=_cparams(("arbitrary", "arbitrary")),
        name="gla",
    )(rest3, rest3, rest3, small.reshape(B, L, LANES), s0_t, w2p, gb, nw)


def _lru_kernel(xd_ref, zd_ref, buf_ref, h0_ref, cw_ref, cb_ref, wa_ref, ba_ref, wx_ref, bx_ref,
                lam_ref, y_ref, hout_ref, cout_ref, xp_scr, h_scr, a_scr, u_scr, *, q, nb):
    c = pl.program_id(1)

    @pl.when(c == 0)
    def _():
        h_scr[...] = h0_ref[...]
        xp_scr[:, CONV_TAIL:CONV_PAD, :] = buf_ref[...]

    sp_lam = _softplus(-lam_ref[...])
    for b in range(nb):
        xr = _conv_step(xd_ref.at[b], cw_ref, cb_ref, xp_scr.at[b], q)
        xb = xr.astype(BF16)
        r = _sigmoid(_dot(xb, wa_ref[...]) + ba_ref[...])
        i = _sigmoid(_dot(xb, wx_ref[...]) + bx_ref[...])
        log_a = -LRU_C * r * sp_lam
        a = jnp.exp(log_a)
        th = jnp.tanh(log_a)
        u = jnp.sqrt(-2.0 * th / (1.0 - th)) * (i * xr)
        a = a.reshape(q // SUBLANES, SUBLANES, W_D)
        u = u.reshape(q // SUBLANES, SUBLANES, W_D)
        r8 = lax.broadcasted_iota(jnp.int32, a.shape, 1)
        for s in (1, 2, 4):
            keep = r8 >= s
            a_sh = pltpu.roll(a, s, 1)
            u_sh = pltpu.roll(u, s, 1)
            u = jnp.where(keep, a * u_sh + u, u)
            a = jnp.where(keep, a * a_sh, a)
        a_scr[b] = a.reshape(q, W_D)
        u_scr[b] = u.reshape(q, W_D)
    carry = [h_scr[b] for b in range(nb)]
    for gi in range(q // SUBLANES):
        rows = slice(gi * SUBLANES, (gi + 1) * SUBLANES)
        for b in range(nb):
            hg = a_scr[b, rows, :] * carry[b] + u_scr[b, rows, :]
            y_ref[b, rows, :] = (hg * _silu(zd_ref[b, rows, :])).astype(y_ref.dtype)
            carry[b] = hg[SUBLANES - 1:SUBLANES, :]
    for b in range(nb):
        h_scr[b] = carry[b]

    @pl.when(c == pl.num_programs(1) - 1)
    def _():
        hout_ref[...] = h_scr[...]
        cout_ref[...] = xp_scr[:, CONV_TAIL:CONV_PAD, :]


def _lru(rest, buf, h0, cw, cb, wa, ba, wx, bx, lam, B, L, out_dtype):
    q = min(L, 256)
    nc = L // q
    nb = _batch_group(B, MIXER_SEQS)
    kern = functools.partial(_lru_kernel, q=q, nb=nb)
    full = lambda shape: pl.BlockSpec(shape, lambda b, c: (0,) * len(shape))
    rest3 = rest.reshape(B, L, REST_COLS)
    return pl.pallas_call(
        kern,
        grid=(B // nb, nc),
        in_specs=[
            pl.BlockSpec((nb, q, W_D), lambda b, c: (b, c, 7)),
            pl.BlockSpec((nb, q, W_D), lambda b, c: (b, c, 8)),
            pl.BlockSpec((nb, CONV_W - 1, W_D), lambda b, c: (b, 0, 0)),
            pl.BlockSpec((nb, 1, W_D), lambda b, c: (b, 0, 0)),
            full((CONV_W, W_D)), full((1, W_D)),
            full((W_D, W_D)), full((1, W_D)), full((W_D, W_D)), full((1, W_D)), full((1, W_D)),
        ],
        out_specs=(
            pl.BlockSpec((nb, q, W_D), lambda b, c: (b, c, 0)),
            pl.BlockSpec((nb, 1, W_D), lambda b, c: (b, 0, 0)),
            pl.BlockSpec((nb, CONV_W - 1, W_D), lambda b, c: (b, 0, 0)),
        ),
        out_shape=(
            jax.ShapeDtypeStruct((B, L, W_D), out_dtype),
            jax.ShapeDtypeStruct((B, 1, W_D), F32),
            jax.ShapeDtypeStruct((B, CONV_W - 1, W_D), F32),
        ),
        scratch_shapes=[
            pltpu.VMEM((nb, q + SUBLANES, W_D), F32),
            pltpu.VMEM((nb, 1, W_D), F32),
            pltpu.VMEM((nb, q, W_D), F32),
            pltpu.VMEM((nb, q, W_D), F32),
        ],
        compiler_params=_cparams(("arbitrary", "arbitrary")),
        name="lru",
    )(rest3, rest3, buf, h0, cw, cb, wa, ba, wx, bx, lam)


def _outproj_kernel(ya_ref, yb_ref, yc_ref, yd_ref, w_ref, x_ref, fw_ref, o_ref, *, final):
    out = _dot(ya_ref[...].astype(BF16), w_ref[0])
    for g, y_ref in enumerate((yb_ref, yc_ref, yd_ref), start=1):
        out = out + _dot(y_ref[...].astype(BF16), w_ref[g])
    res = x_ref[...] + out
    if final:
        res = _rmsnorm_rows(res, fw_ref[...])
    o_ref[...] = res


def _outproj(ys, w_out4, x2d, final_w, final):
    T = x2d.shape[0]
    tm = min(512, T)
    kern = functools.partial(_outproj_kernel, final=final)
    yspec = pl.BlockSpec((tm, W_GROUP), lambda i: (i, 0))
    return pl.pallas_call(
        kern,
        grid=(T // tm,),
        in_specs=[yspec] * 4 + [
            pl.BlockSpec((4, W_GROUP, D_MODEL), lambda i: (0, 0, 0)),
            pl.BlockSpec((tm, D_MODEL), lambda i: (i, 0)),
            pl.BlockSpec((1, D_MODEL), lambda i: (0, 0)),
        ],
        out_specs=pl.BlockSpec((tm, D_MODEL), lambda i: (i, 0)),
        out_shape=jax.ShapeDtypeStruct((T, D_MODEL), F32),
        compiler_params=_cparams(("arbitrary",)),
        name="outproj",
    )(*ys, w_out4, x2d, final_w)


def _rope_table(pos, rows):
    half = ROPE_DIM // 2
    inv = ROPE_THETA ** (-jnp.arange(half, dtype=F32) / half)
    ang = pos.astype(F32)[:, None] * inv[None, :]
    cos, sin = jnp.cos(ang), jnp.sin(ang)
    n = pos.shape[0]
    pad = jnp.zeros((n, DK_A - ROPE_DIM), F32)
    c64 = jnp.concatenate([cos, cos, pad + 1.0], axis=1)
    s1 = jnp.concatenate([-sin, jnp.zeros_like(sin), pad], axis=1)
    s2 = jnp.concatenate([jnp.zeros_like(sin), sin, pad], axis=1)
    tab = jnp.stack([jnp.tile(t, (1, LANES // DK_A)) for t in (c64, s1, s2)])
    return jnp.tile(tab, (1, rows // n, 1))


def _block_diag(w):
    nb, bw, _ = w.shape
    eye = jnp.eye(nb, dtype=w.dtype)
    return (eye[:, None, :, None] * w[:, :, None, :]).reshape(nb * bw, nb * bw)


def _prep_layer(l, norm_w, w_in, w_out, lam_q1, lam_k1, lam_q2, lam_k2, subln_w, ssd_conv_w,
                ssd_conv_b, ssd_dt_bias, ssd_a_log, ssd_d, ssd_norm_w, gla_gk_w2, gla_gk_b,
                gla_norm_w, lru_conv_w, lru_conv_b, lru_wa, lru_ba, lru_wx, lru_bx, lru_lambda):
    w = w_in[l]
    cols = lambda a, n: w[:, a:a + n]
    w_main = jnp.stack([
        cols(_QA, 512), cols(_KA, 512), cols(_VA, 512), cols(_XBC, 512), cols(_XBC + 512, 512),
        cols(_ZA, 512), cols(_ZB, 512), jnp.concatenate([cols(_QC, 256), cols(_KC, 256)], axis=1),
        cols(_VC, 512), cols(_GC, 512), cols(_XD, 512), cols(_ZD, 512)]).astype(BF16)
    w_small = jnp.concatenate([cols(_DT, H_B), cols(_GKLR, GK_RANK),
                               jnp.zeros((D_MODEL, LANES - H_B - GK_RANK), F32)], axis=1).astype(BF16)
    lam_init = 0.8 - 0.6 * math.exp(-0.3 * l)
    lam = (jnp.exp(jnp.sum(lam_q1[l] * lam_k1[l])) - jnp.exp(jnp.sum(lam_q2[l] * lam_k2[l]))
           + lam_init).reshape(1, 1).astype(F32)
    pad_lanes = lambda v: jnp.concatenate([v, jnp.zeros((LANES - v.shape[0],), F32)]).reshape(1, LANES)
    w2p = jnp.zeros((LANES, H_C * DK_C), F32).at[H_B:H_B + GK_RANK].set(gla_gk_w2[l]).astype(BF16)
    return dict(
        norm_w=norm_w[l].reshape(1, D_MODEL), w_main=w_main, w_small=w_small,
        w_out4=w_out[l].reshape(4, W_GROUP, D_MODEL).astype(BF16),
        lam=lam, lam_init=lam_init, subln_w=subln_w[l].reshape(1, DV_A),
        ssd_cw=ssd_conv_w[l], ssd_cb=ssd_conv_b[l].reshape(1, CONV_B_CH),
        ssd_dtb=pad_lanes(ssd_dt_bias[l]), ssd_alog=pad_lanes(ssd_a_log[l]),
        ssd_dvec=jnp.repeat(ssd_d[l], P_B).reshape(1, W_GROUP), ssd_nw=ssd_norm_w[l].reshape(1, W_GROUP),
        gla_w2p=w2p, gla_gb=gla_gk_b[l].reshape(1, H_C * DK_C), gla_nw=gla_norm_w[l].reshape(1, DV_C),
        lru_cw=lru_conv_w[l], lru_cb=lru_conv_b[l].reshape(1, W_D),
        lru_wa=_block_diag(lru_wa[l]).astype(BF16), lru_ba=lru_ba[l].reshape(1, W_D),
        lru_wx=_block_diag(lru_wx[l]).astype(BF16), lru_bx=lru_bx[l].reshape(1, W_D),
        lru_lam=lru_lambda[l].reshape(1, W_D),
    )


def _gla_state_to_pairs(s):
    B = s.shape[0]
    return s.reshape(B, H_C // 2, 2, DK_C, DV_C).transpose(0, 1, 4, 2, 3).reshape(
        B, H_C // 2, DV_C, 2 * DK_C)


def _gla_state_from_pairs(s):
    B = s.shape[0]
    return s.reshape(B, H_C // 2, DV_C, 2, DK_C).transpose(0, 1, 3, 4, 2).reshape(B, H_C, DK_C, DV_C)


def _ssd_args(rest, small, states, p):
    return (rest, small, states[1], states[0], p["ssd_cw"], p["ssd_cb"], p["ssd_dtb"], p["ssd_alog"],
            p["ssd_dvec"], p["ssd_nw"])


def _mixers(rest, small, states, p, B, L, ydt, with_ssd=True):
    _, _, gla_s0, lru_h0, lru_buf = states
    out = {}
    if with_ssd:
        out["yb"], out["ssd_s"], out["ssd_c"] = _ssd(*_ssd_args(rest, small, states, p), B, L, ydt)
    yc, gla_st = _gla(rest, small, _gla_state_to_pairs(gla_s0), p["gla_w2p"], p["gla_gb"],
                      p["gla_nw"], B, L, ydt)
    yd, lru_h, lru_c = _lru(rest, lru_buf, lru_h0.reshape(B, 1, W_D), p["lru_cw"], p["lru_cb"],
                            p["lru_wa"], p["lru_ba"], p["lru_wx"], p["lru_bx"], p["lru_lam"], B, L, ydt)
    out.update(yc=yc, yd=yd, gla_s=_gla_state_from_pairs(gla_st), lru_h=lru_h.reshape(B, W_D),
               lru_c=lru_c)
    return out


def _finish_layer(x2d, ya, m, p, T, final_w, final):
    ys = (ya, m["yb"].reshape(T, W_GROUP), m["yc"].reshape(T, W_GROUP), m["yd"].reshape(T, W_D))
    x_new = _outproj(ys, p["w_out4"], x2d, final_w, final)
    return x_new, (m["ssd_s"], m["ssd_c"], m["gla_s"], m["lru_h"], m["lru_c"])


def _layer_pair(hp, hs, Bp, Lp, Bs, Ls, tab_p, tab_s, p, st_p, st_s, past, layer, depth, kv_p, kv_s,
                final_w, final):
    cache_k, cache_v, page_table = past
    ydt_p = BF16 if Lp % 16 == 0 else F32
    ydt_s = BF16 if Ls % 16 == 0 else F32
    q_p, k_p, v_p, rest_p, small_p, kf_p, vf_p = _inproj(
        hp, p["norm_w"], p["w_main"], p["w_small"], tab_p, Lp, layer, depth, kv_p)
    q_s, k_s, v_s, rest_s, small_s, kf_s, vf_s = _inproj(
        hs, p["norm_w"], p["w_main"], p["w_small"], tab_s, Ls, layer, depth, kv_s)
    yb_p, ssd_s_p, ssd_c_p, ya_s = _ssd_decode(
        _ssd_args(rest_p, small_p, st_p, p), Bp, Lp, ydt_p, p["lam"], q_s, k_s, v_s, rest_s,
        p["subln_w"], cache_k, cache_v, page_table, layer, Bs, Ls, p["lam_init"])
    ya_p = _attn_prompt(p["lam"], q_p, k_p, v_p, rest_p, p["subln_w"], Bp, Lp, p["lam_init"])
    m_p = _mixers(rest_p, small_p, st_p, p, Bp, Lp, ydt_p, with_ssd=False)
    m_p.update(yb=yb_p, ssd_s=ssd_s_p, ssd_c=ssd_c_p)
    m_s = _mixers(rest_s, small_s, st_s, p, Bs, Ls, ydt_s)
    hp, new_p = _finish_layer(hp, ya_p, m_p, p, Bp * Lp, final_w, final)
    hs, new_s = _finish_layer(hs, ya_s, m_s, p, Bs * Ls, final_w, final)
    return hp, hs, new_p, new_s, (kf_p, vf_p), (kf_s, vf_s)


def kernel(x_prompt, x_sample, cache_k, cache_v, page_table, state_ssd, state_ssd_conv, state_gla, state_lru, state_lru_conv, norm_w, w_in, w_out, lam_q1, lam_k1, lam_q2, lam_k2, subln_w, ssd_conv_w, ssd_conv_b, ssd_dt_bias, ssd_a_log, ssd_d, ssd_norm_w, gla_gk_w2, gla_gk_b, gla_norm_w, lru_conv_w, lru_conv_b, lru_wa, lru_ba, lru_wx, lru_bx, lru_lambda, final_norm_w):
    Bp, Lp, _ = x_prompt.shape
    Bs, Ls, _ = x_sample.shape
    depth = w_in.shape[0]
    past_len = page_table.shape[1] * cache_k.shape[2]
    tab_p = _rope_table(jnp.arange(Lp, dtype=jnp.int32), max(Lp, min(INPROJ_ROWS, Bp * Lp)))
    tab_s = _rope_table(past_len + jnp.arange(Ls, dtype=jnp.int32), max(Ls, min(INPROJ_ROWS, Bs * Ls)))
    final_w = final_norm_w.reshape(1, D_MODEL)
    hp = x_prompt.reshape(Bp * Lp, D_MODEL)
    hs = x_sample.reshape(Bs * Ls, D_MODEL)
    sp, ss = [], []
    kv_p = tuple(jnp.zeros((depth, Bp * Lp * H_A, LANES), F32) for _ in range(2))
    kv_s = tuple(jnp.zeros((depth, Bs * Ls * H_A, LANES), F32) for _ in range(2))
    for l in range(depth):
        p = _prep_layer(l, norm_w, w_in, w_out, lam_q1, lam_k1, lam_q2, lam_k2, subln_w, ssd_conv_w,
                        ssd_conv_b, ssd_dt_bias, ssd_a_log, ssd_d, ssd_norm_w, gla_gk_w2, gla_gk_b,
                        gla_norm_w, lru_conv_w, lru_conv_b, lru_wa, lru_ba, lru_wx, lru_bx, lru_lambda)
        final = l == depth - 1
        st_p = (jnp.zeros((Bp, H_B, P_B, N_B), F32), jnp.zeros((Bp, CONV_W - 1, CONV_B_CH), F32),
                jnp.zeros((Bp, H_C, DK_C, DV_C), F32), jnp.zeros((Bp, W_D), F32),
                jnp.zeros((Bp, CONV_W - 1, W_D), F32))
        st_s = (state_ssd[l], state_ssd_conv[l], state_gla[l], state_lru[l], state_lru_conv[l])
        hp, hs, new_p, new_s, kv_p, kv_s = _layer_pair(
            hp, hs, Bp, Lp, Bs, Ls, tab_p, tab_s, p, st_p, st_s, (cache_k, cache_v, page_table), l,
            depth, kv_p, kv_s, final_w, final)
        sp.append(new_p)
        ss.append(new_s)
    outs_p = [x.reshape(depth, Bp, Lp, H_A, LANES) for x in kv_p]
    outs_p += [jnp.stack([s[i] for s in sp]) for i in range(5)]
    outs_s = [x.reshape(depth, Bs, Ls, H_A, LANES) for x in kv_s]
    outs_s += [jnp.stack([s[i] for s in ss]) for i in range(5)]
    return (hp.reshape(Bp, Lp, D_MODEL), hs.reshape(Bs, Ls, D_MODEL), *outs_p, *outs_s)
```

```python
import functools
import math

import jax
import jax.numpy as jnp
from jax import lax
from jax.experimental import pallas as pl
from jax.experimental.pallas import tpu as pltpu

F32 = jnp.float32
BF16 = jnp.bfloat16
EPS = 1e-6
NEG = -1e30
LOG2E = math.log2(math.e)

D_MODEL = 2048
W_GROUP = 512
H_A, DV_A, DK_A = 4, 128, 64
ROPE_DIM, ROPE_THETA = 16, 500000.0
H_B, P_B, N_B, G_B = 8, 64, 128, 2
CONV_W = 4
CONV_B_CH = W_GROUP + 2 * G_B * N_B
H_C, DK_C, DV_C = 4, 64, 128
GK_RANK, GK_NORM = 16, 16.0
W_D, NB_D, BW_D = 512, 8, 64
LRU_C = 8.0

LANES = 128
SUBLANES = 8
COL_BLK = 512
INPROJ_ROWS = 1024
ATTN_BQ = 512
ATTN_BK = 512
MIXER_SEQS = 4
GLA_SEQS = 8
VMEM_LIMIT = 56 * 1024 * 1024

_SPLITS = (512, 512, 512, 512, CONV_B_CH, H_B, W_GROUP, 256, 256, 512, GK_RANK, W_GROUP, W_D, W_D)
_OFF = [0]
for _s in _SPLITS:
    _OFF.append(_OFF[-1] + _s)
(_QA, _KA, _VA, _ZA, _XBC, _DT, _ZB, _QC, _KC, _VC, _GKLR, _GC, _XD, _ZD) = _OFF[:-1]

REST_COLS = 4608


def _cparams(sem):
    return pltpu.CompilerParams(dimension_semantics=sem, vmem_limit_bytes=VMEM_LIMIT)


def _sigmoid(x):
    return jax.nn.sigmoid(x)


def _silu(x):
    return x * _sigmoid(x)


def _softplus(x):
    return jnp.maximum(x, 0.0) + jnp.log1p(jnp.exp(-jnp.abs(x)))


def _rmsnorm_rows(x, w):
    return x * lax.rsqrt(jnp.mean(x * x, axis=-1, keepdims=True) + EPS) * w


def _dot(a, b):
    return jnp.dot(a, b, preferred_element_type=F32)


def _dot_nt(a, b):
    return lax.dot_general(a, b, (((1,), (1,)), ((), ())), preferred_element_type=F32)


def _dot_tn(a, b):
    return lax.dot_general(a, b, (((0,), (0,)), ((), ())), preferred_element_type=F32)


def _dot_exact(a, b):
    return jnp.dot(a, b, preferred_element_type=F32, precision=lax.Precision.HIGHEST)


def _lower_tri(n):
    r = lax.broadcasted_iota(jnp.int32, (n, n), 0)
    c = lax.broadcasted_iota(jnp.int32, (n, n), 1)
    return r >= c


def _batch_group(b, want):
    return math.gcd(b, want)


def _inproj_kernel(x_ref, nw_ref, w_ref, ws_ref, rope_ref, kf_in_ref, vf_in_ref,
                   q_ref, k_ref, v_ref, rest_ref, small_ref, kf_ref, vf_ref, u_scr):
    del kf_in_ref, vf_in_ref
    j = pl.program_id(1)
    tm = x_ref.shape[0]

    @pl.when(j == 0)
    def _():
        u = _rmsnorm_rows(x_ref[...], nw_ref[...]).astype(BF16)
        u_scr[...] = u
        small_ref[...] = _dot(u, ws_ref[...])

    acc = _dot(u_scr[...], w_ref[...])

    def rope_store(dst, scale, dst_heads=None):
        c, s1, s2 = rope_ref[0], rope_ref[1], rope_ref[2]
        for hh in range(COL_BLK // LANES):
            sl = slice(hh * LANES, (hh + 1) * LANES)
            a = acc[:, sl]
            r = (a * c + pltpu.roll(a, LANES - ROPE_DIM // 2, 1) * s1
                 + pltpu.roll(a, ROPE_DIM // 2, 1) * s2)
            dst[:, sl] = r * scale
            if dst_heads is not None:
                dst_heads[pl.ds(hh, tm, stride=H_A), :] = r

    @pl.when(j == 0)
    def _():
        rope_store(q_ref, DK_A ** -0.5 * LOG2E)

    @pl.when(j == 1)
    def _():
        rope_store(k_ref, 1.0, kf_ref)

    @pl.when(j == 2)
    def _():
        v_ref[...] = acc
        for hh in range(H_A):
            vf_ref[pl.ds(hh, tm, stride=H_A), :] = acc[:, hh * LANES:(hh + 1) * LANES]

    @pl.when(j >= 3)
    def _():
        rest_ref[...] = acc


def _inproj(x2d, norm_w, w_main, w_small, rope_tab, seq_len, layer, depth, kv_prev):
    T = x2d.shape[0]
    tm = min(INPROJ_ROWS, T)
    assert T % tm == 0 and (seq_len % tm == 0 or tm % seq_len == 0)
    n_pos_blk = max(seq_len // tm, 1)
    nj = w_main.shape[0]
    kv_shape = jax.ShapeDtypeStruct((depth, T * H_A, LANES), F32)
    out_shape = (
        jax.ShapeDtypeStruct((T, COL_BLK), F32),
        jax.ShapeDtypeStruct((T, COL_BLK), F32),
        jax.ShapeDtypeStruct((T, COL_BLK), F32),
        jax.ShapeDtypeStruct((T, REST_COLS), F32),
        jax.ShapeDtypeStruct((T, LANES), F32),
        kv_shape, kv_shape,
    )
    row_blk = pl.BlockSpec((tm, COL_BLK), lambda i, j: (i, 0))
    kv_blk = pl.BlockSpec((None, tm * H_A, LANES), lambda i, j: (layer, i, 0))
    in_specs = [
        pl.BlockSpec((tm, D_MODEL), lambda i, j: (i, 0)),
        pl.BlockSpec((1, D_MODEL), lambda i, j: (0, 0)),
        pl.BlockSpec((None, D_MODEL, COL_BLK), lambda i, j: (j, 0, 0)),
        pl.BlockSpec((D_MODEL, LANES), lambda i, j: (0, 0)),
        pl.BlockSpec((3, tm, LANES), lambda i, j: (0, i % n_pos_blk, 0)),
    ]
    in_specs += [pl.BlockSpec(memory_space=pl.ANY)] * 2
    args = (x2d, norm_w, w_main, w_small, rope_tab, *kv_prev)
    aliases = {len(args) - 2: 5, len(args) - 1: 6}
    return pl.pallas_call(
        _inproj_kernel,
        grid=(T // tm, nj),
        in_specs=in_specs,
        out_specs=(
            row_blk, row_blk, row_blk,
            pl.BlockSpec((tm, COL_BLK), lambda i, j: (i, jnp.maximum(j - 3, 0))),
            pl.BlockSpec((tm, LANES), lambda i, j: (i, 0)),
            kv_blk, kv_blk,
        ),
        out_shape=out_shape,
        input_output_aliases=aliases,
        scratch_shapes=[pltpu.VMEM((tm, D_MODEL), BF16)],
        compiler_params=_cparams(("arbitrary", "arbitrary")),
        name="inproj",
    )(*args)


def _attn_finish(o, lam_init, w, z):
    o = _rmsnorm_rows(o, w) * (1.0 - lam_init)
    return o * _silu(z)


def _attn_kernel(lam_ref, q_ref, k_ref, v_ref, z_ref, w_ref, o_ref, kb_scr, vt_scr,
                 m_scr, l_scr, acc_scr, *, bq, bk, lam_init):
    qi = pl.program_id(2)
    n_kb = kb_scr.shape[0]
    diag_blocks = bq // bk

    @pl.when(qi == 0)
    def _():
        for c in range(n_kb):
            rows = slice(c * bk, (c + 1) * bk)
            kb_scr[c] = k_ref[rows, :].astype(BF16)
            vt_scr[c] = v_ref[rows, :].T.astype(BF16)

    qt = q_ref[...].T
    feat = lax.broadcasted_iota(jnp.int32, qt.shape, 0)
    qts = (jnp.where(feat < DK_A, qt, 0.0).astype(BF16),
           jnp.where(feat >= DK_A, qt, 0.0).astype(BF16))
    m_scr[...] = jnp.full(m_scr.shape, NEG, F32)
    l_scr[...] = jnp.zeros(l_scr.shape, F32)
    acc_scr[...] = jnp.zeros(acc_scr.shape, F32)

    halves = 2
    hw = bq // halves

    def step(j, mask):
        kb = kb_scr[j]
        vtb = vt_scr[j]
        chains = [(m, hf) for hf in range(halves) for m in range(2)]
        n_keys = [(hf + 1) * hw if (mask is not None and diag_blocks == 1) else bk
                  for _, hf in chains]
        ss = [_dot(kb[:nk], qts[m][:, hf * hw:(hf + 1) * hw])
              for (m, hf), nk in zip(chains, n_keys)]
        for (m, hf), nk, s in zip(chains, n_keys, ss):
            cols = slice(hf * hw, (hf + 1) * hw)
            if mask is not None:
                s = jnp.where(mask[:nk, cols], s, NEG)
            mx = m_scr[m, :, cols]
            mn = jnp.maximum(mx, jnp.max(s, axis=0, keepdims=True))
            alpha = jnp.exp2(mx - mn)
            p = jnp.exp2(s - mn)
            l_scr[m, :, cols] = alpha * l_scr[m, :, cols] + jnp.sum(p, axis=0, keepdims=True)
            acc_scr[m, :, cols] = alpha * acc_scr[m, :, cols] + _dot(vtb[:, :nk], p.astype(BF16))
            m_scr[m, :, cols] = mn

    def unmasked(j, carry):
        step(j, None)
        return carry

    lax.fori_loop(0, qi * diag_blocks, unmasked, 0)
    key_i = lax.broadcasted_iota(jnp.int32, (bk, bq), 0)
    qry_i = lax.broadcasted_iota(jnp.int32, (bk, bq), 1)
    for d in range(diag_blocks):
        step(qi * diag_blocks + d, key_i + d * bk <= qry_i)
    lam = lam_ref[0, 0]
    o = (acc_scr[0] / l_scr[0] - lam * (acc_scr[1] / l_scr[1])).T
    o_ref[...] = _attn_finish(o, lam_init, w_ref[...], z_ref[...]).astype(o_ref.dtype)


def _attn_prompt(lam, q, k, v, rest, subln_w, B, L, lam_init):
    bq = min(ATTN_BQ, L)
    bk = min(ATTN_BK, bq)
    nq = L // bq
    kern = functools.partial(_attn_kernel, bq=bq, bk=bk, lam_init=lam_init)
    return pl.pallas_call(
        kern,
        grid=(B, H_A, nq),
        in_specs=[
            pl.BlockSpec(memory_space=pltpu.SMEM),
            pl.BlockSpec((bq, LANES), lambda b, h, i: (b * nq + i, h)),
            pl.BlockSpec((L, LANES), lambda b, h, i: (b, h)),
            pl.BlockSpec((L, LANES), lambda b, h, i: (b, h)),
            pl.BlockSpec((bq, LANES), lambda b, h, i: (b * nq + i, 8 + h)),
            pl.BlockSpec((1, LANES), lambda b, h, i: (0, 0)),
        ],
        out_specs=pl.BlockSpec((bq, LANES), lambda b, h, i: (b * nq + i, h)),
        out_shape=jax.ShapeDtypeStruct((B * L, W_GROUP), BF16),
        scratch_shapes=[
            pltpu.VMEM((L // bk, bk, LANES), BF16),
            pltpu.VMEM((L // bk, DV_A, bk), BF16),
            pltpu.VMEM((2, 1, bq), F32),
            pltpu.VMEM((2, 1, bq), F32),
            pltpu.VMEM((2, DV_A, bq), F32),
        ],
        compiler_params=_cparams(("arbitrary", "arbitrary", "arbitrary")),
        name="attn_prompt",
    )(lam, q, k, v, rest, subln_w)


def _decode_init(q_ref, qrow_scr, m_scr, l_scr, acc_scr, ls):
    rows = 2 * ls
    r = lax.broadcasted_iota(jnp.int32, (rows, LANES), 0)
    c = lax.broadcasted_iota(jnp.int32, (rows, LANES), 1)
    own_map = (r // ls) == (c // DK_A)
    for h in range(H_A):
        qh = q_ref[:, h * LANES:(h + 1) * LANES]
        qrow_scr[h] = jnp.where(own_map, jnp.concatenate([qh, qh], axis=0), 0.0).astype(BF16)
    m_scr[...] = jnp.full(m_scr.shape, NEG, F32)
    l_scr[...] = jnp.zeros(l_scr.shape, F32)
    acc_scr[...] = jnp.zeros(acc_scr.shape, F32)


def _decode_head_rows(pages, h, page):
    head_rows = pl.ds(h, page, stride=H_A)
    return jnp.concatenate([pg[head_rows, :] for pg in pages], axis=0).astype(BF16)


def _decode_accumulate(h, s, vv, m_scr, l_scr, acc_scr):
    mx = m_scr[h]
    mn = jnp.maximum(mx, jnp.max(s, axis=-1, keepdims=True))
    alpha = jnp.exp2(mx - mn)
    pr = jnp.exp2(s - mn)
    l_scr[h] = alpha * l_scr[h] + jnp.sum(pr, axis=-1, keepdims=True)
    acc_scr[h] = alpha * acc_scr[h] + _dot(pr.astype(BF16), vv)
    m_scr[h] = mn


def _decode_finish(lam_ref, kn_ref, vn_ref, z_ref, w_ref, o_ref, qrow_scr, m_scr, l_scr, acc_scr,
                   ls, lam_init):
    rows = 2 * ls
    lam = lam_ref[0, 0]
    w = w_ref[...]
    jj = lax.broadcasted_iota(jnp.int32, (rows, ls), 1)
    qq = lax.broadcasted_iota(jnp.int32, (rows, ls), 0) % ls
    for h in range(H_A):
        cs = slice(h * LANES, (h + 1) * LANES)
        s = _dot_nt(qrow_scr[h], kn_ref[:, cs].astype(BF16))
        _decode_accumulate(h, jnp.where(jj <= qq, s, NEG), vn_ref[:, cs].astype(BF16),
                           m_scr, l_scr, acc_scr)
        o = acc_scr[h] / l_scr[h]
        o = o[:ls] - lam * o[ls:]
        o_ref[:, cs] = _attn_finish(o, lam_init, w, z_ref[:, cs]).astype(o_ref.dtype)


CONV_PAD = SUBLANES
CONV_TAIL = CONV_PAD - (CONV_W - 1)


def _conv_step(x_ref, cw_ref, cb_ref, xp_scr, q):
    groups = q // SUBLANES
    x = x_ref[...]
    ch = x.shape[-1]
    x3 = x.reshape(groups, SUBLANES, ch)
    prev = xp_scr[...].reshape(1, SUBLANES, ch)
    r8 = lax.broadcasted_iota(jnp.int32, x3.shape, 1)

    def back(s):
        cur = pltpu.roll(x3, s, 1)
        before = pltpu.roll(prev, s, 1)
        if groups > 1:
            before = jnp.concatenate([before, cur[:-1]], axis=0)
        return jnp.where(r8 >= s, cur, before)

    y = back(CONV_W - 1) * cw_ref[0:1, :]
    for t in range(1, CONV_W - 1):
        y = y + back(CONV_W - 1 - t) * cw_ref[t:t + 1, :]
    y = y + x3 * cw_ref[CONV_W - 1:CONV_W, :]
    y = y + cb_ref[...]
    xp_scr[...] = x[q - SUBLANES:, :]
    return y.reshape(q, ch)


def _ssd_inputs(xbc_ref, dt_ref, cw_ref, cb_ref, dtb_ref, xp_scr, q, nb):
    xbc = [_silu(_conv_step(xbc_ref.at[b], cw_ref, cb_ref, xp_scr.at[b], q)) for b in range(nb)]
    dt = [_softplus(dt_ref[b] + dtb_ref[...]) for b in range(nb)]
    return xbc, dt


def _ssd_chunk_pairs(xbc_all, dt_all, zb_ref, alog_ref, dvec_ref, nw_ref, y_ref, s_scr, y_scr, q, nb):
    half = LANES // 2
    row = lax.broadcasted_iota(jnp.int32, (q, LANES), 0)
    lane = lax.broadcasted_iota(jnp.int32, (q, LANES), 1)
    hi = lane >= half
    tri2 = row >= lane % half
    a = -jnp.exp(alog_ref[...])
    tri_f = _lower_tri(q).astype(F32)
    heads_per_group = H_B // G_B
    zeros_state = jnp.zeros((P_B, N_B), BF16)
    for b in range(nb):
        xbc, dt = xbc_all[b], dt_all[b]
        xs = xbc[:, :W_GROUP]
        cum = _dot_exact(tri_f, dt * a)
        cum_t = cum.T
        dt_t = dt.T
        last = cum[q - 1:q, :]
        ecum = jnp.exp(cum)
        wj = jnp.exp(last - cum) * dt
        elast = jnp.exp(last)
        for g in range(G_B):
            bgb = xbc[:, W_GROUP + g * N_B:W_GROUP + (g + 1) * N_B].astype(BF16)
            cg = xbc[:, W_GROUP + (G_B + g) * N_B:W_GROUP + (G_B + g + 1) * N_B]
            cb2 = _dot_nt(cg.astype(BF16), jnp.concatenate([bgb, bgb], axis=0))
            for pair in range(heads_per_group // 2):
                ha = g * heads_per_group + 2 * pair
                hb = ha + 1
                lanes = slice(ha * P_B, (hb + 1) * P_B)
                pick = lambda v: jnp.where(hi, v[:, hb:hb + 1], v[:, ha:ha + 1])
                cj = jnp.concatenate([cum_t[ha:ha + 1, :], cum_t[hb:hb + 1, :]], axis=1)
                dtj = jnp.concatenate([dt_t[ha:ha + 1, :], dt_t[hb:hb + 1, :]], axis=1)
                dec = jnp.exp(jnp.where(tri2, pick(cum) - cj, NEG))
                mh = (cb2 * dec * dtj).astype(BF16)
                xp = xs[:, lanes]
                x_bd = jnp.concatenate([jnp.where(hi, 0.0, xp), jnp.where(hi, xp, 0.0)],
                                       axis=0).astype(BF16)
                sa, sb = s_scr[b, ha], s_scr[b, hb]
                s_bd = jnp.concatenate(
                    [jnp.concatenate([sa.astype(BF16), zeros_state], axis=1),
                     jnp.concatenate([zeros_state, sb.astype(BF16)], axis=1)], axis=0)
                c_dec = jnp.concatenate([cg * ecum[:, ha:ha + 1], cg * ecum[:, hb:hb + 1]],
                                        axis=1).astype(BF16)
                y_scr[b, :, lanes] = _dot(mh, x_bd) + _dot_nt(c_dec, s_bd)
                upd = _dot_tn((xp * pick(wj)).astype(BF16), bgb)
                s_scr[b, ha] = elast[:, ha:ha + 1] * sa + upd[:P_B]
                s_scr[b, hb] = elast[:, hb:hb + 1] * sb + upd[P_B:]
        y = y_scr[b] + dvec_ref[...] * xs
        y_ref[b] = _rmsnorm_rows(y * _silu(zb_ref[b]), nw_ref[...]).astype(y_ref.dtype)


def _ssd_chunk(xbc_all, dt_all, zb_ref, alog_ref, dvec_ref, nw_ref, y_ref, s_scr, y_scr, q, nb):
    if 2 * q == LANES and 2 * P_B == LANES:
        return _ssd_chunk_pairs(xbc_all, dt_all, zb_ref, alog_ref, dvec_ref, nw_ref, y_ref, s_scr,
                                y_scr, q, nb)
    tri = _lower_tri(q)
    tri_f = tri.astype(F32)
    a = -jnp.exp(alog_ref[...])
    heads_per_group = H_B // G_B
    for b in range(nb):
        xbc, dt = xbc_all[b], dt_all[b]
        xs = xbc[:, :W_GROUP]
        cum = _dot_exact(tri_f, dt * a)
        cum_t = cum.T
        dt_t = dt.T
        last = cum[q - 1:q, :]
        ecum = jnp.exp(cum)
        wj = jnp.exp(last - cum) * dt
        elast = jnp.exp(last)
        for g in range(G_B):
            bg = xbc[:, W_GROUP + g * N_B:W_GROUP + (g + 1) * N_B]
            cg = xbc[:, W_GROUP + (G_B + g) * N_B:W_GROUP + (G_B + g + 1) * N_B]
            bgb = bg.astype(BF16)
            cb_g = _dot_nt(cg.astype(BF16), bgb)
            for hh in range(heads_per_group):
                h = g * heads_per_group + hh
                hs = slice(h * P_B, (h + 1) * P_B)
                seg = cum[:, h:h + 1] - cum_t[h:h + 1, :]
                dec = jnp.exp(jnp.where(tri, seg, NEG))
                mh = (cb_g * dec * dt_t[h:h + 1, :]).astype(BF16)
                xh = xs[:, hs]
                s_old = s_scr[b, h]
                yh = _dot(mh, xh.astype(BF16)) + _dot_nt((cg * ecum[:, h:h + 1]).astype(BF16),
                                                         s_old.astype(BF16))
                y_scr[b, :, hs] = yh
                xw = (xh * wj[:, h:h + 1]).astype(BF16)
                s_scr[b, h] = elast[:, h:h + 1] * s_old + _dot_tn(xw, bgb)
        y = y_scr[b] + dvec_ref[...] * xs
        y_ref[b] = _rmsnorm_rows(y * _silu(zb_ref[b]), nw_ref[...]).astype(y_ref.dtype)


def _ssd_kernel(xbc_ref, dt_ref, zb_ref, buf_ref, s0_ref, cw_ref, cb_ref, dtb_ref, alog_ref,
                dvec_ref, nw_ref, y_ref, sout_ref, cout_ref, xp_scr, s_scr, y_scr, *, q, nb):
    c = pl.program_id(1)

    @pl.when(c == 0)
    def _():
        s_scr[...] = s0_ref[...]
        xp_scr[...] = jnp.zeros(xp_scr.shape, F32)
        xp_scr[:, CONV_TAIL:CONV_PAD, :] = buf_ref[...]

    xbc, dt = _ssd_inputs(xbc_ref, dt_ref, cw_ref, cb_ref, dtb_ref, xp_scr, q, nb)
    _ssd_chunk(xbc, dt, zb_ref, alog_ref, dvec_ref, nw_ref, y_ref, s_scr, y_scr, q, nb)

    @pl.when(c == pl.num_programs(1) - 1)
    def _():
        sout_ref[...] = s_scr[...]
        cout_ref[...] = xp_scr[:, CONV_TAIL:CONV_PAD, :]


def _ssd_decode_kernel(pt_ref, lam_ref, xbc_ref, dt_ref, zb_ref, buf_ref, s0_ref, cw_ref, cb_ref,
                       dtb_ref, alog_ref, dvec_ref, nw_ref, q_ref, kn_ref, vn_ref, z_ref, w_ref,
                       ck_ref, cv_ref, y_ref, sout_ref, cout_ref, o_ref,
                       xp_scr, s_scr, y_scr, qrow_scr, m_scr, l_scr, acc_scr, kbuf, vbuf, sem,
                       *, q, nb, n_pp, steps_per_seq, ls, page, layer, lam_init):
    c = pl.program_id(1)
    t = pl.program_id(0) * pl.num_programs(1) + c
    n_steps = pl.num_programs(0) * pl.num_programs(1)
    part = t % steps_per_seq
    slot = t % 2

    def page_copies(step, buf_slot):
        seq = step // steps_per_seq
        first = (step % steps_per_seq) * n_pp
        pages = [pt_ref[seq, first + i] for i in range(n_pp)]
        k_copies = [pltpu.make_async_copy(ck_ref.at[layer, pg], kbuf.at[buf_slot, i],
                                          sem.at[buf_slot, 0]) for i, pg in enumerate(pages)]
        v_copies = [pltpu.make_async_copy(cv_ref.at[layer, pg], vbuf.at[buf_slot, i],
                                          sem.at[buf_slot, 1]) for i, pg in enumerate(pages)]
        return k_copies, v_copies

    @pl.when(t == 0)
    def _():
        k_copies, v_copies = page_copies(0, 0)
        for cp in k_copies + v_copies:
            cp.start()

    @pl.when(c == 0)
    def _():
        s_scr[...] = s0_ref[...]
        xp_scr[...] = jnp.zeros(xp_scr.shape, F32)
        xp_scr[:, CONV_TAIL:CONV_PAD, :] = buf_ref[...]

    @pl.when(part == 0)
    def _():
        _decode_init(q_ref, qrow_scr, m_scr, l_scr, acc_scr, ls)

    k_now, v_now = page_copies(t, slot)
    nxt = jnp.minimum(t + 1, n_steps - 1)
    k_next, v_next = page_copies(nxt, 1 - slot)

    kp = [kbuf.at[slot, i] for i in range(n_pp)]
    vp = [vbuf.at[slot, i] for i in range(n_pp)]
    for cp in k_next + v_next:
        cp.start()
    for cp in k_now + v_now:
        cp.wait()
    scores = [_dot_nt(qrow_scr[h], _decode_head_rows(kp, h, page)) for h in range(H_A)]
    xbc, dt = _ssd_inputs(xbc_ref, dt_ref, cw_ref, cb_ref, dtb_ref, xp_scr, q, nb)
    for h in range(H_A):
        _decode_accumulate(h, scores[h], _decode_head_rows(vp, h, page), m_scr, l_scr, acc_scr)
    _ssd_chunk(xbc, dt, zb_ref, alog_ref, dvec_ref, nw_ref, y_ref, s_scr, y_scr, q, nb)

    @pl.when(c == pl.num_programs(1) - 1)
    def _():
        sout_ref[...] = s_scr[...]
        cout_ref[...] = xp_scr[:, CONV_TAIL:CONV_PAD, :]

    @pl.when(part == steps_per_seq - 1)
    def _():
        _decode_finish(lam_ref, kn_ref, vn_ref, z_ref, w_ref, o_ref, qrow_scr, m_scr, l_scr,
                       acc_scr, ls, lam_init)

    @pl.when(t == n_steps - 1)
    def _():
        for cp in k_next + v_next:
            cp.wait()


def _ssd_specs(B, L, nb, q, index):
    full = lambda shape: pl.BlockSpec(shape, index(lambda b, c: (0,) * len(shape)))
    in_specs = [
        pl.BlockSpec((nb, q, CONV_B_CH), index(lambda b, c: (b, c, 0))),
        pl.BlockSpec((nb, q, LANES), index(lambda b, c: (b, c, 0))),
        pl.BlockSpec((nb, q, W_GROUP), index(lambda b, c: (b, c, 3))),
        pl.BlockSpec((nb, CONV_W - 1, CONV_B_CH), index(lambda b, c: (b, 0, 0))),
        pl.BlockSpec((nb, H_B, P_B, N_B), index(lambda b, c: (b, 0, 0, 0))),
        full((CONV_W, CONV_B_CH)), full((1, CONV_B_CH)), full((1, LANES)), full((1, LANES)),
        full((1, W_GROUP)), full((1, W_GROUP)),
    ]
    out_specs = [
        pl.BlockSpec((nb, q, W_GROUP), index(lambda b, c: (b, c, 0))),
        pl.BlockSpec((nb, H_B, P_B, N_B), index(lambda b, c: (b, 0, 0, 0))),
        pl.BlockSpec((nb, CONV_W - 1, CONV_B_CH), index(lambda b, c: (b, 0, 0))),
    ]
    scratch = [
        pltpu.VMEM((nb, CONV_PAD, CONV_B_CH), F32),
        pltpu.VMEM((nb, H_B, P_B, N_B), F32),
        pltpu.VMEM((nb, q, W_GROUP), F32),
    ]
    return in_specs, out_specs, scratch


def _ssd_out_shape(B, L, out_dtype):
    return [
        jax.ShapeDtypeStruct((B, L, W_GROUP), out_dtype),
        jax.ShapeDtypeStruct((B, H_B, P_B, N_B), F32),
        jax.ShapeDtypeStruct((B, CONV_W - 1, CONV_B_CH), F32),
    ]


def _ssd(rest, small, buf, s0, cw, cb, dtb, alog, dvec, nw, B, L, out_dtype):
    q = math.gcd(L, 64)
    nc = L // q
    nb = _batch_group(B, MIXER_SEQS)
    in_specs, out_specs, scratch = _ssd_specs(B, L, nb, q, lambda f: f)
    rest3 = rest.reshape(B, L, REST_COLS)
    return pl.pallas_call(
        functools.partial(_ssd_kernel, q=q, nb=nb),
        grid=(B // nb, nc),
        in_specs=in_specs,
        out_specs=tuple(out_specs),
        out_shape=tuple(_ssd_out_shape(B, L, out_dtype)),
        scratch_shapes=scratch,
        compiler_params=_cparams(("arbitrary", "arbitrary")),
        name="ssd",
    )(rest3, small.reshape(B, L, LANES), rest3, buf, s0, cw, cb, dtb, alog, dvec, nw)


def _ssd_decode(ssd_args, B, L, out_dtype, lam, q_s, k_s, v_s, rest_s, subln_w, cache_k, cache_v,
                page_table, layer, Bs, Ls, lam_init):
    rest, small, buf, s0, cw, cb, dtb, alog, dvec, nw = ssd_args
    q = math.gcd(L, 64)
    nc = L // q
    nb = _batch_group(B, MIXER_SEQS)
    steps = (B // nb) * nc
    n_pages = page_table.shape[1]
    page = cache_k.shape[2]
    assert (Bs * n_pages) % steps == 0, "decode pages must spread evenly over the SSD grid"
    n_pp = Bs * n_pages // steps
    assert n_pages % n_pp == 0
    steps_per_seq = n_pages // n_pp
    rows = 2 * Ls
    ck = cache_k.reshape(cache_k.shape[0], cache_k.shape[1], page * H_A, 2 * DK_A)
    cv = cache_v.reshape(cache_v.shape[0], cache_v.shape[1], page * H_A, DV_A)

    def step_of(b, c):
        return b * nc + c

    with_pt = lambda f: (lambda b, c, pt: f(b, c))
    ssd_in, ssd_out, ssd_scratch = _ssd_specs(B, L, nb, q, with_pt)
    tok = pl.BlockSpec((Ls, W_GROUP), lambda b, c, pt: (step_of(b, c) // steps_per_seq, 0))
    kern = functools.partial(_ssd_decode_kernel, q=q, nb=nb, n_pp=n_pp, steps_per_seq=steps_per_seq,
                             ls=Ls, page=page, layer=layer, lam_init=lam_init)
    grid_spec = pltpu.PrefetchScalarGridSpec(
        num_scalar_prefetch=1,
        grid=(B // nb, nc),
        in_specs=[pl.BlockSpec(memory_space=pltpu.SMEM)] + ssd_in + [
            tok, tok, tok,
            pl.BlockSpec((Ls, W_GROUP), lambda b, c, pt: (step_of(b, c) // steps_per_seq, 2)),
            pl.BlockSpec((1, LANES), lambda b, c, pt: (0, 0)),
            pl.BlockSpec(memory_space=pl.ANY),
            pl.BlockSpec(memory_space=pl.ANY),
        ],
        out_specs=tuple(ssd_out + [tok]),
        scratch_shapes=ssd_scratch + [
            pltpu.VMEM((H_A, rows, LANES), BF16),
            pltpu.VMEM((H_A, rows, 1), F32),
            pltpu.VMEM((H_A, rows, 1), F32),
            pltpu.VMEM((H_A, rows, DV_A), F32),
            pltpu.VMEM((2, n_pp, page * H_A, LANES), F32),
            pltpu.VMEM((2, n_pp, page * H_A, LANES), F32),
            pltpu.SemaphoreType.DMA((2, 2)),
        ],
    )
    rest3 = rest.reshape(B, L, REST_COLS)
    return pl.pallas_call(
        kern,
        grid_spec=grid_spec,
        out_shape=tuple(_ssd_out_shape(B, L, out_dtype)
                        + [jax.ShapeDtypeStruct((Bs * Ls, W_GROUP), F32)]),
        compiler_params=_cparams(("arbitrary", "arbitrary")),
        name="ssd_decode",
    )(page_table, lam, rest3, small.reshape(B, L, LANES), rest3, buf, s0, cw, cb, dtb, alog, dvec, nw,
      q_s, k_s, v_s, rest_s, subln_w, ck, cv)


def _gla_kernel(qk_ref, v_ref, gc_ref, sm_ref, s0_ref, w2_ref, gb_ref, nw_ref,
                y_ref, sout_ref, st_scr, *, q, nb):
    c = pl.program_id(1)

    @pl.when(c == 0)
    def _():
        st_scr[...] = s0_ref[...]

    seqs = range(nb)
    pairs = range(H_C // 2)
    width = H_C * DK_C
    nw = nw_ref[...]
    hi = lax.broadcasted_iota(jnp.int32, (q, LANES), 1) >= DK_C
    row2 = lax.broadcasted_iota(jnp.int32, (2 * q, q), 0)
    col2 = lax.broadcasted_iota(jnp.int32, (2 * q, q), 1)
    causal2 = jnp.where(row2 >= q, row2 - q, row2) >= col2
    sm = sm_ref[...].reshape(nb * q, LANES).astype(BF16)
    gkl = _dot(sm, w2_ref[...]) + gb_ref[...]
    g = (jnp.minimum(gkl, 0.0) - jnp.log1p(jnp.exp(-jnp.abs(gkl)))) * (1.0 / GK_NORM)
    bc_all = _dot_exact(_lower_tri(q).astype(F32),
                        jnp.concatenate([g[b * q:(b + 1) * q] for b in seqs], axis=1))
    bc = [bc_all[:, b * width:(b + 1) * width] for b in seqs]
    last = [x[q - 1:q, :] for x in bc]
    kc = [qk_ref[b, :, width:] for b in seqs]
    qe = [qk_ref[b, :, :width] * (DK_C ** -0.5) * jnp.exp(bc[b]) for b in seqs]
    ke = [(kc[b] * jnp.exp(-bc[b])).astype(BF16) for b in seqs]
    kl = [kc[b] * jnp.exp(last[b] - bc[b]) for b in seqs]
    el = [jnp.exp(x) for x in last]
    lanes = [slice(p * LANES, (p + 1) * LANES) for p in pairs]
    vs = [slice(h * DV_C, (h + 1) * DV_C) for h in range(H_C)]
    q2 = [[jnp.concatenate([jnp.where(hi, 0.0, qe[b][:, lanes[p]]),
                            jnp.where(hi, qe[b][:, lanes[p]], 0.0)], axis=0).astype(BF16)
           for p in pairs] for b in seqs]
    kl_a = [[jnp.where(hi, 0.0, kl[b][:, lanes[p]]).astype(BF16) for p in pairs] for b in seqs]
    kl_b = [[jnp.where(hi, kl[b][:, lanes[p]], 0.0).astype(BF16) for p in pairs] for b in seqs]
    vh = [[v_ref[b, :, vs[h]].astype(BF16) for h in range(H_C)] for b in seqs]
    st_old = [[st_scr[b, p] for p in pairs] for b in seqs]
    att = [[_dot_nt(q2[b][p], ke[b][:, lanes[p]]) for p in pairs] for b in seqs]
    o_state = [[_dot_nt(q2[b][p], st_old[b][p].astype(BF16)) for p in pairs] for b in seqs]
    upd = [[_dot_tn(vh[b][2 * p], kl_a[b][p]) + _dot_tn(vh[b][2 * p + 1], kl_b[b][p])
            for p in pairs] for b in seqs]
    for b in seqs:
        for p in pairs:
            attm = jnp.where(causal2, att[b][p], 0.0).astype(BF16)
            st_scr[b, p] = el[b][:, lanes[p]] * st_old[b][p] + upd[b][p]
            for j in range(2):
                h = 2 * p + j
                rows = slice(j * q, (j + 1) * q)
                o = _dot(attm[rows], vh[b][h]) + o_state[b][p][rows]
                y_ref[b, :, vs[h]] = (_rmsnorm_rows(o, nw)
                                      * _silu(gc_ref[b, :, vs[h]])).astype(y_ref.dtype)

    @pl.when(c == pl.num_programs(1) - 1)
    def _():
        sout_ref[...] = st_scr[...]


def _gla(rest, small, s0_t, w2p, gb, nw, B, L, out_dtype):
    q = min(L, 64)
    nc = L // q
    nb = _batch_group(B, GLA_SEQS)
    kern = functools.partial(_gla_kernel, q=q, nb=nb)
    full = lambda shape: pl.BlockSpec(shape, lambda b, c: (0,) * len(shape))
    state = pl.BlockSpec((nb, H_C // 2, DV_C, 2 * DK_C), lambda b, c: (b, 0, 0, 0))
    rest3 = rest.reshape(B, L, REST_COLS)
    return pl.pallas_call(
        kern,
        grid=(B // nb, nc),
        in_specs=[
            pl.BlockSpec((nb, q, W_GROUP), lambda b, c: (b, c, 4)),
            pl.BlockSpec((nb, q, W_GROUP), lambda b, c: (b, c, 5)),
            pl.BlockSpec((nb, q, W_GROUP), lambda b, c: (b, c, 6)),
            pl.BlockSpec((nb, q, LANES), lambda b, c: (b, c, 0)),
            state,
            full((LANES, H_C * DK_C)), full((1, H_C * DK_C)), full((1, DV_C)),
        ],
        out_specs=(pl.BlockSpec((nb, q, W_GROUP), lambda b, c: (b, c, 0)), state),
        out_shape=(
            jax.ShapeDtypeStruct((B, L, W_GROUP), out_dtype),
            jax.ShapeDtypeStruct((B, H_C // 2, DV_C, 2 * DK_C), F32),
        ),
        scratch_shapes=[pltpu.VMEM((nb, H_C // 2, DV_C, 2 * DK_C), F32)],
        compiler_params=_cparams(("arbitrary", "arbitrary")),
        name="gla",
    )(rest3, rest3, rest3, small.reshape(B, L, LANES), s0_t, w2p, gb, nw)


def _lru_kernel(xd_ref, zd_ref, buf_ref, h0_ref, cw_ref, cb_ref, wa_ref, ba_ref, wx_ref, bx_ref,
                lam_ref, y_ref, hout_ref, cout_ref, xp_scr, h_scr, a_scr, u_scr, *, q, nb):
    c = pl.program_id(1)

    @pl.when(c == 0)
    def _():
        h_scr[...] = h0_ref[...]
        xp_scr[...] = jnp.zeros(xp_scr.shape, F32)
        xp_scr[:, CONV_TAIL:CONV_PAD, :] = buf_ref[...]

    sp_lam = _softplus(-lam_ref[...])
    for b in range(nb):
        xr = _conv_step(xd_ref.at[b], cw_ref, cb_ref, xp_scr.at[b], q)
        xb = xr.astype(BF16)
        r = _sigmoid(_dot(xb, wa_ref[...]) + ba_ref[...])
        i = _sigmoid(_dot(xb, wx_ref[...]) + bx_ref[...])
        log_a = -LRU_C * r * sp_lam
        a = jnp.exp(log_a)
        th = jnp.tanh(log_a)
        u = jnp.sqrt(-2.0 * th / (1.0 - th)) * (i * xr)
        a = a.reshape(q // SUBLANES, SUBLANES, W_D)
        u = u.reshape(q // SUBLANES, SUBLANES, W_D)
        r8 = lax.broadcasted_iota(jnp.int32, a.shape, 1)
        for s in (1, 2, 4):
            keep = r8 >= s
            a_sh = pltpu.roll(a, s, 1)
            u_sh = pltpu.roll(u, s, 1)
            u = jnp.where(keep, a * u_sh + u, u)
            a = jnp.where(keep, a * a_sh, a)
        a_scr[b] = a.reshape(q, W_D)
        u_scr[b] = u.reshape(q, W_D)
    carry = [h_scr[b] for b in range(nb)]
    for gi in range(q // SUBLANES):
        rows = slice(gi * SUBLANES, (gi + 1) * SUBLANES)
        for b in range(nb):
            hg = a_scr[b, rows, :] * carry[b] + u_scr[b, rows, :]
            y_ref[b, rows, :] = (hg * _silu(zd_ref[b, rows, :])).astype(y_ref.dtype)
            carry[b] = hg[SUBLANES - 1:SUBLANES, :]
    for b in range(nb):
        h_scr[b] = carry[b]

    @pl.when(c == pl.num_programs(1) - 1)
    def _():
        hout_ref[...] = h_scr[...]
        cout_ref[...] = xp_scr[:, CONV_TAIL:CONV_PAD, :]


def _lru(rest, buf, h0, cw, cb, wa, ba, wx, bx, lam, B, L, out_dtype):
    q = min(L, 256)
    nc = L // q
    nb = _batch_group(B, MIXER_SEQS)
    kern = functools.partial(_lru_kernel, q=q, nb=nb)
    full = lambda shape: pl.BlockSpec(shape, lambda b, c: (0,) * len(shape))
    rest3 = rest.reshape(B, L, REST_COLS)
    return pl.pallas_call(
        kern,
        grid=(B // nb, nc),
        in_specs=[
            pl.BlockSpec((nb, q, W_D), lambda b, c: (b, c, 7)),
            pl.BlockSpec((nb, q, W_D), lambda b, c: (b, c, 8)),
            pl.BlockSpec((nb, CONV_W - 1, W_D), lambda b, c: (b, 0, 0)),
            pl.BlockSpec((nb, 1, W_D), lambda b, c: (b, 0, 0)),
            full((CONV_W, W_D)), full((1, W_D)),
            full((W_D, W_D)), full((1, W_D)), full((W_D, W_D)), full((1, W_D)), full((1, W_D)),
        ],
        out_specs=(
            pl.BlockSpec((nb, q, W_D), lambda b, c: (b, c, 0)),
            pl.BlockSpec((nb, 1, W_D), lambda b, c: (b, 0, 0)),
            pl.BlockSpec((nb, CONV_W - 1, W_D), lambda b, c: (b, 0, 0)),
        ),
        out_shape=(
            jax.ShapeDtypeStruct((B, L, W_D), out_dtype),
            jax.ShapeDtypeStruct((B, 1, W_D), F32),
            jax.ShapeDtypeStruct((B, CONV_W - 1, W_D), F32),
        ),
        scratch_shapes=[
            pltpu.VMEM((nb, CONV_PAD, W_D), F32),
            pltpu.VMEM((nb, 1, W_D), F32),
            pltpu.VMEM((nb, q, W_D), F32),
            pltpu.VMEM((nb, q, W_D), F32),
        ],
        compiler_params=_cparams(("arbitrary", "arbitrary")),
        name="lru",
    )(rest3, rest3, buf, h0, cw, cb, wa, ba, wx, bx, lam)


def _outproj_kernel(ya_ref, yb_ref, yc_ref, yd_ref, w_ref, x_ref, fw_ref, o_ref, *, final):
    out = _dot(ya_ref[...].astype(BF16), w_ref[0])
    for g, y_ref in enumerate((yb_ref, yc_ref, yd_ref), start=1):
        out = out + _dot(y_ref[...].astype(BF16), w_ref[g])
    res = x_ref[...] + out
    if final:
        res = _rmsnorm_rows(res, fw_ref[...])
    o_ref[...] = res


def _outproj(ys, w_out4, x2d, final_w, final):
    T = x2d.shape[0]
    tm = min(512, T)
    kern = functools.partial(_outproj_kernel, final=final)
    yspec = pl.BlockSpec((tm, W_GROUP), lambda i: (i, 0))
    return pl.pallas_call(
        kern,
        grid=(T // tm,),
        in_specs=[yspec] * 4 + [
            pl.BlockSpec((4, W_GROUP, D_MODEL), lambda i: (0, 0, 0)),
            pl.BlockSpec((tm, D_MODEL), lambda i: (i, 0)),
            pl.BlockSpec((1, D_MODEL), lambda i: (0, 0)),
        ],
        out_specs=pl.BlockSpec((tm, D_MODEL), lambda i: (i, 0)),
        out_shape=jax.ShapeDtypeStruct((T, D_MODEL), F32),
        compiler_params=_cparams(("arbitrary",)),
        name="outproj",
    )(*ys, w_out4, x2d, final_w)


def _rope_table(pos, rows):
    half = ROPE_DIM // 2
    inv = ROPE_THETA ** (-jnp.arange(half, dtype=F32) / half)
    ang = pos.astype(F32)[:, None] * inv[None, :]
    cos, sin = jnp.cos(ang), jnp.sin(ang)
    n = pos.shape[0]
    pad = jnp.zeros((n, DK_A - ROPE_DIM), F32)
    c64 = jnp.concatenate([cos, cos, pad + 1.0], axis=1)
    s1 = jnp.concatenate([-sin, jnp.zeros_like(sin), pad], axis=1)
    s2 = jnp.concatenate([jnp.zeros_like(sin), sin, pad], axis=1)
    tab = jnp.stack([jnp.tile(t, (1, LANES // DK_A)) for t in (c64, s1, s2)])
    return jnp.tile(tab, (1, rows // n, 1))


def _block_diag(w):
    nb, bw, _ = w.shape
    eye = jnp.eye(nb, dtype=w.dtype)
    return (eye[:, None, :, None] * w[:, :, None, :]).reshape(nb * bw, nb * bw)


def _prep_layer(l, norm_w, w_in, w_out, lam_q1, lam_k1, lam_q2, lam_k2, subln_w, ssd_conv_w,
                ssd_conv_b, ssd_dt_bias, ssd_a_log, ssd_d, ssd_norm_w, gla_gk_w2, gla_gk_b,
                gla_norm_w, lru_conv_w, lru_conv_b, lru_wa, lru_ba, lru_wx, lru_bx, lru_lambda):
    w = w_in[l]
    cols = lambda a, n: w[:, a:a + n]
    w_main = jnp.stack([
        cols(_QA, 512), cols(_KA, 512), cols(_VA, 512), cols(_XBC, 512), cols(_XBC + 512, 512),
        cols(_ZA, 512), cols(_ZB, 512), jnp.concatenate([cols(_QC, 256), cols(_KC, 256)], axis=1),
        cols(_VC, 512), cols(_GC, 512), cols(_XD, 512), cols(_ZD, 512)]).astype(BF16)
    w_small = jnp.concatenate([cols(_DT, H_B), cols(_GKLR, GK_RANK),
                               jnp.zeros((D_MODEL, LANES - H_B - GK_RANK), F32)], axis=1).astype(BF16)
    lam_init = 0.8 - 0.6 * math.exp(-0.3 * l)
    lam = (jnp.exp(jnp.sum(lam_q1[l] * lam_k1[l])) - jnp.exp(jnp.sum(lam_q2[l] * lam_k2[l]))
           + lam_init).reshape(1, 1).astype(F32)
    pad_lanes = lambda v: jnp.concatenate([v, jnp.zeros((LANES - v.shape[0],), F32)]).reshape(1, LANES)
    w2p = jnp.zeros((LANES, H_C * DK_C), F32).at[H_B:H_B + GK_RANK].set(gla_gk_w2[l]).astype(BF16)
    return dict(
        norm_w=norm_w[l].reshape(1, D_MODEL), w_main=w_main, w_small=w_small,
        w_out4=w_out[l].reshape(4, W_GROUP, D_MODEL).astype(BF16),
        lam=lam, lam_init=lam_init, subln_w=subln_w[l].reshape(1, DV_A),
        ssd_cw=ssd_conv_w[l], ssd_cb=ssd_conv_b[l].reshape(1, CONV_B_CH),
        ssd_dtb=pad_lanes(ssd_dt_bias[l]), ssd_alog=pad_lanes(ssd_a_log[l]),
        ssd_dvec=jnp.repeat(ssd_d[l], P_B).reshape(1, W_GROUP), ssd_nw=ssd_norm_w[l].reshape(1, W_GROUP),
        gla_w2p=w2p, gla_gb=gla_gk_b[l].reshape(1, H_C * DK_C), gla_nw=gla_norm_w[l].reshape(1, DV_C),
        lru_cw=lru_conv_w[l], lru_cb=lru_conv_b[l].reshape(1, W_D),
        lru_wa=_block_diag(lru_wa[l]).astype(BF16), lru_ba=lru_ba[l].reshape(1, W_D),
        lru_wx=_block_diag(lru_wx[l]).astype(BF16), lru_bx=lru_bx[l].reshape(1, W_D),
        lru_lam=lru_lambda[l].reshape(1, W_D),
    )


def _gla_state_to_pairs(s):
    B = s.shape[0]
    return s.reshape(B, H_C // 2, 2, DK_C, DV_C).transpose(0, 1, 4, 2, 3).reshape(
        B, H_C // 2, DV_C, 2 * DK_C)


def _gla_state_from_pairs(s):
    B = s.shape[0]
    return s.reshape(B, H_C // 2, DV_C, 2, DK_C).transpose(0, 1, 3, 4, 2).reshape(B, H_C, DK_C, DV_C)


def _ssd_args(rest, small, states, p):
    return (rest, small, states[1], states[0], p["ssd_cw"], p["ssd_cb"], p["ssd_dtb"], p["ssd_alog"],
            p["ssd_dvec"], p["ssd_nw"])


def _mixers(rest, small, states, p, B, L, ydt, with_ssd=True):
    _, _, gla_s0, lru_h0, lru_buf = states
    out = {}
    if with_ssd:
        out["yb"], out["ssd_s"], out["ssd_c"] = _ssd(*_ssd_args(rest, small, states, p), B, L, ydt)
    yc, gla_st = _gla(rest, small, _gla_state_to_pairs(gla_s0), p["gla_w2p"], p["gla_gb"],
                      p["gla_nw"], B, L, ydt)
    yd, lru_h, lru_c = _lru(rest, lru_buf, lru_h0.reshape(B, 1, W_D), p["lru_cw"], p["lru_cb"],
                            p["lru_wa"], p["lru_ba"], p["lru_wx"], p["lru_bx"], p["lru_lam"], B, L, ydt)
    out.update(yc=yc, yd=yd, gla_s=_gla_state_from_pairs(gla_st), lru_h=lru_h.reshape(B, W_D),
               lru_c=lru_c)
    return out


def _finish_layer(x2d, ya, m, p, T, final_w, final):
    ys = (ya, m["yb"].reshape(T, W_GROUP), m["yc"].reshape(T, W_GROUP), m["yd"].reshape(T, W_D))
    x_new = _outproj(ys, p["w_out4"], x2d, final_w, final)
    return x_new, (m["ssd_s"], m["ssd_c"], m["gla_s"], m["lru_h"], m["lru_c"])


def _layer_pair(hp, hs, Bp, Lp, Bs, Ls, tab_p, tab_s, p, st_p, st_s, past, layer, depth, kv_p, kv_s,
                final_w, final):
    cache_k, cache_v, page_table = past
    ydt_p = BF16 if Lp % 16 == 0 else F32
    ydt_s = BF16 if Ls % 16 == 0 else F32
    q_p, k_p, v_p, rest_p, small_p, kf_p, vf_p = _inproj(
        hp, p["norm_w"], p["w_main"], p["w_small"], tab_p, Lp, layer, depth, kv_p)
    q_s, k_s, v_s, rest_s, small_s, kf_s, vf_s = _inproj(
        hs, p["norm_w"], p["w_main"], p["w_small"], tab_s, Ls, layer, depth, kv_s)
    yb_p, ssd_s_p, ssd_c_p, ya_s = _ssd_decode(
        _ssd_args(rest_p, small_p, st_p, p), Bp, Lp, ydt_p, p["lam"], q_s, k_s, v_s, rest_s,
        p["subln_w"], cache_k, cache_v, page_table, layer, Bs, Ls, p["lam_init"])
    ya_p = _attn_prompt(p["lam"], q_p, k_p, v_p, rest_p, p["subln_w"], Bp, Lp, p["lam_init"])
    m_p = _mixers(rest_p, small_p, st_p, p, Bp, Lp, ydt_p, with_ssd=False)
    m_p.update(yb=yb_p, ssd_s=ssd_s_p, ssd_c=ssd_c_p)
    m_s = _mixers(rest_s, small_s, st_s, p, Bs, Ls, ydt_s)
    hp, new_p = _finish_layer(hp, ya_p, m_p, p, Bp * Lp, final_w, final)
    hs, new_s = _finish_layer(hs, ya_s, m_s, p, Bs * Ls, final_w, final)
    return hp, hs, new_p, new_s, (kf_p, vf_p), (kf_s, vf_s)


def kernel(x_prompt, x_sample, cache_k, cache_v, page_table, state_ssd, state_ssd_conv, state_gla, state_lru, state_lru_conv, norm_w, w_in, w_out, lam_q1, lam_k1, lam_q2, lam_k2, subln_w, ssd_conv_w, ssd_conv_b, ssd_dt_bias, ssd_a_log, ssd_d, ssd_norm_w, gla_gk_w2, gla_gk_b, gla_norm_w, lru_conv_w, lru_conv_b, lru_wa, lru_ba, lru_wx, lru_bx, lru_lambda, final_norm_w):
    Bp, Lp, _ = x_prompt.shape
    Bs, Ls, _ = x_sample.shape
    depth = w_in.shape[0]
    past_len = page_table.shape[1] * cache_k.shape[2]
    tab_p = _rope_table(jnp.arange(Lp, dtype=jnp.int32), max(Lp, min(INPROJ_ROWS, Bp * Lp)))
    tab_s = _rope_table(past_len + jnp.arange(Ls, dtype=jnp.int32), max(Ls, min(INPROJ_ROWS, Bs * Ls)))
    final_w = final_norm_w.reshape(1, D_MODEL)
    hp = x_prompt.reshape(Bp * Lp, D_MODEL)
    hs = x_sample.reshape(Bs * Ls, D_MODEL)
    sp, ss = [], []
    kv_p = tuple(jnp.zeros((depth, Bp * Lp * H_A, LANES), F32) for _ in range(2))
    kv_s = tuple(jnp.zeros((depth, Bs * Ls * H_A, LANES), F32) for _ in range(2))
    for l in range(depth):
        p = _prep_layer(l, norm_w, w_in, w_out, lam_q1, lam_k1, lam_q2, lam_k2, subln_w, ssd_conv_w,
                        ssd_conv_b, ssd_dt_bias, ssd_a_log, ssd_d, ssd_norm_w, gla_gk_w2, gla_gk_b,
                        gla_norm_w, lru_conv_w, lru_conv_b, lru_wa, lru_ba, lru_wx, lru_bx, lru_lambda)
        final = l == depth - 1
        st_p = (jnp.zeros((Bp, H_B, P_B, N_B), F32), jnp.zeros((Bp, CONV_W - 1, CONV_B_CH), F32),
                jnp.zeros((Bp, H_C, DK_C, DV_C), F32), jnp.zeros((Bp, W_D), F32),
                jnp.zeros((Bp, CONV_W - 1, W_D), F32))
        st_s = (state_ssd[l], state_ssd_conv[l], state_gla[l], state_lru[l], state_lru_conv[l])
        hp, hs, new_p, new_s, kv_p, kv_s = _layer_pair(
            hp, hs, Bp, Lp, Bs, Ls, tab_p, tab_s, p, st_p, st_s, (cache_k, cache_v, page_table), l,
            depth, kv_p, kv_s, final_w, final)
        sp.append(new_p)
        ss.append(new_s)
    outs_p = [x.reshape(depth, Bp, Lp, H_A, LANES) for x in kv_p]
    outs_p += [jnp.stack([s[i] for s in sp]) for i in range(5)]
    outs_s = [x.reshape(depth, Bs, Ls, H_A, LANES) for x in kv_s]
    outs_s += [jnp.stack([s[i] for s in ss]) for i in range(5)]
    return (hp.reshape(Bp, Lp, D_MODEL), hs.reshape(Bs, Ls, D_MODEL), *outs_p, *outs_s)
```

```python
import functools
import math

import jax
import jax.numpy as jnp
from jax import lax
from jax.experimental import pallas as pl
from jax.experimental.pallas import tpu as pltpu

F32 = jnp.float32
BF16 = jnp.bfloat16
EPS = 1e-6
NEG = -1e30
LOG2E = math.log2(math.e)

D_MODEL = 2048
W_GROUP = 512
H_A, DV_A, DK_A = 4, 128, 64
ROPE_DIM, ROPE_THETA = 16, 500000.0
H_B, P_B, N_B, G_B = 8, 64, 128, 2
CONV_W = 4
CONV_B_CH = W_GROUP + 2 * G_B * N_B
H_C, DK_C, DV_C = 4, 64, 128
GK_RANK, GK_NORM = 16, 16.0
W_D, NB_D, BW_D = 512, 8, 64
LRU_C = 8.0

LANES = 128
SUBLANES = 8
COL_BLK = 512
INPROJ_ROWS = 1024
ATTN_BQ = 512
ATTN_BK = 512
ATTN_HEADS = 2
MIXER_SEQS = 4
GLA_SEQS = 8
VMEM_LIMIT = 56 * 1024 * 1024

_SPLITS = (512, 512, 512, 512, CONV_B_CH, H_B, W_GROUP, 256, 256, 512, GK_RANK, W_GROUP, W_D, W_D)
_OFF = [0]
for _s in _SPLITS:
    _OFF.append(_OFF[-1] + _s)
(_QA, _KA, _VA, _ZA, _XBC, _DT, _ZB, _QC, _KC, _VC, _GKLR, _GC, _XD, _ZD) = _OFF[:-1]

REST_COLS = 4608


def _cparams(sem):
    return pltpu.CompilerParams(dimension_semantics=sem, vmem_limit_bytes=VMEM_LIMIT)


def _sigmoid(x):
    return jax.nn.sigmoid(x)


def _silu(x):
    return x * _sigmoid(x)


def _softplus(x):
    return jnp.maximum(x, 0.0) + jnp.log1p(jnp.exp(-jnp.abs(x)))


def _rmsnorm_rows(x, w):
    return x * lax.rsqrt(jnp.mean(x * x, axis=-1, keepdims=True) + EPS) * w


def _dot(a, b):
    return jnp.dot(a, b, preferred_element_type=F32)


def _dot_nt(a, b):
    return lax.dot_general(a, b, (((1,), (1,)), ((), ())), preferred_element_type=F32)


def _dot_tn(a, b):
    return lax.dot_general(a, b, (((0,), (0,)), ((), ())), preferred_element_type=F32)


def _dot_exact(a, b):
    return jnp.dot(a, b, preferred_element_type=F32, precision=lax.Precision.HIGHEST)


def _lower_tri(n):
    r = lax.broadcasted_iota(jnp.int32, (n, n), 0)
    c = lax.broadcasted_iota(jnp.int32, (n, n), 1)
    return r >= c


def _batch_group(b, want):
    return math.gcd(b, want)


def _inproj_kernel(x_ref, nw_ref, w_ref, ws_ref, rope_ref, kf_in_ref, vf_in_ref,
                   q_ref, k_ref, v_ref, rest_ref, small_ref, kf_ref, vf_ref, u_scr):
    del kf_in_ref, vf_in_ref
    j = pl.program_id(1)
    tm = x_ref.shape[0]

    @pl.when(j == 0)
    def _():
        u = _rmsnorm_rows(x_ref[...], nw_ref[...]).astype(BF16)
        u_scr[...] = u
        small_ref[...] = _dot(u, ws_ref[...])

    acc = _dot(u_scr[...], w_ref[...])

    def rope_store(dst, scale, dst_heads=None):
        c, s1, s2 = rope_ref[0], rope_ref[1], rope_ref[2]
        for hh in range(COL_BLK // LANES):
            sl = slice(hh * LANES, (hh + 1) * LANES)
            a = acc[:, sl]
            r = (a * c + pltpu.roll(a, LANES - ROPE_DIM // 2, 1) * s1
                 + pltpu.roll(a, ROPE_DIM // 2, 1) * s2)
            dst[:, sl] = r * scale
            if dst_heads is not None:
                dst_heads[pl.ds(hh, tm, stride=H_A), :] = r

    @pl.when(j == 0)
    def _():
        rope_store(q_ref, DK_A ** -0.5 * LOG2E)

    @pl.when(j == 1)
    def _():
        rope_store(k_ref, 1.0, kf_ref)

    @pl.when(j == 2)
    def _():
        v_ref[...] = acc
        for hh in range(H_A):
            vf_ref[pl.ds(hh, tm, stride=H_A), :] = acc[:, hh * LANES:(hh + 1) * LANES]

    @pl.when(j >= 3)
    def _():
        rest_ref[...] = acc


def _inproj(x2d, norm_w, w_main, w_small, rope_tab, seq_len, layer, depth, kv_prev):
    T = x2d.shape[0]
    tm = min(INPROJ_ROWS, T)
    assert T % tm == 0 and (seq_len % tm == 0 or tm % seq_len == 0)
    n_pos_blk = max(seq_len // tm, 1)
    nj = w_main.shape[0]
    kv_shape = jax.ShapeDtypeStruct((depth, T * H_A, LANES), F32)
    out_shape = (
        jax.ShapeDtypeStruct((T, COL_BLK), F32),
        jax.ShapeDtypeStruct((T, COL_BLK), F32),
        jax.ShapeDtypeStruct((T, COL_BLK), F32),
        jax.ShapeDtypeStruct((T, REST_COLS), F32),
        jax.ShapeDtypeStruct((T, LANES), F32),
        kv_shape, kv_shape,
    )
    row_blk = pl.BlockSpec((tm, COL_BLK), lambda i, j: (i, 0))
    kv_blk = pl.BlockSpec((None, tm * H_A, LANES), lambda i, j: (layer, i, 0))
    in_specs = [
        pl.BlockSpec((tm, D_MODEL), lambda i, j: (i, 0)),
        pl.BlockSpec((1, D_MODEL), lambda i, j: (0, 0)),
        pl.BlockSpec((None, D_MODEL, COL_BLK), lambda i, j: (j, 0, 0)),
        pl.BlockSpec((D_MODEL, LANES), lambda i, j: (0, 0)),
        pl.BlockSpec((3, tm, LANES), lambda i, j: (0, i % n_pos_blk, 0)),
    ]
    in_specs += [pl.BlockSpec(memory_space=pl.ANY)] * 2
    args = (x2d, norm_w, w_main, w_small, rope_tab, *kv_prev)
    aliases = {len(args) - 2: 5, len(args) - 1: 6}
    return pl.pallas_call(
        _inproj_kernel,
        grid=(T // tm, nj),
        in_specs=in_specs,
        out_specs=(
            row_blk, row_blk, row_blk,
            pl.BlockSpec((tm, COL_BLK), lambda i, j: (i, jnp.maximum(j - 3, 0))),
            pl.BlockSpec((tm, LANES), lambda i, j: (i, 0)),
            kv_blk, kv_blk,
        ),
        out_shape=out_shape,
        input_output_aliases=aliases,
        scratch_shapes=[pltpu.VMEM((tm, D_MODEL), BF16)],
        compiler_params=_cparams(("arbitrary", "arbitrary")),
        name="inproj",
    )(*args)


def _attn_finish(o, lam_init, w, z):
    o = _rmsnorm_rows(o, w) * (1.0 - lam_init)
    return o * _silu(z)


def _attn_kernel(lam_ref, q_ref, k_ref, v_ref, z_ref, w_ref, o_ref, kb_scr, vt_scr,
                 m_scr, l_scr, acc_scr, *, bq, bk, lam_init):
    qi = pl.program_id(2)
    n_kb = kb_scr.shape[1]
    diag_blocks = bq // bk
    heads = range(ATTN_HEADS)
    hl = [slice(g * LANES, (g + 1) * LANES) for g in heads]

    @pl.when(qi == 0)
    def _():
        for g in heads:
            for c in range(n_kb):
                rows = slice(c * bk, (c + 1) * bk)
                kb_scr[g, c] = k_ref[rows, hl[g]].astype(BF16)
                vt_scr[g, c] = v_ref[rows, hl[g]].T.astype(BF16)

    qts = []
    for g in heads:
        qt = q_ref[:, hl[g]].T
        feat = lax.broadcasted_iota(jnp.int32, qt.shape, 0)
        qts.append((jnp.where(feat < DK_A, qt, 0.0).astype(BF16),
                    jnp.where(feat >= DK_A, qt, 0.0).astype(BF16)))
    m_scr[...] = jnp.full(m_scr.shape, NEG, F32)
    l_scr[...] = jnp.zeros(l_scr.shape, F32)
    acc_scr[...] = jnp.zeros(acc_scr.shape, F32)

    halves = 2
    hw = bq // halves

    def step(j, mask):
        chains = [(g, m, hf) for g in heads for hf in range(halves) for m in range(2)]
        n_keys = [(hf + 1) * hw if (mask is not None and diag_blocks == 1) else bk
                  for _, _, hf in chains]
        ss = [_dot(kb_scr[g, j, :nk, :], qts[g][m][:, hf * hw:(hf + 1) * hw])
              for (g, m, hf), nk in zip(chains, n_keys)]
        for (g, m, hf), nk, s in zip(chains, n_keys, ss):
            cols = slice(hf * hw, (hf + 1) * hw)
            if mask is not None:
                s = jnp.where(mask[:nk, cols], s, NEG)
            mx = m_scr[g, m, :, cols]
            mn = jnp.maximum(mx, jnp.max(s, axis=0, keepdims=True))
            alpha = jnp.exp2(mx - mn)
            p = jnp.exp2(s - mn)
            l_scr[g, m, :, cols] = alpha * l_scr[g, m, :, cols] + jnp.sum(p, axis=0, keepdims=True)
            acc_scr[g, m, :, cols] = (alpha * acc_scr[g, m, :, cols]
                                      + _dot(vt_scr[g, j, :, :nk], p.astype(BF16)))
            m_scr[g, m, :, cols] = mn

    def unmasked(j, carry):
        step(j, None)
        return carry

    lax.fori_loop(0, qi * diag_blocks, unmasked, 0)
    key_i = lax.broadcasted_iota(jnp.int32, (bk, bq), 0)
    qry_i = lax.broadcasted_iota(jnp.int32, (bk, bq), 1)
    for d in range(diag_blocks):
        step(qi * diag_blocks + d, key_i + d * bk <= qry_i)
    lam = lam_ref[0, 0]
    for g in heads:
        o = (acc_scr[g, 0] / l_scr[g, 0] - lam * (acc_scr[g, 1] / l_scr[g, 1])).T
        o_ref[:, hl[g]] = _attn_finish(o, lam_init, w_ref[...], z_ref[:, hl[g]]).astype(o_ref.dtype)


def _attn_prompt(lam, q, k, v, rest, subln_w, B, L, lam_init):
    bq = min(ATTN_BQ, L)
    bk = min(ATTN_BK, bq)
    nq = L // bq
    hw = ATTN_HEADS * LANES
    z_off = (2 * W_GROUP) // hw
    kern = functools.partial(_attn_kernel, bq=bq, bk=bk, lam_init=lam_init)
    return pl.pallas_call(
        kern,
        grid=(B, H_A // ATTN_HEADS, nq),
        in_specs=[
            pl.BlockSpec(memory_space=pltpu.SMEM),
            pl.BlockSpec((bq, hw), lambda b, h, i: (b * nq + i, h)),
            pl.BlockSpec((L, hw), lambda b, h, i: (b, h)),
            pl.BlockSpec((L, hw), lambda b, h, i: (b, h)),
            pl.BlockSpec((bq, hw), lambda b, h, i: (b * nq + i, z_off + h)),
            pl.BlockSpec((1, LANES), lambda b, h, i: (0, 0)),
        ],
        out_specs=pl.BlockSpec((bq, hw), lambda b, h, i: (b * nq + i, h)),
        out_shape=jax.ShapeDtypeStruct((B * L, W_GROUP), BF16),
        scratch_shapes=[
            pltpu.VMEM((ATTN_HEADS, L // bk, bk, LANES), BF16),
            pltpu.VMEM((ATTN_HEADS, L // bk, DV_A, bk), BF16),
            pltpu.VMEM((ATTN_HEADS, 2, 1, bq), F32),
            pltpu.VMEM((ATTN_HEADS, 2, 1, bq), F32),
            pltpu.VMEM((ATTN_HEADS, 2, DV_A, bq), F32),
        ],
        compiler_params=_cparams(("arbitrary", "arbitrary", "arbitrary")),
        name="attn_prompt",
    )(lam, q, k, v, rest, subln_w)


def _decode_init(q_ref, qrow_scr, m_scr, l_scr, acc_scr, ls):
    rows = 2 * ls
    r = lax.broadcasted_iota(jnp.int32, (rows, LANES), 0)
    c = lax.broadcasted_iota(jnp.int32, (rows, LANES), 1)
    own_map = (r // ls) == (c // DK_A)
    for h in range(H_A):
        qh = q_ref[:, h * LANES:(h + 1) * LANES]
        qrow_scr[h] = jnp.where(own_map, jnp.concatenate([qh, qh], axis=0), 0.0).astype(BF16)
    m_scr[...] = jnp.full(m_scr.shape, NEG, F32)
    l_scr[...] = jnp.zeros(l_scr.shape, F32)
    acc_scr[...] = jnp.zeros(acc_scr.shape, F32)


def _decode_head_rows(pages, h, page):
    head_rows = pl.ds(h, page, stride=H_A)
    return jnp.concatenate([pg[head_rows, :] for pg in pages], axis=0).astype(BF16)


def _decode_accumulate(h, s, vv, m_scr, l_scr, acc_scr):
    mx = m_scr[h]
    mn = jnp.maximum(mx, jnp.max(s, axis=-1, keepdims=True))
    alpha = jnp.exp2(mx - mn)
    pr = jnp.exp2(s - mn)
    l_scr[h] = alpha * l_scr[h] + jnp.sum(pr, axis=-1, keepdims=True)
    acc_scr[h] = alpha * acc_scr[h] + _dot(pr.astype(BF16), vv)
    m_scr[h] = mn


def _decode_finish(lam_ref, kn_ref, vn_ref, z_ref, w_ref, o_ref, qrow_scr, m_scr, l_scr, acc_scr,
                   ls, lam_init):
    rows = 2 * ls
    lam = lam_ref[0, 0]
    w = w_ref[...]
    jj = lax.broadcasted_iota(jnp.int32, (rows, ls), 1)
    qq = lax.broadcasted_iota(jnp.int32, (rows, ls), 0) % ls
    for h in range(H_A):
        cs = slice(h * LANES, (h + 1) * LANES)
        s = _dot_nt(qrow_scr[h], kn_ref[:, cs].astype(BF16))
        _decode_accumulate(h, jnp.where(jj <= qq, s, NEG), vn_ref[:, cs].astype(BF16),
                           m_scr, l_scr, acc_scr)
        o = acc_scr[h] / l_scr[h]
        o = o[:ls] - lam * o[ls:]
        o_ref[:, cs] = _attn_finish(o, lam_init, w, z_ref[:, cs]).astype(o_ref.dtype)


CONV_PAD = SUBLANES
CONV_TAIL = CONV_PAD - (CONV_W - 1)


def _conv_step(x_ref, cw_ref, cb_ref, xp_scr, q):
    groups = q // SUBLANES
    x = x_ref[...]
    ch = x.shape[-1]
    x3 = x.reshape(groups, SUBLANES, ch)
    prev = xp_scr[...].reshape(1, SUBLANES, ch)
    r8 = lax.broadcasted_iota(jnp.int32, x3.shape, 1)

    def back(s):
        cur = pltpu.roll(x3, s, 1)
        before = pltpu.roll(prev, s, 1)
        if groups > 1:
            before = jnp.concatenate([before, cur[:-1]], axis=0)
        return jnp.where(r8 >= s, cur, before)

    y = back(CONV_W - 1) * cw_ref[0:1, :]
    for t in range(1, CONV_W - 1):
        y = y + back(CONV_W - 1 - t) * cw_ref[t:t + 1, :]
    y = y + x3 * cw_ref[CONV_W - 1:CONV_W, :]
    y = y + cb_ref[...]
    xp_scr[...] = x[q - SUBLANES:, :]
    return y.reshape(q, ch)


def _ssd_inputs(xbc_ref, dt_ref, cw_ref, cb_ref, dtb_ref, xp_scr, q, nb):
    xbc = [_silu(_conv_step(xbc_ref.at[b], cw_ref, cb_ref, xp_scr.at[b], q)) for b in range(nb)]
    dt = [_softplus(dt_ref[b] + dtb_ref[...]) for b in range(nb)]
    return xbc, dt


def _ssd_chunk_pairs(xbc_all, dt_all, zb_ref, alog_ref, dvec_ref, nw_ref, y_ref, s_scr, y_scr, q, nb):
    half = LANES // 2
    row = lax.broadcasted_iota(jnp.int32, (q, LANES), 0)
    lane = lax.broadcasted_iota(jnp.int32, (q, LANES), 1)
    hi = lane >= half
    tri2 = row >= lane % half
    a = -jnp.exp(alog_ref[...])
    tri_f = _lower_tri(q).astype(F32)
    heads_per_group = H_B // G_B
    zeros_state = jnp.zeros((P_B, N_B), BF16)
    for b in range(nb):
        xbc, dt = xbc_all[b], dt_all[b]
        xs = xbc[:, :W_GROUP]
        cum = _dot_exact(tri_f, dt * a)
        cum_t = cum.T
        dt_t = dt.T
        last = cum[q - 1:q, :]
        ecum = jnp.exp(cum)
        wj = jnp.exp(last - cum) * dt
        elast = jnp.exp(last)
        for g in range(G_B):
            bgb = xbc[:, W_GROUP + g * N_B:W_GROUP + (g + 1) * N_B].astype(BF16)
            cg = xbc[:, W_GROUP + (G_B + g) * N_B:W_GROUP + (G_B + g + 1) * N_B]
            cb2 = _dot_nt(cg.astype(BF16), jnp.concatenate([bgb, bgb], axis=0))
            for pair in range(heads_per_group // 2):
                ha = g * heads_per_group + 2 * pair
                hb = ha + 1
                lanes = slice(ha * P_B, (hb + 1) * P_B)
                pick = lambda v: jnp.where(hi, v[:, hb:hb + 1], v[:, ha:ha + 1])
                cj = jnp.concatenate([cum_t[ha:ha + 1, :], cum_t[hb:hb + 1, :]], axis=1)
                dtj = jnp.concatenate([dt_t[ha:ha + 1, :], dt_t[hb:hb + 1, :]], axis=1)
                dec = jnp.exp(jnp.where(tri2, pick(cum) - cj, NEG))
                mh = (cb2 * dec * dtj).astype(BF16)
                xp = xs[:, lanes]
                x_bd = jnp.concatenate([jnp.where(hi, 0.0, xp), jnp.where(hi, xp, 0.0)],
                                       axis=0).astype(BF16)
                sa, sb = s_scr[b, ha], s_scr[b, hb]
                s_bd = jnp.concatenate(
                    [jnp.concatenate([sa.astype(BF16), zeros_state], axis=1),
                     jnp.concatenate([zeros_state, sb.astype(BF16)], axis=1)], axis=0)
                c_dec = jnp.concatenate([cg * ecum[:, ha:ha + 1], cg * ecum[:, hb:hb + 1]],
                                        axis=1).astype(BF16)
                y_scr[b, :, lanes] = _dot(mh, x_bd) + _dot_nt(c_dec, s_bd)
                upd = _dot_tn((xp * pick(wj)).astype(BF16), bgb)
                s_scr[b, ha] = elast[:, ha:ha + 1] * sa + upd[:P_B]
                s_scr[b, hb] = elast[:, hb:hb + 1] * sb + upd[P_B:]
        y = y_scr[b] + dvec_ref[...] * xs
        y_ref[b] = _rmsnorm_rows(y * _silu(zb_ref[b]), nw_ref[...]).astype(y_ref.dtype)


def _ssd_chunk(xbc_all, dt_all, zb_ref, alog_ref, dvec_ref, nw_ref, y_ref, s_scr, y_scr, q, nb):
    if 2 * q == LANES and 2 * P_B == LANES:
        return _ssd_chunk_pairs(xbc_all, dt_all, zb_ref, alog_ref, dvec_ref, nw_ref, y_ref, s_scr,
                                y_scr, q, nb)
    tri = _lower_tri(q)
    tri_f = tri.astype(F32)
    a = -jnp.exp(alog_ref[...])
    heads_per_group = H_B // G_B
    for b in range(nb):
        xbc, dt = xbc_all[b], dt_all[b]
        xs = xbc[:, :W_GROUP]
        cum = _dot_exact(tri_f, dt * a)
        cum_t = cum.T
        dt_t = dt.T
        last = cum[q - 1:q, :]
        ecum = jnp.exp(cum)
        wj = jnp.exp(last - cum) * dt
        elast = jnp.exp(last)
        for g in range(G_B):
            bg = xbc[:, W_GROUP + g * N_B:W_GROUP + (g + 1) * N_B]
            cg = xbc[:, W_GROUP + (G_B + g) * N_B:W_GROUP + (G_B + g + 1) * N_B]
            bgb = bg.astype(BF16)
            cb_g = _dot_nt(cg.astype(BF16), bgb)
            for hh in range(heads_per_group):
                h = g * heads_per_group + hh
                hs = slice(h * P_B, (h + 1) * P_B)
                seg = cum[:, h:h + 1] - cum_t[h:h + 1, :]
                dec = jnp.exp(jnp.where(tri, seg, NEG))
                mh = (cb_g * dec * dt_t[h:h + 1, :]).astype(BF16)
                xh = xs[:, hs]
                s_old = s_scr[b, h]
                yh = _dot(mh, xh.astype(BF16)) + _dot_nt((cg * ecum[:, h:h + 1]).astype(BF16),
                                                         s_old.astype(BF16))
                y_scr[b, :, hs] = yh
                xw = (xh * wj[:, h:h + 1]).astype(BF16)
                s_scr[b, h] = elast[:, h:h + 1] * s_old + _dot_tn(xw, bgb)
        y = y_scr[b] + dvec_ref[...] * xs
        y_ref[b] = _rmsnorm_rows(y * _silu(zb_ref[b]), nw_ref[...]).astype(y_ref.dtype)


def _ssd_kernel(xbc_ref, dt_ref, zb_ref, buf_ref, s0_ref, cw_ref, cb_ref, dtb_ref, alog_ref,
                dvec_ref, nw_ref, y_ref, sout_ref, cout_ref, xp_scr, s_scr, y_scr, *, q, nb):
    c = pl.program_id(1)

    @pl.when(c == 0)
    def _():
        s_scr[...] = s0_ref[...]
        xp_scr[...] = jnp.zeros(xp_scr.shape, F32)
        xp_scr[:, CONV_TAIL:CONV_PAD, :] = buf_ref[...]

    xbc, dt = _ssd_inputs(xbc_ref, dt_ref, cw_ref, cb_ref, dtb_ref, xp_scr, q, nb)
    _ssd_chunk(xbc, dt, zb_ref, alog_ref, dvec_ref, nw_ref, y_ref, s_scr, y_scr, q, nb)

    @pl.when(c == pl.num_programs(1) - 1)
    def _():
        sout_ref[...] = s_scr[...]
        cout_ref[...] = xp_scr[:, CONV_TAIL:CONV_PAD, :]


def _ssd_decode_kernel(pt_ref, lam_ref, xbc_ref, dt_ref, zb_ref, buf_ref, s0_ref, cw_ref, cb_ref,
                       dtb_ref, alog_ref, dvec_ref, nw_ref, q_ref, kn_ref, vn_ref, z_ref, w_ref,
                       ck_ref, cv_ref, y_ref, sout_ref, cout_ref, o_ref,
                       xp_scr, s_scr, y_scr, qrow_scr, m_scr, l_scr, acc_scr, kbuf, vbuf, sem,
                       *, q, nb, n_pp, steps_per_seq, ls, page, layer, lam_init):
    c = pl.program_id(1)
    t = pl.program_id(0) * pl.num_programs(1) + c
    n_steps = pl.num_programs(0) * pl.num_programs(1)
    part = t % steps_per_seq
    slot = t % 2

    def page_copies(step, buf_slot):
        seq = step // steps_per_seq
        first = (step % steps_per_seq) * n_pp
        pages = [pt_ref[seq, first + i] for i in range(n_pp)]
        k_copies = [pltpu.make_async_copy(ck_ref.at[layer, pg], kbuf.at[buf_slot, i],
                                          sem.at[buf_slot, 0]) for i, pg in enumerate(pages)]
        v_copies = [pltpu.make_async_copy(cv_ref.at[layer, pg], vbuf.at[buf_slot, i],
                                          sem.at[buf_slot, 1]) for i, pg in enumerate(pages)]
        return k_copies, v_copies

    @pl.when(t == 0)
    def _():
        k_copies, v_copies = page_copies(0, 0)
        for cp in k_copies + v_copies:
            cp.start()

    @pl.when(c == 0)
    def _():
        s_scr[...] = s0_ref[...]
        xp_scr[...] = jnp.zeros(xp_scr.shape, F32)
        xp_scr[:, CONV_TAIL:CONV_PAD, :] = buf_ref[...]

    @pl.when(part == 0)
    def _():
        _decode_init(q_ref, qrow_scr, m_scr, l_scr, acc_scr, ls)

    k_now, v_now = page_copies(t, slot)
    nxt = jnp.minimum(t + 1, n_steps - 1)
    k_next, v_next = page_copies(nxt, 1 - slot)

    kp = [kbuf.at[slot, i] for i in range(n_pp)]
    vp = [vbuf.at[slot, i] for i in range(n_pp)]
    for cp in k_next + v_next:
        cp.start()
    for cp in k_now + v_now:
        cp.wait()
    scores = [_dot_nt(qrow_scr[h], _decode_head_rows(kp, h, page)) for h in range(H_A)]
    xbc, dt = _ssd_inputs(xbc_ref, dt_ref, cw_ref, cb_ref, dtb_ref, xp_scr, q, nb)
    for h in range(H_A):
        _decode_accumulate(h, scores[h], _decode_head_rows(vp, h, page), m_scr, l_scr, acc_scr)
    _ssd_chunk(xbc, dt, zb_ref, alog_ref, dvec_ref, nw_ref, y_ref, s_scr, y_scr, q, nb)

    @pl.when(c == pl.num_programs(1) - 1)
    def _():
        sout_ref[...] = s_scr[...]
        cout_ref[...] = xp_scr[:, CONV_TAIL:CONV_PAD, :]

    @pl.when(part == steps_per_seq - 1)
    def _():
        _decode_finish(lam_ref, kn_ref, vn_ref, z_ref, w_ref, o_ref, qrow_scr, m_scr, l_scr,
                       acc_scr, ls, lam_init)

    @pl.when(t == n_steps - 1)
    def _():
        for cp in k_next + v_next:
            cp.wait()


def _ssd_specs(B, L, nb, q, index):
    full = lambda shape: pl.BlockSpec(shape, index(lambda b, c: (0,) * len(shape)))
    in_specs = [
        pl.BlockSpec((nb, q, CONV_B_CH), index(lambda b, c: (b, c, 0))),
        pl.BlockSpec((nb, q, LANES), index(lambda b, c: (b, c, 0))),
        pl.BlockSpec((nb, q, W_GROUP), index(lambda b, c: (b, c, 3))),
        pl.BlockSpec((nb, CONV_W - 1, CONV_B_CH), index(lambda b, c: (b, 0, 0))),
        pl.BlockSpec((nb, H_B, P_B, N_B), index(lambda b, c: (b, 0, 0, 0))),
        full((CONV_W, CONV_B_CH)), full((1, CONV_B_CH)), full((1, LANES)), full((1, LANES)),
        full((1, W_GROUP)), full((1, W_GROUP)),
    ]
    out_specs = [
        pl.BlockSpec((nb, q, W_GROUP), index(lambda b, c: (b, c, 0))),
        pl.BlockSpec((nb, H_B, P_B, N_B), index(lambda b, c: (b, 0, 0, 0))),
        pl.BlockSpec((nb, CONV_W - 1, CONV_B_CH), index(lambda b, c: (b, 0, 0))),
    ]
    scratch = [
        pltpu.VMEM((nb, CONV_PAD, CONV_B_CH), F32),
        pltpu.VMEM((nb, H_B, P_B, N_B), F32),
        pltpu.VMEM((nb, q, W_GROUP), F32),
    ]
    return in_specs, out_specs, scratch


def _ssd_out_shape(B, L, out_dtype):
    return [
        jax.ShapeDtypeStruct((B, L, W_GROUP), out_dtype),
        jax.ShapeDtypeStruct((B, H_B, P_B, N_B), F32),
        jax.ShapeDtypeStruct((B, CONV_W - 1, CONV_B_CH), F32),
    ]


def _ssd(rest, small, buf, s0, cw, cb, dtb, alog, dvec, nw, B, L, out_dtype):
    q = math.gcd(L, 64)
    nc = L // q
    nb = _batch_group(B, MIXER_SEQS)
    in_specs, out_specs, scratch = _ssd_specs(B, L, nb, q, lambda f: f)
    rest3 = rest.reshape(B, L, REST_COLS)
    return pl.pallas_call(
        functools.partial(_ssd_kernel, q=q, nb=nb),
        grid=(B // nb, nc),
        in_specs=in_specs,
        out_specs=tuple(out_specs),
        out_shape=tuple(_ssd_out_shape(B, L, out_dtype)),
        scratch_shapes=scratch,
        compiler_params=_cparams(("arbitrary", "arbitrary")),
        name="ssd",
    )(rest3, small.reshape(B, L, LANES), rest3, buf, s0, cw, cb, dtb, alog, dvec, nw)


def _ssd_decode(ssd_args, B, L, out_dtype, lam, q_s, k_s, v_s, rest_s, subln_w, cache_k, cache_v,
                page_table, layer, Bs, Ls, lam_init):
    rest, small, buf, s0, cw, cb, dtb, alog, dvec, nw = ssd_args
    q = math.gcd(L, 64)
    nc = L // q
    nb = _batch_group(B, MIXER_SEQS)
    steps = (B // nb) * nc
    n_pages = page_table.shape[1]
    page = cache_k.shape[2]
    assert (Bs * n_pages) % steps == 0, "decode pages must spread evenly over the SSD grid"
    n_pp = Bs * n_pages // steps
    assert n_pages % n_pp == 0
    steps_per_seq = n_pages // n_pp
    rows = 2 * Ls
    ck = cache_k.reshape(cache_k.shape[0], cache_k.shape[1], page * H_A, 2 * DK_A)
    cv = cache_v.reshape(cache_v.shape[0], cache_v.shape[1], page * H_A, DV_A)

    def step_of(b, c):
        return b * nc + c

    with_pt = lambda f: (lambda b, c, pt: f(b, c))
    ssd_in, ssd_out, ssd_scratch = _ssd_specs(B, L, nb, q, with_pt)
    tok = pl.BlockSpec((Ls, W_GROUP), lambda b, c, pt: (step_of(b, c) // steps_per_seq, 0))
    kern = functools.partial(_ssd_decode_kernel, q=q, nb=nb, n_pp=n_pp, steps_per_seq=steps_per_seq,
                             ls=Ls, page=page, layer=layer, lam_init=lam_init)
    grid_spec = pltpu.PrefetchScalarGridSpec(
        num_scalar_prefetch=1,
        grid=(B // nb, nc),
        in_specs=[pl.BlockSpec(memory_space=pltpu.SMEM)] + ssd_in + [
            tok, tok, tok,
            pl.BlockSpec((Ls, W_GROUP), lambda b, c, pt: (step_of(b, c) // steps_per_seq, 2)),
            pl.BlockSpec((1, LANES), lambda b, c, pt: (0, 0)),
            pl.BlockSpec(memory_space=pl.ANY),
            pl.BlockSpec(memory_space=pl.ANY),
        ],
        out_specs=tuple(ssd_out + [tok]),
        scratch_shapes=ssd_scratch + [
            pltpu.VMEM((H_A, rows, LANES), BF16),
            pltpu.VMEM((H_A, rows, 1), F32),
            pltpu.VMEM((H_A, rows, 1), F32),
            pltpu.VMEM((H_A, rows, DV_A), F32),
            pltpu.VMEM((2, n_pp, page * H_A, LANES), F32),
            pltpu.VMEM((2, n_pp, page * H_A, LANES), F32),
            pltpu.SemaphoreType.DMA((2, 2)),
        ],
    )
    rest3 = rest.reshape(B, L, REST_COLS)
    return pl.pallas_call(
        kern,
        grid_spec=grid_spec,
        out_shape=tuple(_ssd_out_shape(B, L, out_dtype)
                        + [jax.ShapeDtypeStruct((Bs * Ls, W_GROUP), F32)]),
        compiler_params=_cparams(("arbitrary", "arbitrary")),
        name="ssd_decode",
    )(page_table, lam, rest3, small.reshape(B, L, LANES), rest3, buf, s0, cw, cb, dtb, alog, dvec, nw,
      q_s, k_s, v_s, rest_s, subln_w, ck, cv)


def _gla_kernel(qk_ref, v_ref, gc_ref, sm_ref, s0_ref, w2_ref, gb_ref, nw_ref,
                y_ref, sout_ref, st_scr, *, q, nb):
    c = pl.program_id(1)

    @pl.when(c == 0)
    def _():
        st_scr[...] = s0_ref[...]

    seqs = range(nb)
    pairs = range(H_C // 2)
    width = H_C * DK_C
    nw = nw_ref[...]
    hi = lax.broadcasted_iota(jnp.int32, (q, LANES), 1) >= DK_C
    row2 = lax.broadcasted_iota(jnp.int32, (2 * q, q), 0)
    col2 = lax.broadcasted_iota(jnp.int32, (2 * q, q), 1)
    causal2 = jnp.where(row2 >= q, row2 - q, row2) >= col2
    sm = sm_ref[...].reshape(nb * q, LANES).astype(BF16)
    gkl = _dot(sm, w2_ref[...]) + gb_ref[...]
    g = (jnp.minimum(gkl, 0.0) - jnp.log1p(jnp.exp(-jnp.abs(gkl)))) * (1.0 / GK_NORM)
    bc_all = _dot_exact(_lower_tri(q).astype(F32),
                        jnp.concatenate([g[b * q:(b + 1) * q] for b in seqs], axis=1))
    bc = [bc_all[:, b * width:(b + 1) * width] for b in seqs]
    last = [x[q - 1:q, :] for x in bc]
    kc = [qk_ref[b, :, width:] for b in seqs]
    qe = [qk_ref[b, :, :width] * (DK_C ** -0.5) * jnp.exp(bc[b]) for b in seqs]
    ke = [(kc[b] * jnp.exp(-bc[b])).astype(BF16) for b in seqs]
    kl = [kc[b] * jnp.exp(last[b] - bc[b]) for b in seqs]
    el = [jnp.exp(x) for x in last]
    lanes = [slice(p * LANES, (p + 1) * LANES) for p in pairs]
    vs = [slice(h * DV_C, (h + 1) * DV_C) for h in range(H_C)]
    q2 = [[jnp.concatenate([jnp.where(hi, 0.0, qe[b][:, lanes[p]]),
                            jnp.where(hi, qe[b][:, lanes[p]], 0.0)], axis=0).astype(BF16)
           for p in pairs] for b in seqs]
    kl_a = [[jnp.where(hi, 0.0, kl[b][:, lanes[p]]).astype(BF16) for p in pairs] for b in seqs]
    kl_b = [[jnp.where(hi, kl[b][:, lanes[p]], 0.0).astype(BF16) for p in pairs] for b in seqs]
    vh = [[v_ref[b, :, vs[h]].astype(BF16) for h in range(H_C)] for b in seqs]
    st_old = [[st_scr[b, p] for p in pairs] for b in seqs]
    att = [[_dot_nt(q2[b][p], ke[b][:, lanes[p]]) for p in pairs] for b in seqs]
    o_state = [[_dot_nt(q2[b][p], st_old[b][p].astype(BF16)) for p in pairs] for b in seqs]
    upd = [[_dot_tn(vh[b][2 * p], kl_a[b][p]) + _dot_tn(vh[b][2 * p + 1], kl_b[b][p])
            for p in pairs] for b in seqs]
    for b in seqs:
        for p in pairs:
            attm = jnp.where(causal2, att[b][p], 0.0).astype(BF16)
            st_scr[b, p] = el[b][:, lanes[p]] * st_old[b][p] + upd[b][p]
            for j in range(2):
                h = 2 * p + j
                rows = slice(j * q, (j + 1) * q)
                o = _dot(attm[rows], vh[b][h]) + o_state[b][p][rows]
                y_ref[b, :, vs[h]] = (_rmsnorm_rows(o, nw)
                                      * _silu(gc_ref[b, :, vs[h]])).astype(y_ref.dtype)

    @pl.when(c == pl.num_programs(1) - 1)
    def _():
        sout_ref[...] = st_scr[...]


def _gla(rest, small, s0_t, w2p, gb, nw, B, L, out_dtype):
    q = min(L, 64)
    nc = L // q
    nb = _batch_group(B, GLA_SEQS)
    kern = functools.partial(_gla_kernel, q=q, nb=nb)
    full = lambda shape: pl.BlockSpec(shape, lambda b, c: (0,) * len(shape))
    state = pl.BlockSpec((nb, H_C // 2, DV_C, 2 * DK_C), lambda b, c: (b, 0, 0, 0))
    rest3 = rest.reshape(B, L, REST_COLS)
    return pl.pallas_call(
        kern,
        grid=(B // nb, nc),
        in_specs=[
            pl.BlockSpec((nb, q, W_GROUP), lambda b, c: (b, c, 4)),
            pl.BlockSpec((nb, q, W_GROUP), lambda b, c: (b, c, 5)),
            pl.BlockSpec((nb, q, W_GROUP), lambda b, c: (b, c, 6)),
            pl.BlockSpec((nb, q, LANES), lambda b, c: (b, c, 0)),
            state,
            full((LANES, H_C * DK_C)), full((1, H_C * DK_C)), full((1, DV_C)),
        ],
        out_specs=(pl.BlockSpec((nb, q, W_GROUP), lambda b, c: (b, c, 0)), state),
        out_shape=(
            jax.ShapeDtypeStruct((B, L, W_GROUP), out_dtype),
            jax.ShapeDtypeStruct((B, H_C // 2, DV_C, 2 * DK_C), F32),
        ),
        scratch_shapes=[pltpu.VMEM((nb, H_C // 2, DV_C, 2 * DK_C), F32)],
        compiler_params=_cparams(("arbitrary", "arbitrary")),
        name="gla",
    )(rest3, rest3, rest3, small.reshape(B, L, LANES), s0_t, w2p, gb, nw)


def _lru_kernel(xd_ref, zd_ref, buf_ref, h0_ref, cw_ref, cb_ref, wa_ref, ba_ref, wx_ref, bx_ref,
                lam_ref, y_ref, hout_ref, cout_ref, xp_scr, h_scr, a_scr, u_scr, *, q, nb):
    c = pl.program_id(1)

    @pl.when(c == 0)
    def _():
        h_scr[...] = h0_ref[...]
        xp_scr[...] = jnp.zeros(xp_scr.shape, F32)
        xp_scr[:, CONV_TAIL:CONV_PAD, :] = buf_ref[...]

    sp_lam = _softplus(-lam_ref[...])
    for b in range(nb):
        xr = _conv_step(xd_ref.at[b], cw_ref, cb_ref, xp_scr.at[b], q)
        xb = xr.astype(BF16)
        r = _sigmoid(_dot(xb, wa_ref[...]) + ba_ref[...])
        i = _sigmoid(_dot(xb, wx_ref[...]) + bx_ref[...])
        log_a = -LRU_C * r * sp_lam
        a = jnp.exp(log_a)
        th = jnp.tanh(log_a)
        u = jnp.sqrt(-2.0 * th / (1.0 - th)) * (i * xr)
        a = a.reshape(q // SUBLANES, SUBLANES, W_D)
        u = u.reshape(q // SUBLANES, SUBLANES, W_D)
        r8 = lax.broadcasted_iota(jnp.int32, a.shape, 1)
        for s in (1, 2, 4):
            keep = r8 >= s
            a_sh = pltpu.roll(a, s, 1)
            u_sh = pltpu.roll(u, s, 1)
            u = jnp.where(keep, a * u_sh + u, u)
            a = jnp.where(keep, a * a_sh, a)
        a_scr[b] = a.reshape(q, W_D)
        u_scr[b] = u.reshape(q, W_D)
    carry = [h_scr[b] for b in range(nb)]
    for gi in range(q // SUBLANES):
        rows = slice(gi * SUBLANES, (gi + 1) * SUBLANES)
        for b in range(nb):
            hg = a_scr[b, rows, :] * carry[b] + u_scr[b, rows, :]
            y_ref[b, rows, :] = (hg * _silu(zd_ref[b, rows, :])).astype(y_ref.dtype)
            carry[b] = hg[SUBLANES - 1:SUBLANES, :]
    for b in range(nb):
        h_scr[b] = carry[b]

    @pl.when(c == pl.num_programs(1) - 1)
    def _():
        hout_ref[...] = h_scr[...]
        cout_ref[...] = xp_scr[:, CONV_TAIL:CONV_PAD, :]


def _lru(rest, buf, h0, cw, cb, wa, ba, wx, bx, lam, B, L, out_dtype):
    q = min(L, 256)
    nc = L // q
    nb = _batch_group(B, MIXER_SEQS)
    kern = functools.partial(_lru_kernel, q=q, nb=nb)
    full = lambda shape: pl.BlockSpec(shape, lambda b, c: (0,) * len(shape))
    rest3 = rest.reshape(B, L, REST_COLS)
    return pl.pallas_call(
        kern,
        grid=(B // nb, nc),
        in_specs=[
            pl.BlockSpec((nb, q, W_D), lambda b, c: (b, c, 7)),
            pl.BlockSpec((nb, q, W_D), lambda b, c: (b, c, 8)),
            pl.BlockSpec((nb, CONV_W - 1, W_D), lambda b, c: (b, 0, 0)),
            pl.BlockSpec((nb, 1, W_D), lambda b, c: (b, 0, 0)),
            full((CONV_W, W_D)), full((1, W_D)),
            full((W_D, W_D)), full((1, W_D)), full((W_D, W_D)), full((1, W_D)), full((1, W_D)),
        ],
        out_specs=(
            pl.BlockSpec((nb, q, W_D), lambda b, c: (b, c, 0)),
            pl.BlockSpec((nb, 1, W_D), lambda b, c: (b, 0, 0)),
            pl.BlockSpec((nb, CONV_W - 1, W_D), lambda b, c: (b, 0, 0)),
        ),
        out_shape=(
            jax.ShapeDtypeStruct((B, L, W_D), out_dtype),
            jax.ShapeDtypeStruct((B, 1, W_D), F32),
            jax.ShapeDtypeStruct((B, CONV_W - 1, W_D), F32),
        ),
        scratch_shapes=[
            pltpu.VMEM((nb, CONV_PAD, W_D), F32),
            pltpu.VMEM((nb, 1, W_D), F32),
            pltpu.VMEM((nb, q, W_D), F32),
            pltpu.VMEM((nb, q, W_D), F32),
        ],
        compiler_params=_cparams(("arbitrary", "arbitrary")),
        name="lru",
    )(rest3, rest3, buf, h0, cw, cb, wa, ba, wx, bx, lam)


def _outproj_kernel(ya_ref, yb_ref, yc_ref, yd_ref, w_ref, x_ref, fw_ref, o_ref, *, final):
    out = _dot(ya_ref[...].astype(BF16), w_ref[0])
    for g, y_ref in enumerate((yb_ref, yc_ref, yd_ref), start=1):
        out = out + _dot(y_ref[...].astype(BF16), w_ref[g])
    res = x_ref[...] + out
    if final:
        res = _rmsnorm_rows(res, fw_ref[...])
    o_ref[...] = res


def _outproj(ys, w_out4, x2d, final_w, final):
    T = x2d.shape[0]
    tm = min(512, T)
    kern = functools.partial(_outproj_kernel, final=final)
    yspec = pl.BlockSpec((tm, W_GROUP), lambda i: (i, 0))
    return pl.pallas_call(
        kern,
        grid=(T // tm,),
        in_specs=[yspec] * 4 + [
            pl.BlockSpec((4, W_GROUP, D_MODEL), lambda i: (0, 0, 0)),
            pl.BlockSpec((tm, D_MODEL), lambda i: (i, 0)),
            pl.BlockSpec((1, D_MODEL), lambda i: (0, 0)),
        ],
        out_specs=pl.BlockSpec((tm, D_MODEL), lambda i: (i, 0)),
        out_shape=jax.ShapeDtypeStruct((T, D_MODEL), F32),
        compiler_params=_cparams(("arbitrary",)),
        name="outproj",
    )(*ys, w_out4, x2d, final_w)


def _rope_table(pos, rows):
    half = ROPE_DIM // 2
    inv = ROPE_THETA ** (-jnp.arange(half, dtype=F32) / half)
    ang = pos.astype(F32)[:, None] * inv[None, :]
    cos, sin = jnp.cos(ang), jnp.sin(ang)
    n = pos.shape[0]
    pad = jnp.zeros((n, DK_A - ROPE_DIM), F32)
    c64 = jnp.concatenate([cos, cos, pad + 1.0], axis=1)
    s1 = jnp.concatenate([-sin, jnp.zeros_like(sin), pad], axis=1)
    s2 = jnp.concatenate([jnp.zeros_like(sin), sin, pad], axis=1)
    tab = jnp.stack([jnp.tile(t, (1, LANES // DK_A)) for t in (c64, s1, s2)])
    return jnp.tile(tab, (1, rows // n, 1))


def _block_diag(w):
    nb, bw, _ = w.shape
    eye = jnp.eye(nb, dtype=w.dtype)
    return (eye[:, None, :, None] * w[:, :, None, :]).reshape(nb * bw, nb * bw)


def _prep_layer(l, norm_w, w_in, w_out, lam_q1, lam_k1, lam_q2, lam_k2, subln_w, ssd_conv_w,
                ssd_conv_b, ssd_dt_bias, ssd_a_log, ssd_d, ssd_norm_w, gla_gk_w2, gla_gk_b,
                gla_norm_w, lru_conv_w, lru_conv_b, lru_wa, lru_ba, lru_wx, lru_bx, lru_lambda):
    w = w_in[l]
    cols = lambda a, n: w[:, a:a + n]
    w_main = jnp.stack([
        cols(_QA, 512), cols(_KA, 512), cols(_VA, 512), cols(_XBC, 512), cols(_XBC + 512, 512),
        cols(_ZA, 512), cols(_ZB, 512), jnp.concatenate([cols(_QC, 256), cols(_KC, 256)], axis=1),
        cols(_VC, 512), cols(_GC, 512), cols(_XD, 512), cols(_ZD, 512)]).astype(BF16)
    w_small = jnp.concatenate([cols(_DT, H_B), cols(_GKLR, GK_RANK),
                               jnp.zeros((D_MODEL, LANES - H_B - GK_RANK), F32)], axis=1).astype(BF16)
    lam_init = 0.8 - 0.6 * math.exp(-0.3 * l)
    lam = (jnp.exp(jnp.sum(lam_q1[l] * lam_k1[l])) - jnp.exp(jnp.sum(lam_q2[l] * lam_k2[l]))
           + lam_init).reshape(1, 1).astype(F32)
    pad_lanes = lambda v: jnp.concatenate([v, jnp.zeros((LANES - v.shape[0],), F32)]).reshape(1, LANES)
    w2p = jnp.zeros((LANES, H_C * DK_C), F32).at[H_B:H_B + GK_RANK].set(gla_gk_w2[l]).astype(BF16)
    return dict(
        norm_w=norm_w[l].reshape(1, D_MODEL), w_main=w_main, w_small=w_small,
        w_out4=w_out[l].reshape(4, W_GROUP, D_MODEL).astype(BF16),
        lam=lam, lam_init=lam_init, subln_w=subln_w[l].reshape(1, DV_A),
        ssd_cw=ssd_conv_w[l], ssd_cb=ssd_conv_b[l].reshape(1, CONV_B_CH),
        ssd_dtb=pad_lanes(ssd_dt_bias[l]), ssd_alog=pad_lanes(ssd_a_log[l]),
        ssd_dvec=jnp.repeat(ssd_d[l], P_B).reshape(1, W_GROUP), ssd_nw=ssd_norm_w[l].reshape(1, W_GROUP),
        gla_w2p=w2p, gla_gb=gla_gk_b[l].reshape(1, H_C * DK_C), gla_nw=gla_norm_w[l].reshape(1, DV_C),
        lru_cw=lru_conv_w[l], lru_cb=lru_conv_b[l].reshape(1, W_D),
        lru_wa=_block_diag(lru_wa[l]).astype(BF16), lru_ba=lru_ba[l].reshape(1, W_D),
        lru_wx=_block_diag(lru_wx[l]).astype(BF16), lru_bx=lru_bx[l].reshape(1, W_D),
        lru_lam=lru_lambda[l].reshape(1, W_D),
    )


def _gla_state_to_pairs(s):
    B = s.shape[0]
    return s.reshape(B, H_C // 2, 2, DK_C, DV_C).transpose(0, 1, 4, 2, 3).reshape(
        B, H_C // 2, DV_C, 2 * DK_C)


def _gla_state_from_pairs(s):
    B = s.shape[0]
    return s.reshape(B, H_C // 2, DV_C, 2, DK_C).transpose(0, 1, 3, 4, 2).reshape(B, H_C, DK_C, DV_C)


def _ssd_args(rest, small, states, p):
    return (rest, small, states[1], states[0], p["ssd_cw"], p["ssd_cb"], p["ssd_dtb"], p["ssd_alog"],
            p["ssd_dvec"], p["ssd_nw"])


def _mixers(rest, small, states, p, B, L, ydt, with_ssd=True):
    _, _, gla_s0, lru_h0, lru_buf = states
    out = {}
    if with_ssd:
        out["yb"], out["ssd_s"], out["ssd_c"] = _ssd(*_ssd_args(rest, small, states, p), B, L, ydt)
    yc, gla_st = _gla(rest, small, _gla_state_to_pairs(gla_s0), p["gla_w2p"], p["gla_gb"],
                      p["gla_nw"], B, L, ydt)
    yd, lru_h, lru_c = _lru(rest, lru_buf, lru_h0.reshape(B, 1, W_D), p["lru_cw"], p["lru_cb"],
                            p["lru_wa"], p["lru_ba"], p["lru_wx"], p["lru_bx"], p["lru_lam"], B, L, ydt)
    out.update(yc=yc, yd=yd, gla_s=_gla_state_from_pairs(gla_st), lru_h=lru_h.reshape(B, W_D),
               lru_c=lru_c)
    return out


def _finish_layer(x2d, ya, m, p, T, final_w, final):
    ys = (ya, m["yb"].reshape(T, W_GROUP), m["yc"].reshape(T, W_GROUP), m["yd"].reshape(T, W_D))
    x_new = _outproj(ys, p["w_out4"], x2d, final_w, final)
    return x_new, (m["ssd_s"], m["ssd_c"], m["gla_s"], m["lru_h"], m["lru_c"])


def _layer_pair(hp, hs, Bp, Lp, Bs, Ls, tab_p, tab_s, p, st_p, st_s, past, layer, depth, kv_p, kv_s,
                final_w, final):
    cache_k, cache_v, page_table = past
    ydt_p = BF16 if Lp % 16 == 0 else F32
    ydt_s = BF16 if Ls % 16 == 0 else F32
    q_p, k_p, v_p, rest_p, small_p, kf_p, vf_p = _inproj(
        hp, p["norm_w"], p["w_main"], p["w_small"], tab_p, Lp, layer, depth, kv_p)
    q_s, k_s, v_s, rest_s, small_s, kf_s, vf_s = _inproj(
        hs, p["norm_w"], p["w_main"], p["w_small"], tab_s, Ls, layer, depth, kv_s)
    yb_p, ssd_s_p, ssd_c_p, ya_s = _ssd_decode(
        _ssd_args(rest_p, small_p, st_p, p), Bp, Lp, ydt_p, p["lam"], q_s, k_s, v_s, rest_s,
        p["subln_w"], cache_k, cache_v, page_table, layer, Bs, Ls, p["lam_init"])
    ya_p = _attn_prompt(p["lam"], q_p, k_p, v_p, rest_p, p["subln_w"], Bp, Lp, p["lam_init"])
    m_p = _mixers(rest_p, small_p, st_p, p, Bp, Lp, ydt_p, with_ssd=False)
    m_p.update(yb=yb_p, ssd_s=ssd_s_p, ssd_c=ssd_c_p)
    m_s = _mixers(rest_s, small_s, st_s, p, Bs, Ls, ydt_s)
    hp, new_p = _finish_layer(hp, ya_p, m_p, p, Bp * Lp, final_w, final)
    hs, new_s = _finish_layer(hs, ya_s, m_s, p, Bs * Ls, final_w, final)
    return hp, hs, new_p, new_s, (kf_p, vf_p), (kf_s, vf_s)


def kernel(x_prompt, x_sample, cache_k, cache_v, page_table, state_ssd, state_ssd_conv, state_gla, state_lru, state_lru_conv, norm_w, w_in, w_out, lam_q1, lam_k1, lam_q2, lam_k2, subln_w, ssd_conv_w, ssd_conv_b, ssd_dt_bias, ssd_a_log, ssd_d, ssd_norm_w, gla_gk_w2, gla_gk_b, gla_norm_w, lru_conv_w, lru_conv_b, lru_wa, lru_ba, lru_wx, lru_bx, lru_lambda, final_norm_w):
    Bp, Lp, _ = x_prompt.shape
    Bs, Ls, _ = x_sample.shape
    depth = w_in.shape[0]
    past_len = page_table.shape[1] * cache_k.shape[2]
    tab_p = _rope_table(jnp.arange(Lp, dtype=jnp.int32), max(Lp, min(INPROJ_ROWS, Bp * Lp)))
    tab_s = _rope_table(past_len + jnp.arange(Ls, dtype=jnp.int32), max(Ls, min(INPROJ_ROWS, Bs * Ls)))
    final_w = final_norm_w.reshape(1, D_MODEL)
    hp = x_prompt.reshape(Bp * Lp, D_MODEL)
    hs = x_sample.reshape(Bs * Ls, D_MODEL)
    sp, ss = [], []
    kv_p = tuple(jnp.zeros((depth, Bp * Lp * H_A, LANES), F32) for _ in range(2))
    kv_s = tuple(jnp.zeros((depth, Bs * Ls * H_A, LANES), F32) for _ in range(2))
    for l in range(depth):
        p = _prep_layer(l, norm_w, w_in, w_out, lam_q1, lam_k1, lam_q2, lam_k2, subln_w, ssd_conv_w,
                        ssd_conv_b, ssd_dt_bias, ssd_a_log, ssd_d, ssd_norm_w, gla_gk_w2, gla_gk_b,
                        gla_norm_w, lru_conv_w, lru_conv_b, lru_wa, lru_ba, lru_wx, lru_bx, lru_lambda)
        final = l == depth - 1
        st_p = (jnp.zeros((Bp, H_B, P_B, N_B), F32), jnp.zeros((Bp, CONV_W - 1, CONV_B_CH), F32),
                jnp.zeros((Bp, H_C, DK_C, DV_C), F32), jnp.zeros((Bp, W_D), F32),
                jnp.zeros((Bp, CONV_W - 1, W_D), F32))
        st_s = (state_ssd[l], state_ssd_conv[l], state_gla[l], state_lru[l], state_lru_conv[l])
        hp, hs, new_p, new_s, kv_p, kv_s = _layer_pair(
            hp, hs, Bp, Lp, Bs, Ls, tab_p, tab_s, p, st_p, st_s, (cache_k, cache_v, page_table), l,
            depth, kv_p, kv_s, final_w, final)
        sp.append(new_p)
        ss.append(new_s)
    outs_p = [x.reshape(depth, Bp, Lp, H_A, LANES) for x in kv_p]
    outs_p += [jnp.stack([s[i] for s in sp]) for i in range(5)]
    outs_s = [x.reshape(depth, Bs, Ls, H_A, LANES) for x in kv_s]
    outs_s += [jnp.stack([s[i] for s in ss]) for i in range(5)]
    return (hp.reshape(Bp, Lp, D_MODEL), hs.reshape(Bs, Ls, D_MODEL), *outs_p, *outs_s)
```
